```python
import jax
import jax.numpy as jnp
from jax import lax
import numpy as np

D_MODEL = 1024
BATCH = 8
SEQ = 2048
DEPTH = 2
DEC_BATCH = 32
DEC_SEQ = 64
PAST_LEN = 1024

CHUNK = 64
Q_BLOCK = 128
ROPE_THETA = 10000.0
NORM_EPS = 1e-6
NEG_INF = -1e30

A_HEADS = 8
A_D_NOPE = 64
A_D_ROPE = 32
A_D_QK = A_D_NOPE + A_D_ROPE
A_D_V = 64
A_D_CQ = 256
A_D_C = 128
A_COLS = A_D_CQ + A_D_C + A_D_ROPE
B_HEADS = 8
B_HD = 64
B_WIDTH = B_HEADS * B_HD
B_W_LORA = 64
B_A_LORA = 64
B_G_LORA = 128
B_PROJ = 3 * B_WIDTH + B_W_LORA + B_A_LORA + B_G_LORA
B_LN_EPS = 64e-5
C_HEADS = 8
C_KV_HEADS = 2
C_HD = 64
C_IDX_HEADS = 8
C_IDX_D = 64
C_TOPK = 256
C_COLS = C_HEADS * C_HD + 2 * C_KV_HEADS * C_HD + C_IDX_HEADS * C_IDX_D + C_IDX_D + C_IDX_HEADS
D_GROUPS = 4
D_WIDTH = 512
D_SPAN = 128
D_COLS = 2 * D_WIDTH
PEER_HEADS = 8
PEER_N_KEYS = 128
PEER_N_EXPERTS = PEER_N_KEYS * PEER_N_KEYS
PEER_DK = 128
PEER_TOPK = 16
PEER_BLOCK = 128

L0_IN = A_COLS + B_PROJ
L1_IN = C_COLS + D_COLS
L0_OUT_IN = A_HEADS * A_D_V + B_WIDTH
L1_OUT_IN = C_HEADS * C_HD + D_WIDTH

kernel_name = 'hybrid_mla_rwkv7_dsa_gmlp_peer_stream_step'


def rmsnorm(x, g, eps=NORM_EPS):
    xf = x.astype(jnp.float32)
    y = xf * lax.rsqrt(jnp.mean(xf * xf, axis=-1, keepdims=True) + eps)
    return (y * g.astype(jnp.float32)).astype(x.dtype)


def layernorm(x, g, b, eps):
    xf = x.astype(jnp.float32)
    mu = jnp.mean(xf, axis=-1, keepdims=True)
    var = jnp.mean(jnp.square(xf - mu), axis=-1, keepdims=True)
    y = (xf - mu) * lax.rsqrt(var + eps) * g.astype(jnp.float32) + b.astype(jnp.float32)
    return y.astype(x.dtype)


def split_cols(p, sizes):
    return jnp.split(p, [int(i) for i in np.cumsum(sizes)[:-1]], axis=-1)


def rope(x, pos):
    half = x.shape[-1] // 2
    inv_freq = ROPE_THETA ** (-jnp.arange(half, dtype=jnp.float32) / half)
    ang = pos.astype(jnp.float32)[:, None] * inv_freq[None, :]
    cos = jnp.cos(ang)[:, None, :]
    sin = jnp.sin(ang)[:, None, :]
    xf = x.astype(jnp.float32)
    x1, x2 = xf[..., :half], xf[..., half:]
    return jnp.concatenate([x1 * cos - x2 * sin, x1 * sin + x2 * cos], axis=-1).astype(x.dtype)


def chunk_mask(qpos, kpos):
    return (kpos[None, :] // CHUNK) <= (qpos[:, None] // CHUNK)


def query_blocks(s):
    qb = Q_BLOCK if s % Q_BLOCK == 0 else s
    return qb, s // qb


def to_blocks(a, qb):
    b, s = a.shape[:2]
    return jnp.swapaxes(a.reshape((b, s // qb, qb) + a.shape[2:]), 0, 1)


def from_blocks(o):
    nb, b, qb = o.shape[:3]
    return jnp.swapaxes(o, 0, 1).reshape((b, nb * qb) + o.shape[3:])


def chunk_causal_attention(q, k, v, qpos, kpos):
    s_len, dq = q.shape[1], q.shape[-1]
    qb, nb = query_blocks(s_len)
    scale = dq ** -0.5

    def block(args):
        qblk, qp = args
        s = jnp.einsum('bqhd,bkhd->bhqk', qblk, k).astype(jnp.float32) * scale
        s = jnp.where(chunk_mask(qp, kpos)[None, None], s, NEG_INF)
        p = jax.nn.softmax(s, axis=-1).astype(v.dtype)
        return jnp.einsum('bhqk,bkhd->bqhd', p, v)

    return from_blocks(lax.map(block, (to_blocks(q, qb), qpos.reshape(nb, qb))))


def mla_mixer(p_a, qpos, kpos, cache_ckv, cache_krope, a_q_norm, a_w_uq, a_kv_norm, a_w_ukv, a_q_gain, a_k_gain):
    b, s, _ = p_a.shape
    cq, ckv, krope = split_cols(p_a, [A_D_CQ, A_D_C, A_D_ROPE])
    q = (rmsnorm(cq, a_q_norm) @ a_w_uq).reshape(b, s, A_HEADS, A_D_QK)
    ckv = rmsnorm(ckv, a_kv_norm)
    ckv_all = jnp.concatenate([cache_ckv, ckv], axis=1)
    kr_all = jnp.concatenate([cache_krope, krope], axis=1)
    l = ckv_all.shape[1]
    kv = (ckv_all @ a_w_ukv).reshape(b, l, A_HEADS, A_D_NOPE + A_D_V)
    k_nope, v = kv[..., :A_D_NOPE], kv[..., A_D_NOPE:]
    k = jnp.concatenate([k_nope, jnp.broadcast_to(kr_all[:, :, None, :], (b, l, A_HEADS, A_D_ROPE))], axis=-1)
    q = rmsnorm(q, a_q_gain)
    k = rmsnorm(k, a_k_gain)
    q = jnp.concatenate([q[..., :A_D_NOPE], rope(q[..., A_D_NOPE:], qpos)], axis=-1)
    k = jnp.concatenate([k[..., :A_D_NOPE], rope(k[..., A_D_NOPE:], kpos)], axis=-1)
    out = chunk_causal_attention(q, k, v, qpos, kpos)
    return out.reshape(b, s, A_HEADS * A_D_V), ckv, krope


def rwkv7_mixer(p_b, shift_state, wkv_state, b_mu, b_w0, b_w2, b_a0, b_a2, b_g2, b_kk, b_ka, b_rk, b_ln_g, b_ln_b):
    b, s, _ = p_b.shape
    f32 = jnp.float32
    prev = jnp.concatenate([shift_state[:, None, :], p_b[:, :-1]], axis=1)
    xs = p_b + (prev - p_b) * b_mu
    r, k, v, wl, al, gl = split_cols(xs, [B_WIDTH, B_WIDTH, B_WIDTH, B_W_LORA, B_A_LORA, B_G_LORA])
    w_log = -jax.nn.softplus(-(b_w0 + jnp.tanh(wl) @ b_w2)) - 0.5
    decay = jnp.exp(-jnp.exp(w_log.astype(f32)))
    a = jax.nn.sigmoid(b_a0 + al @ b_a2)
    g = jax.nn.sigmoid(gl) @ b_g2

    def heads(t):
        return t.reshape(b, s, B_HEADS, B_HD).astype(f32)

    kk = heads(k * b_kk)
    kk = kk * lax.rsqrt(jnp.sum(kk * kk, axis=-1, keepdims=True) + 1e-12)
    k = k * (1.0 + (a - 1.0) * b_ka)
    rh, kh, vh, ah, wh = heads(r), heads(k), heads(v), heads(a), heads(decay)

    def step(state, inp):
        r_t, w_t, k_t, v_t, a_t, bb_t = inp
        sa = jnp.einsum('bhij,bhj->bhi', state, a_t)
        state = state * w_t[:, :, None, :] + sa[..., None] * bb_t[:, :, None, :] + v_t[..., None] * k_t[:, :, None, :]
        return state, jnp.einsum('bhij,bhj->bhi', state, r_t)

    seq = tuple(jnp.swapaxes(t, 0, 1) for t in (rh, wh, kh, vh, -kk, kk * ah))
    state, y = lax.scan(step, wkv_state.astype(f32), seq)
    y = jnp.swapaxes(y, 0, 1)
    y = layernorm(y, b_ln_g.reshape(B_HEADS, B_HD), b_ln_b.reshape(B_HEADS, B_HD), B_LN_EPS)
    bonus = jnp.sum(rh * kh * b_rk, axis=-1, keepdims=True) * vh
    out = ((y + bonus).reshape(b, s, B_WIDTH) * g).astype(p_b.dtype)
    return out, p_b[:, -1], state.astype(wkv_state.dtype)


def dsa_mixer(p_c, qpos, kpos, cache_k, cache_v, cache_kidx, c_q_gain, c_k_gain, c_kidx_gain):
    b, s, _ = p_c.shape
    f32 = jnp.float32
    q, k, v, iq, ik, iw = split_cols(p_c, [C_HEADS * C_HD, C_KV_HEADS * C_HD, C_KV_HEADS * C_HD,
                                           C_IDX_HEADS * C_IDX_D, C_IDX_D, C_IDX_HEADS])
    q = rope(rmsnorm(q.reshape(b, s, C_HEADS, C_HD), c_q_gain), qpos)
    k = rope(rmsnorm(k.reshape(b, s, C_KV_HEADS, C_HD), c_k_gain), qpos)
    v = v.reshape(b, s, C_KV_HEADS, C_HD)
    iq = rope(iq.reshape(b, s, C_IDX_HEADS, C_IDX_D), qpos)
    ik = rope(rmsnorm(ik, c_kidx_gain)[:, :, None, :], qpos)[:, :, 0]
    iw = iw * (C_IDX_HEADS ** -0.5)
    k_all = jnp.concatenate([cache_k, k], axis=1)
    v_all = jnp.concatenate([cache_v, v], axis=1)
    ik_all = jnp.concatenate([cache_kidx, ik], axis=1)
    l = k_all.shape[1]
    n_sel = min(C_TOPK, l // 4)
    rep = C_HEADS // C_KV_HEADS
    qb, nb = query_blocks(s)
    gather_rows = jax.vmap(lambda rows, idx: rows[idx])

    def block(args):
        qblk, iqblk, iwblk, qp = args
        logits = jnp.einsum('bqhd,bkd->bqhk', iqblk, ik_all).astype(f32) * (C_IDX_D ** -0.5)
        score = jnp.einsum('bqh,bqhk->bqk', iwblk.astype(f32), jax.nn.relu(logits))
        score = jnp.where(chunk_mask(qp, kpos)[None], score, NEG_INF)
        top_val, top_idx = lax.top_k(score, n_sel)
        valid = top_val > 0.5 * NEG_INF
        ksel = gather_rows(k_all, top_idx)
        vsel = gather_rows(v_all, top_idx)
        qg = qblk.reshape(b, qb, C_KV_HEADS, rep, C_HD)
        sc = jnp.einsum('bqcrd,bqkcd->bqcrk', qg, ksel).astype(f32) * (C_HD ** -0.5)
        sc = jnp.where(valid[:, :, None, None, :], sc, NEG_INF)
        p = jax.nn.softmax(sc, axis=-1).astype(vsel.dtype)
        o = jnp.einsum('bqcrk,bqkcd->bqcrd', p, vsel)
        return o.reshape(b, qb, C_HEADS * C_HD)

    out = from_blocks(lax.map(block, (to_blocks(q, qb), to_blocks(iq, qb), to_blocks(iw, qb), qpos.reshape(nb, qb))))
    return out, k, v, ik


def gmlp_mixer(p_d, d_ln_g, d_ln_b, d_ws, d_bs):
    b, s, _ = p_d.shape
    u, v = split_cols(jax.nn.gelu(p_d, approximate=False), [D_WIDTH, D_WIDTH])
    v = layernorm(v, d_ln_g, d_ln_b, NORM_EPS)
    span = D_SPAN if s % D_SPAN == 0 else s
    nc = s // span
    w = jnp.tril(d_ws[:, :span, :span])
    vg = v.reshape(b, nc, span, D_GROUPS, D_WIDTH // D_GROUPS)
    mixed = jnp.einsum('gts,bnsgc->bntgc', w, vg) + jnp.swapaxes(d_bs[:, :span], 0, 1)[:, :, None]
    return u * mixed.reshape(b, s, D_WIDTH), v


def peer_ffn(x, w_q, keys, u_tab, v_tab):
    b, s, d = x.shape
    t = b * s
    nb = -(-t // PEER_BLOCK)
    xt = jnp.pad(x.reshape(t, d), ((0, nb * PEER_BLOCK - t), (0, 0))).reshape(nb, PEER_BLOCK, d)
    half = PEER_DK // 2

    def block(xb):
        q = (xb @ w_q).reshape(PEER_BLOCK, PEER_HEADS, 2, half)
        sc = jnp.einsum('thpd,hpnd->thpn', q, keys).astype(jnp.float32)
        s1, i1 = lax.top_k(sc[:, :, 0], PEER_TOPK)
        s2, i2 = lax.top_k(sc[:, :, 1], PEER_TOPK)
        cand = (s1[..., :, None] + s2[..., None, :]).reshape(PEER_BLOCK, PEER_HEADS, PEER_TOPK * PEER_TOPK)
        cidx = (i1[..., :, None] * PEER_N_KEYS + i2[..., None, :]).reshape(PEER_BLOCK, PEER_HEADS, PEER_TOPK * PEER_TOPK)
        top_s, top_pos = lax.top_k(cand, PEER_TOPK)
        eidx = jnp.take_along_axis(cidx, top_pos, axis=-1)
        gate = jax.nn.softmax(top_s, axis=-1)
        ue = u_tab[eidx]
        ve = v_tab[eidx]
        h = jax.nn.gelu(jnp.einsum('td,thkd->thk', xb, ue), approximate=False)
        return jnp.einsum('thk,thkd->td', (gate * h).astype(ve.dtype), ve)

    out = lax.map(block, xt).reshape(nb * PEER_BLOCK, d)[:t]
    return out.reshape(b, s, d)


def even_layer(x, qpos, kpos, cache_ckv, cache_krope, state_wkv, state_shift, w):
    (norm_mix, w_in, a_q_norm, a_w_uq, a_kv_norm, a_w_ukv, a_q_gain, a_k_gain,
     b_mu, b_w0, b_w2, b_a0, b_a2, b_g2, b_kk, b_ka, b_rk, b_ln_g, b_ln_b, w_out) = w
    proj = rmsnorm(x, norm_mix) @ w_in
    p_a, p_b = proj[..., :A_COLS], proj[..., A_COLS:]
    o_a, ckv, krope = mla_mixer(p_a, qpos, kpos, cache_ckv, cache_krope,
                                a_q_norm, a_w_uq, a_kv_norm, a_w_ukv, a_q_gain, a_k_gain)
    o_b, shift, wkv = rwkv7_mixer(p_b, state_shift, state_wkv, b_mu, b_w0, b_w2, b_a0, b_a2,
                                  b_g2, b_kk, b_ka, b_rk, b_ln_g, b_ln_b)
    x = x + jnp.concatenate([o_a, o_b], axis=-1) @ w_out
    return x, [ckv, krope, wkv, shift]


def odd_layer(x, qpos, kpos, cache_k, cache_v, cache_kidx, w):
    (norm_mix, w_in, c_q_gain, c_k_gain, c_kidx_gain, d_ln_g, d_ln_b, d_ws, d_bs, w_out) = w
    proj = rmsnorm(x, norm_mix) @ w_in
    p_c, p_d = proj[..., :C_COLS], proj[..., C_COLS:]
    o_c, k, v, kidx = dsa_mixer(p_c, qpos, kpos, cache_k, cache_v, cache_kidx, c_q_gain, c_k_gain, c_kidx_gain)
    o_d, d_v = gmlp_mixer(p_d, d_ln_g, d_ln_b, d_ws, d_bs)
    x = x + jnp.concatenate([o_c, o_d], axis=-1) @ w_out
    return x, [k, v, kidx, d_v]


def run_trunk(x, states, mix_w, ffn_w):
    s = x.shape[1]
    past = states[0][0].shape[1]
    qpos = past + jnp.arange(s, dtype=jnp.int32)
    kpos = jnp.arange(past + s, dtype=jnp.int32)
    new_state = []
    for layer in range(DEPTH):
        if layer % 2 == 0:
            x, st = even_layer(x, qpos, kpos, *states[layer], mix_w[layer])
        else:
            x, st = odd_layer(x, qpos, kpos, *states[layer], mix_w[layer])
        new_state.extend(st)
        norm_ffn, p_wq, p_keys, p_u, p_v = ffn_w[layer]
        x = x + peer_ffn(rmsnorm(x, norm_ffn), p_wq, p_keys, p_u, p_v)
    return x, new_state


def setup_inputs(seed: int = 0) -> dict:
    key = jax.random.key(seed)
    ks = jax.random.split(key, 64)
    f32 = jnp.float32
    d = D_MODEL

    def nrm(i, shape, scale):
        return jax.random.normal(ks[i], shape, f32) * scale

    def gain(i, n):
        return 1.0 + nrm(i, (n,), 0.02)

    return {
        'x_prompt': nrm(0, (BATCH, SEQ, d), 1.0),
        'x_sample': nrm(1, (DEC_BATCH, DEC_SEQ, d), 1.0),
        'cache_a_ckv': nrm(2, (DEC_BATCH, PAST_LEN, A_D_C), 1.0),
        'cache_a_krope': nrm(3, (DEC_BATCH, PAST_LEN, A_D_ROPE), 1.0),
        'state_b_wkv': nrm(4, (DEC_BATCH, B_HEADS, B_HD, B_HD), 0.5),
        'state_b_shift': nrm(5, (DEC_BATCH, B_PROJ), 1.0),
        'cache_c_k': nrm(6, (DEC_BATCH, PAST_LEN, C_KV_HEADS, C_HD), 1.0),
        'cache_c_v': nrm(7, (DEC_BATCH, PAST_LEN, C_KV_HEADS, C_HD), 1.0),
        'cache_c_kidx': nrm(8, (DEC_BATCH, PAST_LEN, C_IDX_D), 1.0),
        'l0_norm_mix': gain(9, d),
        'l0_w_in': nrm(10, (d, L0_IN), d ** -0.5),
        'a_q_norm': gain(11, A_D_CQ),
        'a_w_uq': nrm(12, (A_D_CQ, A_HEADS * A_D_QK), A_D_CQ ** -0.5),
        'a_kv_norm': gain(13, A_D_C),
        'a_w_ukv': nrm(14, (A_D_C, A_HEADS * (A_D_NOPE + A_D_V)), A_D_C ** -0.5),
        'a_q_gain': gain(15, A_D_QK),
        'a_k_gain': gain(16, A_D_QK),
        'b_mu': jax.random.uniform(ks[17], (B_PROJ,), f32),
        'b_w0': -2.0 + nrm(18, (B_WIDTH,), 1.0),
        'b_w2': nrm(19, (B_W_LORA, B_WIDTH), 0.1),
        'b_a0': nrm(20, (B_WIDTH,), 0.1),
        'b_a2': nrm(21, (B_A_LORA, B_WIDTH), 0.1),
        'b_g2': nrm(22, (B_G_LORA, B_WIDTH), B_G_LORA ** -0.5),
        'b_kk': 0.85 + nrm(23, (B_WIDTH,), 0.02),
        'b_ka': gain(24, B_WIDTH),
        'b_rk': nrm(25, (B_HEADS, B_HD), 0.1),
        'b_ln_g': gain(26, B_WIDTH),
        'b_ln_b': nrm(27, (B_WIDTH,), 0.02),
        'l0_w_out': nrm(28, (L0_OUT_IN, d), L0_OUT_IN ** -0.5),
        'l0_norm_ffn': gain(29, d),
        'l0_peer_wq': nrm(30, (d, PEER_HEADS * PEER_DK), d ** -0.5),
        'l0_peer_keys': nrm(31, (PEER_HEADS, 2, PEER_N_KEYS, PEER_DK // 2), (PEER_DK // 2) ** -0.5),
        'l0_peer_u': nrm(32, (PEER_N_EXPERTS, d), d ** -0.5),
        'l0_peer_v': nrm(33, (PEER_N_EXPERTS, d), PEER_HEADS ** -0.5),
        'l1_norm_mix': gain(34, d),
        'l1_w_in': nrm(35, (d, L1_IN), d ** -0.5),
        'c_q_gain': gain(36, C_HD),
        'c_k_gain': gain(37, C_HD),
        'c_kidx_gain': gain(38, C_IDX_D),
        'd_ln_g': gain(39, D_WIDTH),
        'd_ln_b': nrm(40, (D_WIDTH,), 0.02),
        'd_ws': nrm(41, (D_GROUPS, D_SPAN, D_SPAN), D_SPAN ** -0.5),
        'd_bs': 1.0 + nrm(42, (D_GROUPS, D_SPAN), 0.02),
        'l1_w_out': nrm(43, (L1_OUT_IN, d), L1_OUT_IN ** -0.5),
        'l1_norm_ffn': gain(44, d),
        'l1_peer_wq': nrm(45, (d, PEER_HEADS * PEER_DK), d ** -0.5),
        'l1_peer_keys': nrm(46, (PEER_HEADS, 2, PEER_N_KEYS, PEER_DK // 2), (PEER_DK // 2) ** -0.5),
        'l1_peer_u': nrm(47, (PEER_N_EXPERTS, d), d ** -0.5),
        'l1_peer_v': nrm(48, (PEER_N_EXPERTS, d), PEER_HEADS ** -0.5),
    }


def reference(x_prompt, x_sample, cache_a_ckv, cache_a_krope, state_b_wkv, state_b_shift,
              cache_c_k, cache_c_v, cache_c_kidx,
              l0_norm_mix, l0_w_in, a_q_norm, a_w_uq, a_kv_norm, a_w_ukv, a_q_gain, a_k_gain,
              b_mu, b_w0, b_w2, b_a0, b_a2, b_g2, b_kk, b_ka, b_rk, b_ln_g, b_ln_b,
              l0_w_out, l0_norm_ffn, l0_peer_wq, l0_peer_keys, l0_peer_u, l0_peer_v,
              l1_norm_mix, l1_w_in, c_q_gain, c_k_gain, c_kidx_gain, d_ln_g, d_ln_b, d_ws, d_bs,
              l1_w_out, l1_norm_ffn, l1_peer_wq, l1_peer_keys, l1_peer_u, l1_peer_v):
    mix_w = (
        (l0_norm_mix, l0_w_in, a_q_norm, a_w_uq, a_kv_norm, a_w_ukv, a_q_gain, a_k_gain,
         b_mu, b_w0, b_w2, b_a0, b_a2, b_g2, b_kk, b_ka, b_rk, b_ln_g, b_ln_b, l0_w_out),
        (l1_norm_mix, l1_w_in, c_q_gain, c_k_gain, c_kidx_gain, d_ln_g, d_ln_b, d_ws, d_bs, l1_w_out),
    )
    ffn_w = (
        (l0_norm_ffn, l0_peer_wq, l0_peer_keys, l0_peer_u, l0_peer_v),
        (l1_norm_ffn, l1_peer_wq, l1_peer_keys, l1_peer_u, l1_peer_v),
    )
    bp = x_prompt.shape[0]
    dt = x_prompt.dtype
    prompt_states = (
        (jnp.zeros((bp, 0, A_D_C), dt), jnp.zeros((bp, 0, A_D_ROPE), dt),
         jnp.zeros((bp, B_HEADS, B_HD, B_HD), dt), jnp.zeros((bp, B_PROJ), dt)),
        (jnp.zeros((bp, 0, C_KV_HEADS, C_HD), dt), jnp.zeros((bp, 0, C_KV_HEADS, C_HD), dt),
         jnp.zeros((bp, 0, C_IDX_D), dt)),
    )
    sample_states = (
        (cache_a_ckv, cache_a_krope, state_b_wkv, state_b_shift),
        (cache_c_k, cache_c_v, cache_c_kidx),
    )
    y_prompt, new_p = run_trunk(x_prompt, prompt_states, mix_w, ffn_w)
    y_sample, new_s = run_trunk(x_sample, sample_states, mix_w, ffn_w)
    a_ckv_p, a_krope_p, b_wkv_p, b_shift_p, c_k_p, c_v_p, c_kidx_p, _d_v_p = new_p
    a_ckv_s, a_krope_s, b_wkv_s, b_shift_s, c_k_s, c_v_s, c_kidx_s, d_v_s = new_s
    return (y_prompt, y_sample,
            a_ckv_p, a_krope_p, b_wkv_p, b_shift_p, c_k_p, c_v_p, c_kidx_p,
            a_ckv_s, a_krope_s, b_wkv_s, b_shift_s, c_k_s, c_v_s, c_kidx_s, d_v_s)
```

```python
import functools
import math

import jax
import jax.numpy as jnp
from jax import lax
from jax.experimental import pallas as pl
from jax.experimental.pallas import tpu as pltpu

F32 = jnp.float32
BF16 = jnp.bfloat16
I32 = jnp.int32

D_MODEL = 1024
CHUNK = 64
CHUNK_SHIFT = 6
ROPE_THETA = 10000.0
NORM_EPS = 1e-6
NEG_INF = -1e30
LANES = 128
SUBLANES = 8

A_HEADS = 8
A_D_NOPE = 64
A_D_ROPE = 32
A_D_QK = A_D_NOPE + A_D_ROPE
A_D_V = 64
A_D_CQ = 256
A_D_C = 128
B_HEADS = 8
B_HD = 64
B_WIDTH = B_HEADS * B_HD
B_W_LORA = 64
B_A_LORA = 64
B_G_LORA = 128
B_PROJ = 3 * B_WIDTH + B_W_LORA + B_A_LORA + B_G_LORA
B_LN_EPS = 64e-5
C_HEADS = 8
C_KV_HEADS = 2
C_HD = 64
C_IDX_HEADS = 8
C_IDX_D = 64
C_TOPK = 256
D_GROUPS = 4
D_WIDTH = 512
D_SPAN = 128
PEER_HEADS = 8
PEER_N_KEYS = 128
PEER_N_EXPERTS = PEER_N_KEYS * PEER_N_KEYS
PEER_DK = 128
PEER_TOPK = 16

VMEM_LIMIT = 48 * 1024 * 1024
NT = (((1,), (1,)), ((), ()))


def _tile(n, target):
    t = min(n, target)
    while n % t:
        t -= 8
    return t


def _cparams(*sem):
    return pltpu.CompilerParams(dimension_semantics=sem, vmem_limit_bytes=VMEM_LIMIT)


def _dot(a, b):
    return jnp.dot(a, b, preferred_element_type=F32)


def _dot_nt(a, b):
    return lax.dot_general(a, b, NT, preferred_element_type=F32)


def _split_dot(x, w, passes=2):
    hi = x.astype(BF16)
    acc = _dot(hi, w)
    rem = x - hi.astype(F32)
    for _ in range(passes - 1):
        part = rem.astype(BF16)
        acc = acc + _dot(part, w)
        rem = rem - part.astype(F32)
    return acc


def _rms(x, eps=NORM_EPS):
    return x * lax.rsqrt(jnp.mean(x * x, axis=-1, keepdims=True) + eps)


def _full(shape):
    nd = len(shape)
    return pl.BlockSpec(shape, lambda *_: (0,) * nd)


def _block_ones(n, seg):
    i = jnp.arange(n)
    return (i[:, None] // seg == i[None, :] // seg).astype(BF16)


def _rope_perm(n, seg, off, half):
    r = jnp.arange(n)[:, None]
    c = jnp.arange(n)[None, :]
    same = r // seg == c // seg
    rr = r % seg - off
    cc = c % seg - off
    plus = same & (rr >= 0) & (rr < half) & (cc == rr + half)
    minus = same & (cc >= 0) & (cc < half) & (rr == cc + half)
    return (plus.astype(F32) - minus.astype(F32)).astype(BF16)


def _rope_tables(pos, half, seg, off, reps):
    inv_freq = ROPE_THETA ** (-jnp.arange(half, dtype=F32) / half)
    ang = pos.astype(F32)[:, None] * inv_freq[None, :]
    n = pos.shape[0]
    cos = jnp.ones((n, seg), F32).at[:, off:off + 2 * half].set(jnp.tile(jnp.cos(ang), (1, 2)))
    sin = jnp.zeros((n, seg), F32).at[:, off:off + 2 * half].set(jnp.tile(jnp.sin(ang), (1, 2)))
    return jnp.tile(cos, (1, reps)), jnp.tile(sin, (1, reps))


def _in0_kernel(x_ref, g_ref, wcq_ref, wckv_ref, wkr_ref, wb_ref, kvn_ref,
                cq_ref, ckv_ref, krp_ref, pb_ref):
    xn = (_rms(x_ref[...]) * g_ref[...]).astype(BF16)
    cq_ref[...] = _dot(xn, wcq_ref[...])
    ckv_ref[...] = _rms(_dot(xn, wckv_ref[...])) * kvn_ref[...]
    krp_ref[...] = _dot(xn, wkr_ref[...])
    pb_ref[...] = _dot(xn, wb_ref[...])


def _in_proj0(x2, w):
    t = x2.shape[0]
    tm = _tile(t, 512)
    row = lambda n: pl.BlockSpec((tm, n), lambda i: (i, 0))
    return pl.pallas_call(
        _in0_kernel,
        grid=(t // tm,),
        in_specs=[row(D_MODEL), _full((1, D_MODEL)), _full((D_MODEL, A_D_CQ)), _full((D_MODEL, A_D_C)),
                  _full((D_MODEL, LANES)), _full((D_MODEL, B_PROJ)), _full((1, A_D_C))],
        out_specs=[row(A_D_CQ), row(A_D_C), row(LANES), row(B_PROJ)],
        out_shape=[jax.ShapeDtypeStruct((t, A_D_CQ), F32), jax.ShapeDtypeStruct((t, A_D_C), F32),
                   jax.ShapeDtypeStruct((t, LANES), F32), jax.ShapeDtypeStruct((t, B_PROJ), F32)],
        compiler_params=_cparams("parallel"),
        name="in_proj0",
    )(x2, w["l0_norm"], w["w_cq"], w["w_ckv"], w["w_krp"], w["w_b"], w["a_kv_norm"])


def _head_norm_rope(xh, gain, cos, sin, perm):
    ss = jnp.sum(xh * xh, axis=-1, keepdims=True) * (1.0 / A_D_QK)
    xh = xh * lax.rsqrt(ss + NORM_EPS) * gain
    return xh * cos + _split_dot(xh, perm) * sin


def _mla_q_kernel(cq_ref, qn_ref, wq_ref, gain_ref, cos_ref, sin_ref, perm_ref, o_ref):
    cqn = (_rms(cq_ref[0]) * qn_ref[...]).astype(BF16)
    q = _dot(cqn, wq_ref[...])
    cos, sin, perm, gain = cos_ref[...], sin_ref[...], perm_ref[...], gain_ref[...]
    for h in range(A_HEADS):
        qh = _head_norm_rope(q[:, h * LANES:(h + 1) * LANES], gain, cos, sin, perm)
        o_ref[0, :, h * LANES:(h + 1) * LANES] = (qh * (A_D_QK ** -0.5)).astype(BF16)


def _mla_q(cq, w, cos, sin):
    b, s, _ = cq.shape
    ts = _tile(s, 256)
    return pl.pallas_call(
        _mla_q_kernel,
        grid=(b, s // ts),
        in_specs=[pl.BlockSpec((1, ts, A_D_CQ), lambda i, j: (i, j, 0)), _full((1, A_D_CQ)),
                  _full((A_D_CQ, A_HEADS * LANES)), _full((1, LANES)),
                  pl.BlockSpec((ts, LANES), lambda i, j: (j, 0)), pl.BlockSpec((ts, LANES), lambda i, j: (j, 0)),
                  _full((LANES, LANES))],
        out_specs=pl.BlockSpec((1, ts, A_HEADS * LANES), lambda i, j: (i, j, 0)),
        out_shape=jax.ShapeDtypeStruct((b, s, A_HEADS * LANES), BF16),
        compiler_params=_cparams("parallel", "parallel"),
        name="mla_q",
    )(cq, w["a_q_norm"], w["w_uq"], w["a_q_gain"], cos, sin, w["perm_a"])


def _mla_kv_kernel(ckv_ref, krp_ref, wk_ref, wv_ref, gain_ref, cos_ref, sin_ref, perm_ref, k_ref, v_ref):
    c = ckv_ref[0].astype(BF16)
    k = _dot(c, wk_ref[...])
    v_ref[0] = _dot(c, wv_ref[...]).astype(BF16)
    krp = krp_ref[0]
    cos, sin, perm, gain = cos_ref[...], sin_ref[...], perm_ref[...], gain_ref[...]
    for h in range(A_HEADS):
        kh = _head_norm_rope(k[:, h * LANES:(h + 1) * LANES] + krp, gain, cos, sin, perm)
        k_ref[0, :, h * LANES:(h + 1) * LANES] = kh.astype(BF16)


def _mla_kv(ckv_all, krp_all, w, cos, sin):
    b, lp, _ = ckv_all.shape
    tl = _tile(lp, 256)
    blk = lambda n: pl.BlockSpec((1, tl, n), lambda i, j: (i, j, 0))
    tab = pl.BlockSpec((tl, LANES), lambda i, j: (j, 0))
    return pl.pallas_call(
        _mla_kv_kernel,
        grid=(b, lp // tl),
        in_specs=[blk(A_D_C), blk(LANES), _full((A_D_C, A_HEADS * LANES)), _full((A_D_C, A_HEADS * LANES)),
                  _full((1, LANES)), tab, tab, _full((LANES, LANES))],
        out_specs=[blk(A_HEADS * LANES), blk(A_HEADS * LANES)],
        out_shape=[jax.ShapeDtypeStruct((b, lp, A_HEADS * LANES), BF16)] * 2,
        compiler_params=_cparams("parallel", "parallel"),
        name="mla_kv",
    )(ckv_all, krp_all, w["w_uk"], w["w_uv"], w["a_k_gain"], cos, sin, w["perm_a"])


def _visible(tq, lp, past, true_len):
    qpos = past + pl.program_id(1) * tq + lax.broadcasted_iota(I32, (tq, 1), 0)
    kpos = lax.broadcasted_iota(I32, (1, lp), 1)
    kchunk = jnp.where(kpos < true_len, kpos >> CHUNK_SHIFT, jnp.int32(1 << 30))
    return kchunk <= (qpos >> CHUNK_SHIFT)


def _masked_softmax(s, keep):
    s = jnp.where(keep, s, NEG_INF)
    p = jnp.exp(s - jnp.max(s, axis=-1, keepdims=True))
    return p, 1.0 / jnp.sum(p, axis=-1, keepdims=True)


def _mla_attn_kernel(q_ref, k_ref, v_ref, o_ref, *, past, true_len):
    tq = q_ref.shape[1]
    lp = k_ref.shape[1]
    vis = _visible(tq, lp, past, true_len)
    for m in range(A_HEADS // 2):
        acc = None
        for h in (2 * m, 2 * m + 1):
            sl = slice(h * LANES, (h + 1) * LANES)
            p, inv = _masked_softmax(_dot_nt(q_ref[0, :, sl], k_ref[0, :, sl]), vis)
            o = _dot(p.astype(BF16), v_ref[0, :, sl]) * inv
            acc = o if acc is None else acc + o
        o_ref[0, :, m * LANES:(m + 1) * LANES] = acc


def _mla_attn(q, k, v, past, true_len):
    b, s, _ = q.shape
    lp = k.shape[1]
    tq = _tile(s, 128)
    kv = pl.BlockSpec((1, lp, A_HEADS * LANES), lambda i, j: (i, 0, 0))
    return pl.pallas_call(
        functools.partial(_mla_attn_kernel, past=past, true_len=true_len),
        grid=(b, s // tq),
        in_specs=[pl.BlockSpec((1, tq, A_HEADS * LANES), lambda i, j: (i, j, 0)), kv, kv],
        out_specs=pl.BlockSpec((1, tq, A_HEADS * A_D_V), lambda i, j: (i, j, 0)),
        out_shape=jax.ShapeDtypeStruct((b, s, A_HEADS * A_D_V), F32),
        compiler_params=_cparams("parallel", "arbitrary"),
        name="mla_attn",
    )(q, k, v)


def _rwkv_prep_kernel(pb_ref, prev_ref, mu_ref, w0_ref, w2_ref, a0_ref, a2_ref, g2_ref, kk_ref, ka_ref, rk_ref,
                      bd_ref, r_ref, w_ref, k_ref, v_ref, an_ref, bv_ref, g_ref, bonus_ref):
    pb = pb_ref[...]
    xs = pb + (prev_ref[...] - pb) * mu_ref[...]
    r = xs[:, 0:B_WIDTH]
    k = xs[:, B_WIDTH:2 * B_WIDTH]
    v = xs[:, 2 * B_WIDTH:3 * B_WIDTH]
    wa = xs[:, 3 * B_WIDTH:3 * B_WIDTH + B_W_LORA + B_A_LORA]
    gl = xs[:, 3 * B_WIDTH + B_W_LORA + B_A_LORA:]
    bd = bd_ref[...]
    z = -(w0_ref[...] + _dot(jnp.tanh(wa).astype(BF16), w2_ref[...]))
    softplus = jnp.maximum(z, 0.0) + jnp.log(1.0 + jnp.exp(-jnp.abs(z)))
    w_ref[...] = jnp.exp(-jnp.exp(-softplus - 0.5))
    a = jax.nn.sigmoid(a0_ref[...] + _dot(wa.astype(BF16), a2_ref[...]))
    g_ref[...] = _dot(jax.nn.sigmoid(gl).astype(BF16), g2_ref[...])
    kk = k * kk_ref[...]
    kk = kk * lax.rsqrt(_split_dot(kk * kk, bd) + 1e-12)
    k2 = k * (1.0 + (a - 1.0) * ka_ref[...])
    r_ref[...] = r
    k_ref[...] = k2
    v_ref[...] = v
    an_ref[...] = -kk
    bv_ref[...] = kk * a
    bonus_ref[...] = _split_dot(r * k2 * rk_ref[...], bd) * v


def _rwkv_prep(pb2, prev2, w):
    t = pb2.shape[0]
    tm = _tile(t, 256)
    row = lambda n: pl.BlockSpec((tm, n), lambda i: (i, 0))
    vec = _full((1, B_WIDTH))
    return pl.pallas_call(
        _rwkv_prep_kernel,
        grid=(t // tm,),
        in_specs=[row(B_PROJ), row(B_PROJ), _full((1, B_PROJ)), vec, _full((LANES, B_WIDTH)), vec,
                  _full((LANES, B_WIDTH)), _full((B_G_LORA, B_WIDTH)), vec, vec, vec, _full((B_WIDTH, B_WIDTH))],
        out_specs=[row(B_WIDTH)] * 8,
        out_shape=[jax.ShapeDtypeStruct((t, B_WIDTH), F32)] * 8,
        compiler_params=_cparams("parallel"),
        name="rwkv_prep",
    )(pb2, prev2, w["b_mu"], w["b_w0"], w["b_w2"], w["b_a0"], w["b_a2"], w["b_g2"], w["b_kk"], w["b_ka"],
      w["b_rk"], w["bd512"])


def _rwkv_scan_kernel(r_ref, w_ref, k_ref, v_ref, an_ref, bv_ref, s0_ref, bd_ref, y_ref, sout_ref, st_ref):
    nb, ts, _ = r_ref.shape
    pairs = B_HEADS // 2

    @pl.when(pl.program_id(1) == 0)
    def _():
        st_ref[...] = s0_ref[...]

    bd = bd_ref[...]
    lane = lax.broadcasted_iota(I32, (B_HD, LANES), 1)
    sub = lax.broadcasted_iota(I32, (B_HD, LANES), 0)
    diag = jnp.where((lane & (B_HD - 1)) == sub, 1.0, 0.0).astype(F32)[None]

    def seg_bcast(x, passes):
        return _split_dot(x.reshape(nb * B_HD, LANES), bd, passes).reshape(nb, B_HD, LANES)

    def steps(c, carry):
        t0 = pl.multiple_of(c * SUBLANES, SUBLANES)
        for p in range(pairs):
            sl = slice(p * LANES, (p + 1) * LANES)
            r, w, k, v, an, bv = [ref[:, pl.ds(t0, SUBLANES), sl] for ref in
                                  (r_ref, w_ref, k_ref, v_ref, an_ref, bv_ref)]
            st = st_ref[:, p]
            ys = []
            for j in range(SUBLANES):
                row = lambda x: x[:, j:j + 1, :]
                sa = seg_bcast(st * row(an), 2)
                vcol = seg_bcast(diag * row(v), 2)
                st = st * row(w) + sa * row(bv) + vcol * row(k)
                yb = seg_bcast(st * row(r), 1)
                ys.append(jnp.sum(yb * diag, axis=1, keepdims=True))
            st_ref[:, p] = st
            y_ref[:, pl.ds(t0, SUBLANES), sl] = jnp.concatenate(ys, axis=1)
        return carry

    lax.fori_loop(0, ts // SUBLANES, steps, 0)

    @pl.when(pl.program_id(1) == pl.num_programs(1) - 1)
    def _():
        sout_ref[...] = st_ref[...]


def _rwkv_scan(seqs, s0, bd128):
    b, s, _ = seqs[0].shape
    nb = _tile(b, 8)
    ts = _tile(s, 64)
    pairs = B_HEADS // 2
    seq = pl.BlockSpec((nb, ts, B_WIDTH), lambda i, j: (i, j, 0))
    sts = pl.BlockSpec((nb, pairs, B_HD, LANES), lambda i, j: (i, 0, 0, 0))
    return pl.pallas_call(
        _rwkv_scan_kernel,
        grid=(b // nb, s // ts),
        in_specs=[seq] * 6 + [sts, _full((LANES, LANES))],
        out_specs=[seq, sts],
        out_shape=[jax.ShapeDtypeStruct((b, s, B_WIDTH), F32), jax.ShapeDtypeStruct((b, pairs, B_HD, LANES), F32)],
        scratch_shapes=[pltpu.VMEM((nb, pairs, B_HD, LANES), F32)],
        compiler_params=_cparams("parallel", "arbitrary"),
        name="rwkv_scan",
    )(*seqs, s0, bd128)


def _rwkv_post_kernel(y_ref, bonus_ref, g_ref, lg_ref, lb_ref, bd_ref, o_ref):
    bd = bd_ref[...]
    y = y_ref[...]
    d = y - _split_dot(y, bd) * (1.0 / B_HD)
    var = _split_dot(d * d, bd) * (1.0 / B_HD)
    yn = d * lax.rsqrt(var + B_LN_EPS) * lg_ref[...] + lb_ref[...]
    o_ref[...] = (yn + bonus_ref[...]) * g_ref[...]


def _rwkv_post(y2, bonus2, g2, w):
    t = y2.shape[0]
    tm = _tile(t, 512)
    row = pl.BlockSpec((tm, B_WIDTH), lambda i: (i, 0))
    vec = _full((1, B_WIDTH))
    return pl.pallas_call(
        _rwkv_post_kernel,
        grid=(t // tm,),
        in_specs=[row, row, row, vec, vec, _full((B_WIDTH, B_WIDTH))],
        out_specs=row,
        out_shape=jax.ShapeDtypeStruct((t, B_WIDTH), F32),
        compiler_params=_cparams("parallel"),
        name="rwkv_post",
    )(y2, bonus2, g2, w["b_ln_g"], w["b_ln_b"], w["bd512"])


def _out_proj_kernel(x_ref, o1_ref, o2_ref, w1_ref, w2_ref, gf_ref, wq_ref, xo_ref, xn_ref, q_ref):
    x = x_ref[...] + _dot(o1_ref[...].astype(BF16), w1_ref[...]) + _dot(o2_ref[...].astype(BF16), w2_ref[...])
    xo_ref[...] = x
    xn = (_rms(x) * gf_ref[...]).astype(BF16)
    xn_ref[...] = xn
    q_ref[...] = _dot(xn, wq_ref[...])


def _out_proj(x2, o1, o2, w1, w2, gf, wq):
    t = x2.shape[0]
    tm = _tile(t, 512)
    row = lambda n: pl.BlockSpec((tm, n), lambda i: (i, 0))
    half = o1.shape[1]
    return pl.pallas_call(
        _out_proj_kernel,
        grid=(t // tm,),
        in_specs=[row(D_MODEL), row(half), row(half), _full((half, D_MODEL)), _full((half, D_MODEL)),
                  _full((1, D_MODEL)), _full((D_MODEL, PEER_HEADS * PEER_DK))],
        out_specs=[row(D_MODEL), row(D_MODEL), row(PEER_HEADS * PEER_DK)],
        out_shape=[jax.ShapeDtypeStruct((t, D_MODEL), F32), jax.ShapeDtypeStruct((t, D_MODEL), BF16),
                   jax.ShapeDtypeStruct((t, PEER_HEADS * PEER_DK), F32)],
        compiler_params=_cparams("parallel"),
        name="out_proj",
    )(x2, o1, o2, w1, w2, gf, wq)


def _top16_rows(sc, val_ref, idx_ref):
    n = sc.shape[0]
    rowi = lax.broadcasted_iota(I32, sc.shape, 0).astype(F32)

    def body(r, sc):
        m = jnp.max(sc, axis=0, keepdims=True)
        ix = jnp.min(jnp.where(sc == m, rowi, float(n)), axis=0, keepdims=True)
        val_ref[pl.ds(r, 1), :] = m
        idx_ref[pl.ds(r, 1), :] = ix
        return jnp.where(rowi == ix, -jnp.inf, sc)

    lax.fori_loop(0, PEER_TOPK, body, sc)


def _peer_route_kernel(q_ref, keys_ref, e1_ref, e2_ref, gate_ref, v1, i1, v2, i2, tv, te):
    qh = q_ref[...].astype(BF16)
    _top16_rows(_dot_nt(keys_ref[0, 0], qh), v1, i1)
    _top16_rows(_dot_nt(keys_ref[0, 1], qh), v2, i2)
    tb = qh.shape[0]
    s1, s2, x1, x2 = v1[...], v2[...], i1[...], i2[...]
    sub = lax.broadcasted_iota(I32, (8, tb), 0).astype(F32)
    cand, flat, eid = [], [], []
    for a in range(8):
        cand.append(s1[a:a + 1] + s2[0:8])
        flat.append(a * PEER_TOPK + sub)
        eid.append(x1[a:a + 1] * PEER_N_KEYS + x2[0:8])
    cand.append(s1[0:1] + s2[8:16])
    flat.append(8 + sub)
    eid.append(x1[0:1] * PEER_N_KEYS + x2[8:16])
    cand.append(s1[8:16] + s2[0:1])
    flat.append((8 + sub) * PEER_TOPK)
    eid.append(x1[8:16] * PEER_N_KEYS + x2[0:1])
    cand = jnp.concatenate(cand, axis=0)
    flat = jnp.concatenate(flat, axis=0)
    eid = jnp.concatenate(eid, axis=0)

    def body(r, cand):
        m = jnp.max(cand, axis=0, keepdims=True)
        f = jnp.min(jnp.where(cand == m, flat, 1e9), axis=0, keepdims=True)
        hit = flat == f
        tv[pl.ds(r, 1), :] = m
        te[pl.ds(r, 1), :] = jnp.max(jnp.where(hit, eid, -1.0), axis=0, keepdims=True)
        return jnp.where(hit, -jnp.inf, cand)

    lax.fori_loop(0, PEER_TOPK, body, cand)
    top = tv[...]
    e = jnp.exp(top - top[0:1])
    gate_ref[...] = e * (1.0 / jnp.sum(e, axis=0, keepdims=True))
    ex = te[...].astype(I32)
    e1_ref[...] = ex >> 7
    e2_ref[...] = ex & (PEER_N_KEYS - 1)


def _peer_route(q2, keys_pad):
    t = q2.shape[0]
    tb = _tile(t, 128)
    out = pl.BlockSpec((PEER_TOPK, tb), lambda i, h: (h, i))
    scr = lambda dt: pltpu.VMEM((PEER_TOPK, tb), dt)
    return pl.pallas_call(
        _peer_route_kernel,
        grid=(t // tb, PEER_HEADS),
        in_specs=[pl.BlockSpec((tb, PEER_DK), lambda i, h: (i, h)),
                  pl.BlockSpec((1, 2, PEER_N_KEYS, PEER_DK), lambda i, h: (h, 0, 0, 0))],
        out_specs=[out, out, out],
        out_shape=[jax.ShapeDtypeStruct((PEER_HEADS * PEER_TOPK, t), I32)] * 2
        + [jax.ShapeDtypeStruct((PEER_HEADS * PEER_TOPK, t), F32)],
        scratch_shapes=[scr(F32)] * 6,
        compiler_params=_cparams("parallel", "arbitrary"),
        name="peer_route",
    )(q2, keys_pad)


def _peer_gate_kernel(e1_ref, e2_ref, gate_ref, o_ref, scr):
    tg = e1_ref.shape[0]
    sub = lax.broadcasted_iota(I32, (PEER_N_KEYS, PEER_HEADS * PEER_TOPK), 0)

    def body(t, carry):
        e1 = e1_ref[pl.ds(t, 1), :]
        e2 = e2_ref[pl.ds(t, 1), :]
        g = gate_ref[pl.ds(t, 1), :]
        a_t = jnp.where(sub == e1, g, 0.0).astype(BF16)
        b_t = jnp.where(sub == e2, 1.0, 0.0).astype(BF16)
        scr[:, pl.ds(t, 1), :] = _dot_nt(a_t, b_t)[:, None, :]
        return carry

    lax.fori_loop(0, tg, body, 0)
    o_ref[...] = scr[...].astype(BF16)


def _peer_gates(e1, e2, gate):
    t = e1.shape[0]
    tg = _tile(t, 64)
    row = pl.BlockSpec((tg, PEER_HEADS * PEER_TOPK), lambda i: (i, 0))
    return pl.pallas_call(
        _peer_gate_kernel,
        grid=(t // tg,),
        in_specs=[row, row, row],
        out_specs=pl.BlockSpec((PEER_N_KEYS, tg, PEER_N_KEYS), lambda i: (0, i, 0)),
        out_shape=jax.ShapeDtypeStruct((PEER_N_KEYS, t, PEER_N_KEYS), BF16),
        scratch_shapes=[pltpu.VMEM((PEER_N_KEYS, tg, PEER_N_KEYS), F32)],
        compiler_params=_cparams("parallel"),
        name="peer_gates",
    )(e1, e2, gate)


def _gelu(x):
    return 0.5 * x * (1.0 + lax.erf(x * (2.0 ** -0.5)))


def _peer_dense_kernel(xn_ref, ut_ref, v_ref, g_ref, x_ref, o_ref):
    @pl.when(pl.program_id(1) == 0)
    def _():
        o_ref[...] = x_ref[...]

    h = _gelu(_dot(xn_ref[...], ut_ref[...]))
    nblk = g_ref.shape[0]
    gh = [(h[:, c * LANES:(c + 1) * LANES] * g_ref[c].astype(F32)).astype(BF16) for c in range(nblk)]
    o_ref[...] += _dot(jnp.concatenate(gh, axis=-1), v_ref[...])


def _peer_dense(xn, ut, vtab, gates, x2):
    t = xn.shape[0]
    tb = _tile(t, 512)
    eb = 512
    return pl.pallas_call(
        _peer_dense_kernel,
        grid=(t // tb, PEER_N_EXPERTS // eb),
        in_specs=[pl.BlockSpec((tb, D_MODEL), lambda i, j: (i, 0)),
                  pl.BlockSpec((D_MODEL, eb), lambda i, j: (0, j)),
                  pl.BlockSpec((eb, D_MODEL), lambda i, j: (j, 0)),
                  pl.BlockSpec((eb // PEER_N_KEYS, tb, PEER_N_KEYS), lambda i, j: (j, i, 0)),
                  pl.BlockSpec((tb, D_MODEL), lambda i, j: (i, 0))],
        out_specs=pl.BlockSpec((tb, D_MODEL), lambda i, j: (i, 0)),
        out_shape=jax.ShapeDtypeStruct((t, D_MODEL), F32),
        compiler_params=_cparams("parallel", "arbitrary"),
        name="peer_dense",
    )(xn, ut, vtab, gates, x2)


def _peer(x2, xn, q2, pw):
    e1, e2, gate = _peer_route(q2, pw["keys"])
    gates = _peer_gates(e1.T, e2.T, gate.T)
    return _peer_dense(xn, pw["ut"], pw["v"], gates, x2)


def _in1_kernel(x_ref, g_ref, wq_ref, wk_ref, wv_ref, wiq_ref, wik_ref, wiw_ref, wd_ref,
                q_ref, k_ref, v_ref, iq_ref, ik_ref, iw_ref, d_ref):
    xn = (_rms(x_ref[...]) * g_ref[...]).astype(BF16)
    for w_ref, o_ref in ((wq_ref, q_ref), (wk_ref, k_ref), (wv_ref, v_ref), (wiq_ref, iq_ref),
                         (wik_ref, ik_ref), (wiw_ref, iw_ref), (wd_ref, d_ref)):
        o_ref[...] = _dot(xn, w_ref[...])


def _in_proj1(x2, w):
    t = x2.shape[0]
    tm = _tile(t, 512)
    widths = [C_HEADS * C_HD, LANES, LANES, C_IDX_HEADS * C_IDX_D, LANES, LANES, 2 * D_WIDTH]
    row = lambda n: pl.BlockSpec((tm, n), lambda i: (i, 0))
    return pl.pallas_call(
        _in1_kernel,
        grid=(t // tm,),
        in_specs=[row(D_MODEL), _full((1, D_MODEL))] + [_full((D_MODEL, n)) for n in widths],
        out_specs=[row(n) for n in widths],
        out_shape=[jax.ShapeDtypeStruct((t, n), F32) for n in widths],
        compiler_params=_cparams("parallel"),
        name="in_proj1",
    )(x2, w["l1_norm"], w["w_cq1"], w["w_ck1"], w["w_cv1"], w["w_ciq"], w["w_cik"], w["w_ciw"], w["w_d"])


def _dsa_prep_kernel(q_ref, k_ref, iq_ref, ik_ref, qg_ref, kg_ref, ig_ref, cos_ref, sin_ref, bd_ref, perm_ref,
                     qo_ref, ko_ref, iqo_ref, iko_ref):
    cos, sin, bd, perm = cos_ref[...], sin_ref[...], bd_ref[...], perm_ref[...]

    def norm(x, gain, n):
        ss = _split_dot(x * x, bd[:n, :n]) * (1.0 / C_HD)
        return x * lax.rsqrt(ss + NORM_EPS) * gain

    def rope(x, n):
        return x * cos[:, :n] + _split_dot(x, perm[:n, :n]) * sin[:, :n]

    nq = C_HEADS * C_HD
    qo_ref[0] = (rope(norm(q_ref[0], qg_ref[...], nq), nq) * (C_HD ** -0.5)).astype(BF16)
    ko_ref[0] = rope(norm(k_ref[0], kg_ref[...], LANES), LANES)
    iqo_ref[0] = (rope(iq_ref[0], nq) * (C_IDX_D ** -0.5)).astype(BF16)
    iko_ref[0] = rope(norm(ik_ref[0], ig_ref[...], LANES), LANES)


def _dsa_prep(q, k, iq, ik, w, cos, sin):
    b, s, _ = q.shape
    ts = _tile(s, 256)
    nq = C_HEADS * C_HD
    blk = lambda n: pl.BlockSpec((1, ts, n), lambda i, j: (i, j, 0))
    tab = pl.BlockSpec((ts, nq), lambda i, j: (j, 0))
    return pl.pallas_call(
        _dsa_prep_kernel,
        grid=(b, s // ts),
        in_specs=[blk(nq), blk(LANES), blk(nq), blk(LANES), _full((1, nq)), _full((1, LANES)), _full((1, LANES)),
                  tab, tab, _full((nq, nq)), _full((nq, nq))],
        out_specs=[blk(nq), blk(LANES), blk(nq), blk(LANES)],
        out_shape=[jax.ShapeDtypeStruct((b, s, nq), BF16), jax.ShapeDtypeStruct((b, s, LANES), F32),
                   jax.ShapeDtypeStruct((b, s, nq), BF16), jax.ShapeDtypeStruct((b, s, LANES), F32)],
        compiler_params=_cparams("parallel", "parallel"),
        name="dsa_prep",
    )(q, k, iq, ik, w["c_q_gain"], w["c_k_gain"], w["c_kidx_gain"], cos, sin, w["bd512"], w["perm_c"])


def _order_key(x):
    bits = pltpu.bitcast(jnp.where(x == 0.0, 0.0, x), I32)
    return jnp.where(bits < 0, bits ^ jnp.int32(0x7FFFFFFF), bits)


def _topk_mask(score, n_sel, su):
    tq, lp = score.shape
    key = _order_key(score)

    def body(i, tau):
        trial = tau + lax.shift_left(jnp.int32(1), 31 - i)
        cnt = jnp.sum(jnp.where(key >= trial, 1.0, 0.0), axis=-1, keepdims=True)
        return jnp.where(cnt >= n_sel, trial, tau)

    tau = lax.fori_loop(0, 32, body, jnp.full((tq, 1), -(2 ** 31), I32))
    gt = jnp.where(key > tau, 1.0, 0.0)
    eq = jnp.where(key == tau, 1.0, 0.0)
    need = n_sel - jnp.sum(gt, axis=-1, keepdims=True)
    parts = []
    before = jnp.zeros((tq, 1), F32)
    for c in range(lp // LANES):
        eqc = eq[:, c * LANES:(c + 1) * LANES]
        rank = before + _dot(eqc.astype(BF16), su)
        parts.append(gt[:, c * LANES:(c + 1) * LANES] + jnp.where(rank < need, eqc, 0.0))
        before = before + jnp.sum(eqc, axis=-1, keepdims=True)
    return jnp.concatenate(parts, axis=-1)


def _dsa_attn_kernel(q_ref, iq_ref, iw_ref, ika_ref, ikb_ref, ka_ref, kb_ref, va_ref, vb_ref, su_ref, o_ref,
                     *, past, true_len, n_sel):
    tq = q_ref.shape[1]
    lp = ika_ref.shape[1]
    vis = _visible(tq, lp, past, true_len)
    iw = iw_ref[0] * (C_IDX_HEADS ** -0.5)
    score = jnp.zeros((tq, lp), F32)
    for m in range(C_IDX_HEADS // 2):
        iqp = iq_ref[0, :, m * LANES:(m + 1) * LANES]
        score = score + jnp.maximum(_dot_nt(iqp, ika_ref[0]), 0.0) * iw[:, 2 * m:2 * m + 1]
        score = score + jnp.maximum(_dot_nt(iqp, ikb_ref[0]), 0.0) * iw[:, 2 * m + 1:2 * m + 2]
    score = jnp.where(vis, score, NEG_INF)
    keep = jnp.where(vis, _topk_mask(score, n_sel, su_ref[...]), 0.0) > 0.5
    pairs_per_kv = C_HEADS // C_KV_HEADS // 2
    for m in range(C_HEADS // 2):
        c = m // pairs_per_kv
        qp = q_ref[0, :, m * LANES:(m + 1) * LANES]
        pa, inva = _masked_softmax(_dot_nt(qp, ka_ref[0, c]), keep)
        pb, invb = _masked_softmax(_dot_nt(qp, kb_ref[0, c]), keep)
        o_ref[0, :, m * LANES:(m + 1) * LANES] = (_dot(pa.astype(BF16), va_ref[0, c]) * inva
                                                  + _dot(pb.astype(BF16), vb_ref[0, c]) * invb)


def _dsa_attn(q, iq, iw, ika, ikb, ka, kb, va, vb, su, past, true_len):
    b, s, nq = q.shape
    lp = ika.shape[1]
    tq = _tile(s, 128)
    n_sel = min(C_TOPK, true_len // 4)
    qblk = lambda n: pl.BlockSpec((1, tq, n), lambda i, j: (i, j, 0))
    idx = pl.BlockSpec((1, lp, LANES), lambda i, j: (i, 0, 0))
    kv = pl.BlockSpec((1, C_KV_HEADS, lp, LANES), lambda i, j: (i, 0, 0, 0))
    return pl.pallas_call(
        functools.partial(_dsa_attn_kernel, past=past, true_len=true_len, n_sel=n_sel),
        grid=(b, s // tq),
        in_specs=[qblk(nq), qblk(nq), qblk(LANES), idx, idx, kv, kv, kv, kv, _full((LANES, LANES))],
        out_specs=qblk(nq),
        out_shape=jax.ShapeDtypeStruct((b, s, nq), F32),
        compiler_params=_cparams("parallel", "arbitrary"),
        name="dsa_attn",
    )(q, iq, iw, ika, ikb, ka, kb, va, vb, su)


def _gmlp_kernel(pd_ref, lg_ref, lb_ref, ws_ref, bs_ref, o_ref, dv_ref):
    span = pd_ref.shape[1]
    h = _gelu(pd_ref[0])
    u = h[:, :D_WIDTH]
    v = h[:, D_WIDTH:]
    mu = jnp.mean(v, axis=-1, keepdims=True)
    d = v - mu
    v = d * lax.rsqrt(jnp.mean(d * d, axis=-1, keepdims=True) + NORM_EPS) * lg_ref[...] + lb_ref[...]
    dv_ref[0] = v
    causal = lax.broadcasted_iota(I32, (span, span), 1) <= lax.broadcasted_iota(I32, (span, span), 0)
    gw = D_WIDTH // D_GROUPS
    mixed = [_dot(jnp.where(causal, ws_ref[g], 0.0).astype(BF16), v[:, g * gw:(g + 1) * gw].astype(BF16))
             for g in range(D_GROUPS)]
    o_ref[0] = u * (jnp.concatenate(mixed, axis=-1) + bs_ref[...])


def _gmlp(pd, w):
    b, s, _ = pd.shape
    span = D_SPAN if s % D_SPAN == 0 else s
    ws = w["d_ws"][:, :span, :span]
    bs = jnp.repeat(w["d_bs"][:, :span].T, D_WIDTH // D_GROUPS, axis=1)
    blk = lambda n: pl.BlockSpec((1, span, n), lambda i, j: (i, j, 0))
    return pl.pallas_call(
        _gmlp_kernel,
        grid=(b, s // span),
        in_specs=[blk(2 * D_WIDTH), _full((1, D_WIDTH)), _full((1, D_WIDTH)), _full((D_GROUPS, span, span)),
                  _full((span, D_WIDTH))],
        out_specs=[blk(D_WIDTH), blk(D_WIDTH)],
        out_shape=[jax.ShapeDtypeStruct((b, s, D_WIDTH), F32)] * 2,
        compiler_params=_cparams("parallel", "parallel"),
        name="gmlp",
    )(pd, w["d_ln_g"], w["d_ln_b"], ws, bs)


def _pad_cols(w, groups, width, slot, off=0):
    k = w.shape[0]
    w = w.reshape(k, groups, width)
    out = jnp.zeros((k, groups, slot), w.dtype).at[:, :, off:off + width].set(w)
    return out.reshape(k, groups * slot)


def _prep_weights(p):
    row = lambda v: v.reshape(1, -1).astype(F32)
    bf = lambda v: v.astype(BF16)
    w = {}
    w_in0 = p["l0_w_in"]
    w["l0_norm"] = row(p["l0_norm_mix"])
    w["w_cq"] = bf(w_in0[:, :A_D_CQ])
    w["w_ckv"] = bf(w_in0[:, A_D_CQ:A_D_CQ + A_D_C])
    a_cols = A_D_CQ + A_D_C + A_D_ROPE
    w["w_krp"] = bf(_pad_cols(w_in0[:, A_D_CQ + A_D_C:a_cols], 1, A_D_ROPE, LANES, A_D_NOPE))
    w["w_b"] = bf(w_in0[:, a_cols:])
    w["a_kv_norm"] = row(p["a_kv_norm"])
    w["a_q_norm"] = row(p["a_q_norm"])
    w["w_uq"] = bf(_pad_cols(p["a_w_uq"], A_HEADS, A_D_QK, LANES))
    ukv = p["a_w_ukv"].reshape(A_D_C, A_HEADS, A_D_NOPE + A_D_V)
    w["w_uk"] = bf(_pad_cols(ukv[:, :, :A_D_NOPE].reshape(A_D_C, -1), A_HEADS, A_D_NOPE, LANES))
    uv = ukv[:, :, A_D_NOPE:]
    uv_pad = jnp.zeros((A_D_C, A_HEADS // 2, 2, 2, A_D_V), F32)
    uv = uv.reshape(A_D_C, A_HEADS // 2, 2, A_D_V)
    uv_pad = uv_pad.at[:, :, 0, 0].set(uv[:, :, 0]).at[:, :, 1, 1].set(uv[:, :, 1])
    w["w_uv"] = bf(uv_pad.reshape(A_D_C, A_HEADS * LANES))
    w["a_q_gain"] = jnp.zeros((1, LANES), F32).at[0, :A_D_QK].set(p["a_q_gain"])
    w["a_k_gain"] = jnp.zeros((1, LANES), F32).at[0, :A_D_QK].set(p["a_k_gain"])
    w["perm_a"] = _rope_perm(LANES, LANES, A_D_NOPE, A_D_ROPE // 2)
    w["b_mu"] = row(p["b_mu"])
    w["b_w0"] = row(p["b_w0"])
    w["b_a0"] = row(p["b_a0"])
    zeros = jnp.zeros((B_W_LORA, B_WIDTH), F32)
    w["b_w2"] = bf(jnp.concatenate([p["b_w2"], zeros], axis=0))
    w["b_a2"] = bf(jnp.concatenate([zeros, p["b_a2"]], axis=0))
    w["b_g2"] = bf(p["b_g2"])
    for n in ("b_kk", "b_ka", "b_rk", "b_ln_g", "b_ln_b"):
        w[n] = row(p[n])
    w["bd512"] = _block_ones(B_WIDTH, B_HD)
    w["bd128"] = _block_ones(LANES, B_HD)
    w["w_out0a"] = bf(p["l0_w_out"][:A_HEADS * A_D_V])
    w["w_out0b"] = bf(p["l0_w_out"][A_HEADS * A_D_V:])
    w_in1 = p["l1_w_in"]
    nq = C_HEADS * C_HD
    nkv = C_KV_HEADS * C_HD
    o = 0
    w["w_cq1"] = bf(w_in1[:, o:o + nq]); o += nq
    w["w_ck1"] = bf(w_in1[:, o:o + nkv]); o += nkv
    w["w_cv1"] = bf(w_in1[:, o:o + nkv]); o += nkv
    w["w_ciq"] = bf(w_in1[:, o:o + nq]); o += nq
    w["w_cik"] = bf(_pad_cols(w_in1[:, o:o + C_IDX_D], 1, C_IDX_D, LANES)); o += C_IDX_D
    w["w_ciw"] = bf(_pad_cols(w_in1[:, o:o + C_IDX_HEADS], 1, C_IDX_HEADS, LANES)); o += C_IDX_HEADS
    w["w_d"] = bf(w_in1[:, o:])
    w["l1_norm"] = row(p["l1_norm_mix"])
    w["c_q_gain"] = row(jnp.tile(p["c_q_gain"], C_HEADS))
    w["c_k_gain"] = row(jnp.tile(p["c_k_gain"], C_KV_HEADS))
    w["c_kidx_gain"] = jnp.zeros((1, LANES), F32).at[0, :C_IDX_D].set(p["c_kidx_gain"])
    w["perm_c"] = _rope_perm(nq, C_HD, 0, C_HD // 2)
    w["su"] = (jnp.arange(LANES)[:, None] < jnp.arange(LANES)[None, :]).astype(BF16)
    w["d_ln_g"] = row(p["d_ln_g"])
    w["d_ln_b"] = row(p["d_ln_b"])
    w["d_ws"] = p["d_ws"]
    w["d_bs"] = p["d_bs"]
    w["w_out1a"] = bf(p["l1_w_out"][:nq])
    w["w_out1b"] = bf(p["l1_w_out"][nq:])
    for l in (0, 1):
        keys = p[f"l{l}_peer_keys"]
        half = PEER_DK // 2
        kp = jnp.zeros((PEER_HEADS, 2, PEER_N_KEYS, PEER_DK), F32)
        kp = kp.at[:, 0, :, :half].set(keys[:, 0]).at[:, 1, :, half:].set(keys[:, 1])
        w[f"peer{l}"] = {"keys": bf(kp), "ut": bf(p[f"l{l}_peer_u"]).T, "v": bf(p[f"l{l}_peer_v"]),
                         "norm": row(p[f"l{l}_norm_ffn"]), "wq": bf(p[f"l{l}_peer_wq"])}
    return w


def _pad_keys(x, lp):
    return jnp.pad(x, ((0, 0), (0, lp - x.shape[1])) + ((0, 0),) * (x.ndim - 2))


def _halves(x):
    z = jnp.zeros_like(x)
    return jnp.concatenate([x, z], axis=-1), jnp.concatenate([z, x], axis=-1)


def _trunk(x, st, w):
    b, s, _ = x.shape
    t = b * s
    past = st["a_ckv"].shape[1]
    true_len = past + s
    lp = -(-true_len // LANES) * LANES
    kpos = jnp.arange(lp, dtype=I32)

    cq, ckv, krp, pb = _in_proj0(x.reshape(t, D_MODEL), w)
    cos_a, sin_a = _rope_tables(kpos, A_D_ROPE // 2, LANES, A_D_NOPE, 1)
    q = _mla_q(cq.reshape(b, s, A_D_CQ), w, cos_a[past:true_len], sin_a[past:true_len])
    ckv3 = ckv.reshape(b, s, A_D_C)
    krp3 = krp.reshape(b, s, LANES)
    krope = krp3[:, :, A_D_NOPE:A_D_NOPE + A_D_ROPE]
    cache_krp = jnp.pad(st["a_krope"], ((0, 0), (0, 0), (A_D_NOPE, LANES - A_D_NOPE - A_D_ROPE)))
    ckv_all = _pad_keys(jnp.concatenate([st["a_ckv"], ckv3], axis=1), lp)
    krp_all = _pad_keys(jnp.concatenate([cache_krp, krp3], axis=1), lp)
    k_a, v_a = _mla_kv(ckv_all, krp_all, w, cos_a, sin_a)
    o_a = _mla_attn(q, k_a, v_a, past, true_len)

    pb3 = pb.reshape(b, s, B_PROJ)
    prev = jnp.concatenate([st["b_shift"][:, None, :], pb3[:, :-1]], axis=1)
    r, dec, k2, v, an, bv, g, bonus = _rwkv_prep(pb, prev.reshape(t, B_PROJ), w)
    pairs = B_HEADS // 2
    s0 = st["b_wkv"].reshape(b, pairs, 2, B_HD, B_HD).transpose(0, 1, 3, 2, 4).reshape(b, pairs, B_HD, LANES)
    y, s_out = _rwkv_scan([a.reshape(b, s, B_WIDTH) for a in (r, dec, k2, v, an, bv)], s0, w["bd128"])
    wkv = s_out.reshape(b, pairs, B_HD, 2, B_HD).transpose(0, 1, 3, 2, 4).reshape(b, B_HEADS, B_HD, B_HD)
    o_b = _rwkv_post(y.reshape(t, B_WIDTH), bonus, g, w)
    pw = w["peer0"]
    x1, xn1, pq1 = _out_proj(x.reshape(t, D_MODEL), o_a.reshape(t, -1), o_b, w["w_out0a"], w["w_out0b"],
                             pw["norm"], pw["wq"])
    x2 = _peer(x1, xn1, pq1, pw)

    nq = C_HEADS * C_HD
    cq1, ck1, cv1, ciq, cik, ciw, pd = _in_proj1(x2, w)
    pos_q = kpos[past:true_len]
    cos_c, sin_c = _rope_tables(pos_q, C_HD // 2, C_HD, 0, C_HEADS)
    three = lambda a: a.reshape(b, s, a.shape[-1])
    q_c, k_c, iq_c, ik_c = _dsa_prep(three(cq1), three(ck1), three(ciq), three(cik), w, cos_c, sin_c)
    c_k = k_c.reshape(b, s, C_KV_HEADS, C_HD)
    c_v = cv1.reshape(b, s, C_KV_HEADS, C_HD)
    c_kidx = ik_c[:, :, :C_IDX_D]
    k_all = _pad_keys(jnp.concatenate([st["c_k"], c_k], axis=1), lp).astype(BF16).transpose(0, 2, 1, 3)
    v_all = _pad_keys(jnp.concatenate([st["c_v"], c_v], axis=1), lp).astype(BF16).transpose(0, 2, 1, 3)
    ik_all = _pad_keys(jnp.concatenate([st["c_kidx"], c_kidx], axis=1), lp).astype(BF16)
    ka, kb = _halves(k_all)
    va, vb = _halves(v_all)
    ika, ikb = _halves(ik_all)
    o_c = _dsa_attn(q_c, iq_c, three(ciw), ika, ikb, ka, kb, va, vb, w["su"], past, true_len)
    o_d, d_v = _gmlp(three(pd), w)
    pw = w["peer1"]
    x3, xn3, pq3 = _out_proj(x2, o_c.reshape(t, nq), o_d.reshape(t, D_WIDTH), w["w_out1a"], w["w_out1b"],
                             pw["norm"], pw["wq"])
    y_out = _peer(x3, xn3, pq3, pw).reshape(b, s, D_MODEL)
    return y_out, [ckv3, krope, wkv, pb3[:, -1], c_k, c_v, c_kidx, d_v]


def kernel(x_prompt, x_sample, cache_a_ckv, cache_a_krope, state_b_wkv, state_b_shift, cache_c_k, cache_c_v, cache_c_kidx, l0_norm_mix, l0_w_in, a_q_norm, a_w_uq, a_kv_norm, a_w_ukv, a_q_gain, a_k_gain, b_mu, b_w0, b_w2, b_a0, b_a2, b_g2, b_kk, b_ka, b_rk, b_ln_g, b_ln_b, l0_w_out, l0_norm_ffn, l0_peer_wq, l0_peer_keys, l0_peer_u, l0_peer_v, l1_norm_mix, l1_w_in, c_q_gain, c_k_gain, c_kidx_gain, d_ln_g, d_ln_b, d_ws, d_bs, l1_w_out, l1_norm_ffn, l1_peer_wq, l1_peer_keys, l1_peer_u, l1_peer_v):
    params = dict(
        l0_norm_mix=l0_norm_mix, l0_w_in=l0_w_in, a_q_norm=a_q_norm, a_w_uq=a_w_uq, a_kv_norm=a_kv_norm,
        a_w_ukv=a_w_ukv, a_q_gain=a_q_gain, a_k_gain=a_k_gain, b_mu=b_mu, b_w0=b_w0, b_w2=b_w2, b_a0=b_a0,
        b_a2=b_a2, b_g2=b_g2, b_kk=b_kk, b_ka=b_ka, b_rk=b_rk.reshape(-1), b_ln_g=b_ln_g, b_ln_b=b_ln_b,
        l0_w_out=l0_w_out, l0_norm_ffn=l0_norm_ffn, l0_peer_wq=l0_peer_wq, l0_peer_keys=l0_peer_keys,
        l0_peer_u=l0_peer_u, l0_peer_v=l0_peer_v, l1_norm_mix=l1_norm_mix, l1_w_in=l1_w_in, c_q_gain=c_q_gain,
        c_k_gain=c_k_gain, c_kidx_gain=c_kidx_gain, d_ln_g=d_ln_g, d_ln_b=d_ln_b, d_ws=d_ws, d_bs=d_bs,
        l1_w_out=l1_w_out, l1_norm_ffn=l1_norm_ffn, l1_peer_wq=l1_peer_wq, l1_peer_keys=l1_peer_keys,
        l1_peer_u=l1_peer_u, l1_peer_v=l1_peer_v)
    w = _prep_weights(params)
    bp = x_prompt.shape[0]
    dt = x_prompt.dtype
    prompt_state = dict(
        a_ckv=jnp.zeros((bp, 0, A_D_C), dt), a_krope=jnp.zeros((bp, 0, A_D_ROPE), dt),
        b_wkv=jnp.zeros((bp, B_HEADS, B_HD, B_HD), dt), b_shift=jnp.zeros((bp, B_PROJ), dt),
        c_k=jnp.zeros((bp, 0, C_KV_HEADS, C_HD), dt), c_v=jnp.zeros((bp, 0, C_KV_HEADS, C_HD), dt),
        c_kidx=jnp.zeros((bp, 0, C_IDX_D), dt))
    sample_state = dict(a_ckv=cache_a_ckv, a_krope=cache_a_krope, b_wkv=state_b_wkv, b_shift=state_b_shift,
                        c_k=cache_c_k, c_v=cache_c_v, c_kidx=cache_c_kidx)
    y_p, new_p = _trunk(x_prompt, prompt_state, w)
    y_s, new_s = _trunk(x_sample, sample_state, w)
    return (y_p, y_s, *new_p[:7], *new_s)
```

```python
import functools
import math

import jax
import jax.numpy as jnp
from jax import lax
from jax.experimental import pallas as pl
from jax.experimental.pallas import tpu as pltpu

F32 = jnp.float32
BF16 = jnp.bfloat16
I32 = jnp.int32

D_MODEL = 1024
CHUNK = 64
CHUNK_SHIFT = 6
KEY_BLOCK = 512
ROPE_THETA = 10000.0
NORM_EPS = 1e-6
NEG_INF = -1e30
LANES = 128
SUBLANES = 8

A_HEADS = 8
A_D_NOPE = 64
A_D_ROPE = 32
A_D_QK = A_D_NOPE + A_D_ROPE
A_D_V = 64
A_D_CQ = 256
A_D_C = 128
B_HEADS = 8
B_HD = 64
B_WIDTH = B_HEADS * B_HD
B_W_LORA = 64
B_A_LORA = 64
B_G_LORA = 128
B_PROJ = 3 * B_WIDTH + B_W_LORA + B_A_LORA + B_G_LORA
B_LN_EPS = 64e-5
C_HEADS = 8
C_KV_HEADS = 2
C_HD = 64
C_IDX_HEADS = 8
C_IDX_D = 64
C_TOPK = 256
D_GROUPS = 4
D_WIDTH = 512
D_SPAN = 128
PEER_HEADS = 8
PEER_N_KEYS = 128
PEER_N_EXPERTS = PEER_N_KEYS * PEER_N_KEYS
PEER_DK = 128
PEER_TOPK = 16

VMEM_LIMIT = 48 * 1024 * 1024
NT = (((1,), (1,)), ((), ()))


def _tile(n, target):
    t = min(n, target)
    while n % t:
        t -= 8
    return t


def _cparams(*sem):
    return pltpu.CompilerParams(dimension_semantics=sem, vmem_limit_bytes=VMEM_LIMIT)


def _dot(a, b):
    return jnp.dot(a, b, preferred_element_type=F32)


def _dot_nt(a, b):
    return lax.dot_general(a, b, NT, preferred_element_type=F32)


def _split_dot(x, w, passes=2):
    hi = x.astype(BF16)
    acc = _dot(hi, w)
    rem = x - hi.astype(F32)
    for _ in range(passes - 1):
        part = rem.astype(BF16)
        acc = acc + _dot(part, w)
        rem = rem - part.astype(F32)
    return acc


def _rms(x, eps=NORM_EPS):
    return x * lax.rsqrt(jnp.mean(x * x, axis=-1, keepdims=True) + eps)


def _full(shape):
    nd = len(shape)
    return pl.BlockSpec(shape, lambda *_: (0,) * nd)


def _block_ones(n, seg):
    i = jnp.arange(n)
    return (i[:, None] // seg == i[None, :] // seg).astype(BF16)


def _rope_perm(n, seg, off, half):
    r = jnp.arange(n)[:, None]
    c = jnp.arange(n)[None, :]
    same = r // seg == c // seg
    rr = r % seg - off
    cc = c % seg - off
    plus = same & (rr >= 0) & (rr < half) & (cc == rr + half)
    minus = same & (cc >= 0) & (cc < half) & (rr == cc + half)
    return (plus.astype(F32) - minus.astype(F32)).astype(BF16)


def _rope_tables(pos, half, seg, off, reps):
    inv_freq = ROPE_THETA ** (-jnp.arange(half, dtype=F32) / half)
    ang = pos.astype(F32)[:, None] * inv_freq[None, :]
    n = pos.shape[0]
    cos = jnp.ones((n, seg), F32).at[:, off:off + 2 * half].set(jnp.tile(jnp.cos(ang), (1, 2)))
    sin = jnp.zeros((n, seg), F32).at[:, off:off + 2 * half].set(jnp.tile(jnp.sin(ang), (1, 2)))
    return jnp.tile(cos, (1, reps)), jnp.tile(sin, (1, reps))


def _in0_kernel(x_ref, g_ref, wcq_ref, wckv_ref, wkr_ref, wb_ref, kvn_ref,
                cq_ref, ckv_ref, krp_ref, pb_ref):
    xn = (_rms(x_ref[...]) * g_ref[...]).astype(BF16)
    cq_ref[...] = _dot(xn, wcq_ref[...])
    ckv_ref[...] = _rms(_dot(xn, wckv_ref[...])) * kvn_ref[...]
    krp_ref[...] = _dot(xn, wkr_ref[...])
    pb_ref[...] = _dot(xn, wb_ref[...])


def _in_proj0(x2, w):
    t = x2.shape[0]
    tm = _tile(t, 512)
    row = lambda n: pl.BlockSpec((tm, n), lambda i: (i, 0))
    return pl.pallas_call(
        _in0_kernel,
        grid=(t // tm,),
        in_specs=[row(D_MODEL), _full((1, D_MODEL)), _full((D_MODEL, A_D_CQ)), _full((D_MODEL, A_D_C)),
                  _full((D_MODEL, LANES)), _full((D_MODEL, B_PROJ)), _full((1, A_D_C))],
        out_specs=[row(A_D_CQ), row(A_D_C), row(LANES), row(B_PROJ)],
        out_shape=[jax.ShapeDtypeStruct((t, A_D_CQ), F32), jax.ShapeDtypeStruct((t, A_D_C), F32),
                   jax.ShapeDtypeStruct((t, LANES), F32), jax.ShapeDtypeStruct((t, B_PROJ), F32)],
        compiler_params=_cparams("parallel"),
        name="in_proj0",
    )(x2, w["l0_norm"], w["w_cq"], w["w_ckv"], w["w_krp"], w["w_b"], w["a_kv_norm"])


def _head_norm_rope(xh, gain, cos, sin, perm):
    ss = jnp.sum(xh * xh, axis=-1, keepdims=True) * (1.0 / A_D_QK)
    xh = xh * lax.rsqrt(ss + NORM_EPS) * gain
    return xh * cos + _split_dot(xh, perm) * sin


def _mla_q_kernel(cq_ref, qn_ref, wq_ref, gain_ref, cos_ref, sin_ref, perm_ref, o_ref):
    cqn = (_rms(cq_ref[0]) * qn_ref[...]).astype(BF16)
    q = _dot(cqn, wq_ref[...])
    cos, sin, perm, gain = cos_ref[...], sin_ref[...], perm_ref[...], gain_ref[...]
    for h in range(A_HEADS):
        qh = _head_norm_rope(q[:, h * LANES:(h + 1) * LANES], gain, cos, sin, perm)
        o_ref[0, :, h * LANES:(h + 1) * LANES] = (qh * (A_D_QK ** -0.5)).astype(BF16)


def _mla_q(cq, w, cos, sin):
    b, s, _ = cq.shape
    ts = _tile(s, 256)
    return pl.pallas_call(
        _mla_q_kernel,
        grid=(b, s // ts),
        in_specs=[pl.BlockSpec((1, ts, A_D_CQ), lambda i, j: (i, j, 0)), _full((1, A_D_CQ)),
                  _full((A_D_CQ, A_HEADS * LANES)), _full((1, LANES)),
                  pl.BlockSpec((ts, LANES), lambda i, j: (j, 0)), pl.BlockSpec((ts, LANES), lambda i, j: (j, 0)),
                  _full((LANES, LANES))],
        out_specs=pl.BlockSpec((1, ts, A_HEADS * LANES), lambda i, j: (i, j, 0)),
        out_shape=jax.ShapeDtypeStruct((b, s, A_HEADS * LANES), BF16),
        compiler_params=_cparams("parallel", "parallel"),
        name="mla_q",
    )(cq, w["a_q_norm"], w["w_uq"], w["a_q_gain"], cos, sin, w["perm_a"])


def _mla_kv_kernel(ckv_ref, krp_ref, wk_ref, wv_ref, gain_ref, cos_ref, sin_ref, perm_ref, k_ref, v_ref):
    c = ckv_ref[0].astype(BF16)
    k = _dot(c, wk_ref[...])
    v_ref[0] = _dot(c, wv_ref[...]).astype(BF16)
    krp = krp_ref[0]
    cos, sin, perm, gain = cos_ref[...], sin_ref[...], perm_ref[...], gain_ref[...]
    for h in range(A_HEADS):
        kh = _head_norm_rope(k[:, h * LANES:(h + 1) * LANES] + krp, gain, cos, sin, perm)
        k_ref[0, :, h * LANES:(h + 1) * LANES] = kh.astype(BF16)


def _mla_kv(ckv_all, krp_all, w, cos, sin):
    b, lp, _ = ckv_all.shape
    tl = _tile(lp, 256)
    blk = lambda n: pl.BlockSpec((1, tl, n), lambda i, j: (i, j, 0))
    tab = pl.BlockSpec((tl, LANES), lambda i, j: (j, 0))
    return pl.pallas_call(
        _mla_kv_kernel,
        grid=(b, lp // tl),
        in_specs=[blk(A_D_C), blk(LANES), _full((A_D_C, A_HEADS * LANES)), _full((A_D_C, A_HEADS * LANES)),
                  _full((1, LANES)), tab, tab, _full((LANES, LANES))],
        out_specs=[blk(A_HEADS * LANES), blk(A_HEADS * LANES)],
        out_shape=[jax.ShapeDtypeStruct((b, lp, A_HEADS * LANES), BF16)] * 2,
        compiler_params=_cparams("parallel", "parallel"),
        name="mla_kv",
    )(ckv_all, krp_all, w["w_uk"], w["w_uv"], w["a_k_gain"], cos, sin, w["perm_a"])


def _visible(tq, lp, q0, true_len):
    qpos = q0 + lax.broadcasted_iota(I32, (tq, 1), 0)
    kpos = lax.broadcasted_iota(I32, (1, lp), 1)
    kchunk = jnp.where(kpos < true_len, kpos >> CHUNK_SHIFT, jnp.int32(1 << 30))
    return kchunk <= (qpos >> CHUNK_SHIFT)


def _masked_softmax(s, keep):
    s = jnp.where(keep, s, NEG_INF)
    p = jnp.exp(s - jnp.max(s, axis=-1, keepdims=True))
    return p, 1.0 / jnp.sum(p, axis=-1, keepdims=True)


def _for_visible_prefix(run, tq, lp, past):
    q0 = past + pl.program_id(1) * tq
    need = (((q0 + tq - 1) >> CHUNK_SHIFT) + 1) << CHUNK_SHIFT
    sizes = [min(c * KEY_BLOCK, lp) for c in range(1, -(-lp // KEY_BLOCK) + 1)]
    case = (need + (KEY_BLOCK - 1)) // KEY_BLOCK
    for c, n in enumerate(sizes, start=1):
        pl.when(case == c)(functools.partial(run, n, q0))


def _mla_attn_kernel(q_ref, k_ref, v_ref, o_ref, *, past, true_len):
    tq = q_ref.shape[1]

    def run(n, q0):
        vis = _visible(tq, n, q0, true_len)
        for m in range(A_HEADS // 2):
            acc = None
            for h in (2 * m, 2 * m + 1):
                sl = slice(h * LANES, (h + 1) * LANES)
                p, inv = _masked_softmax(_dot_nt(q_ref[0, :, sl], k_ref[0, :n, sl]), vis)
                o = _dot(p.astype(BF16), v_ref[0, :n, sl]) * inv
                acc = o if acc is None else acc + o
            o_ref[0, :, m * LANES:(m + 1) * LANES] = acc

    _for_visible_prefix(run, tq, k_ref.shape[1], past)


def _mla_attn(q, k, v, past, true_len):
    b, s, _ = q.shape
    lp = k.shape[1]
    tq = _tile(s, 128)
    kv = pl.BlockSpec((1, lp, A_HEADS * LANES), lambda i, j: (i, 0, 0))
    return pl.pallas_call(
        functools.partial(_mla_attn_kernel, past=past, true_len=true_len),
        grid=(b, s // tq),
        in_specs=[pl.BlockSpec((1, tq, A_HEADS * LANES), lambda i, j: (i, j, 0)), kv, kv],
        out_specs=pl.BlockSpec((1, tq, A_HEADS * A_D_V), lambda i, j: (i, j, 0)),
        out_shape=jax.ShapeDtypeStruct((b, s, A_HEADS * A_D_V), F32),
        compiler_params=_cparams("parallel", "arbitrary"),
        name="mla_attn",
    )(q, k, v)


def _rwkv_prep_kernel(pb_ref, prev_ref, mu_ref, w0_ref, w2_ref, a0_ref, a2_ref, g2_ref, kk_ref, ka_ref, rk_ref,
                      bd_ref, r_ref, w_ref, k_ref, v_ref, an_ref, bv_ref, g_ref, bonus_ref):
    pb = pb_ref[...]
    xs = pb + (prev_ref[...] - pb) * mu_ref[...]
    r = xs[:, 0:B_WIDTH]
    k = xs[:, B_WIDTH:2 * B_WIDTH]
    v = xs[:, 2 * B_WIDTH:3 * B_WIDTH]
    wa = xs[:, 3 * B_WIDTH:3 * B_WIDTH + B_W_LORA + B_A_LORA]
    gl = xs[:, 3 * B_WIDTH + B_W_LORA + B_A_LORA:]
    bd = bd_ref[...]
    z = -(w0_ref[...] + _dot(jnp.tanh(wa).astype(BF16), w2_ref[...]))
    softplus = jnp.maximum(z, 0.0) + jnp.log(1.0 + jnp.exp(-jnp.abs(z)))
    w_ref[...] = jnp.exp(-jnp.exp(-softplus - 0.5))
    a = jax.nn.sigmoid(a0_ref[...] + _dot(wa.astype(BF16), a2_ref[...]))
    g_ref[...] = _dot(jax.nn.sigmoid(gl).astype(BF16), g2_ref[...])
    kk = k * kk_ref[...]
    kk = kk * lax.rsqrt(_split_dot(kk * kk, bd) + 1e-12)
    k2 = k * (1.0 + (a - 1.0) * ka_ref[...])
    r_ref[...] = r
    k_ref[...] = k2
    v_ref[...] = v
    an_ref[...] = -kk
    bv_ref[...] = kk * a
    bonus_ref[...] = _split_dot(r * k2 * rk_ref[...], bd) * v


def _rwkv_prep(pb2, prev2, w):
    t = pb2.shape[0]
    tm = _tile(t, 256)
    row = lambda n: pl.BlockSpec((tm, n), lambda i: (i, 0))
    vec = _full((1, B_WIDTH))
    return pl.pallas_call(
        _rwkv_prep_kernel,
        grid=(t // tm,),
        in_specs=[row(B_PROJ), row(B_PROJ), _full((1, B_PROJ)), vec, _full((LANES, B_WIDTH)), vec,
                  _full((LANES, B_WIDTH)), _full((B_G_LORA, B_WIDTH)), vec, vec, vec, _full((B_WIDTH, B_WIDTH))],
        out_specs=[row(B_WIDTH)] * 8,
        out_shape=[jax.ShapeDtypeStruct((t, B_WIDTH), F32)] * 8,
        compiler_params=_cparams("parallel"),
        name="rwkv_prep",
    )(pb2, prev2, w["b_mu"], w["b_w0"], w["b_w2"], w["b_a0"], w["b_a2"], w["b_g2"], w["b_kk"], w["b_ka"],
      w["b_rk"], w["bd512"])


def _rwkv_scan_kernel(r_ref, w_ref, k_ref, v_ref, an_ref, bv_ref, s0_ref, bd_ref, y_ref, sout_ref, st_ref):
    nb, ts, _ = r_ref.shape
    pairs = B_HEADS // 2

    @pl.when(pl.program_id(1) == 0)
    def _():
        st_ref[...] = s0_ref[...]

    bd = bd_ref[...]
    lane = lax.broadcasted_iota(I32, (B_HD, LANES), 1)
    sub = lax.broadcasted_iota(I32, (B_HD, LANES), 0)
    diag = jnp.where((lane & (B_HD - 1)) == sub, 1.0, 0.0).astype(F32)[None]

    def seg_bcast(x, passes):
        return _split_dot(x.reshape(nb * B_HD, LANES), bd, passes).reshape(nb, B_HD, LANES)

    def steps(c, carry):
        t0 = pl.multiple_of(c * SUBLANES, SUBLANES)
        for p in range(pairs):
            sl = slice(p * LANES, (p + 1) * LANES)
            r, w, k, v, an, bv = [ref[:, pl.ds(t0, SUBLANES), sl] for ref in
                                  (r_ref, w_ref, k_ref, v_ref, an_ref, bv_ref)]
            st = st_ref[:, p]
            ys = []
            for j in range(SUBLANES):
                row = lambda x: x[:, j:j + 1, :]
                sa = seg_bcast(st * row(an), 1)
                vcol = seg_bcast(diag * row(v), 1)
                st = st * row(w) + sa * row(bv) + vcol * row(k)
                yb = seg_bcast(st * row(r), 1)
                ys.append(jnp.sum(yb * diag, axis=1, keepdims=True))
            st_ref[:, p] = st
            y_ref[:, pl.ds(t0, SUBLANES), sl] = jnp.concatenate(ys, axis=1)
        return carry

    lax.fori_loop(0, ts // SUBLANES, steps, 0)

    @pl.when(pl.program_id(1) == pl.num_programs(1) - 1)
    def _():
        sout_ref[...] = st_ref[...]


def _rwkv_scan(seqs, s0, bd128):
    b, s, _ = seqs[0].shape
    nb = _tile(b, 8)
    ts = _tile(s, 64)
    pairs = B_HEADS // 2
    seq = pl.BlockSpec((nb, ts, B_WIDTH), lambda i, j: (i, j, 0))
    sts = pl.BlockSpec((nb, pairs, B_HD, LANES), lambda i, j: (i, 0, 0, 0))
    return pl.pallas_call(
        _rwkv_scan_kernel,
        grid=(b // nb, s // ts),
        in_specs=[seq] * 6 + [sts, _full((LANES, LANES))],
        out_specs=[seq, sts],
        out_shape=[jax.ShapeDtypeStruct((b, s, B_WIDTH), F32), jax.ShapeDtypeStruct((b, pairs, B_HD, LANES), F32)],
        scratch_shapes=[pltpu.VMEM((nb, pairs, B_HD, LANES), F32)],
        compiler_params=_cparams("parallel", "arbitrary"),
        name="rwkv_scan",
    )(*seqs, s0, bd128)


def _rwkv_post_kernel(y_ref, bonus_ref, g_ref, lg_ref, lb_ref, bd_ref, o_ref):
    bd = bd_ref[...]
    y = y_ref[...]
    d = y - _split_dot(y, bd) * (1.0 / B_HD)
    var = _split_dot(d * d, bd) * (1.0 / B_HD)
    yn = d * lax.rsqrt(var + B_LN_EPS) * lg_ref[...] + lb_ref[...]
    o_ref[...] = (yn + bonus_ref[...]) * g_ref[...]


def _rwkv_post(y2, bonus2, g2, w):
    t = y2.shape[0]
    tm = _tile(t, 512)
    row = pl.BlockSpec((tm, B_WIDTH), lambda i: (i, 0))
    vec = _full((1, B_WIDTH))
    return pl.pallas_call(
        _rwkv_post_kernel,
        grid=(t // tm,),
        in_specs=[row, row, row, vec, vec, _full((B_WIDTH, B_WIDTH))],
        out_specs=row,
        out_shape=jax.ShapeDtypeStruct((t, B_WIDTH), F32),
        compiler_params=_cparams("parallel"),
        name="rwkv_post",
    )(y2, bonus2, g2, w["b_ln_g"], w["b_ln_b"], w["bd512"])


def _out_proj_kernel(x_ref, o1_ref, o2_ref, w1_ref, w2_ref, gf_ref, wq_ref, xo_ref, xn_ref, q_ref):
    x = x_ref[...] + _dot(o1_ref[...].astype(BF16), w1_ref[...]) + _dot(o2_ref[...].astype(BF16), w2_ref[...])
    xo_ref[...] = x
    xn = (_rms(x) * gf_ref[...]).astype(BF16)
    xn_ref[...] = xn
    q_ref[...] = _dot(xn, wq_ref[...])


def _out_proj(x2, o1, o2, w1, w2, gf, wq):
    t = x2.shape[0]
    tm = _tile(t, 512)
    row = lambda n: pl.BlockSpec((tm, n), lambda i: (i, 0))
    half = o1.shape[1]
    return pl.pallas_call(
        _out_proj_kernel,
        grid=(t // tm,),
        in_specs=[row(D_MODEL), row(half), row(half), _full((half, D_MODEL)), _full((half, D_MODEL)),
                  _full((1, D_MODEL)), _full((D_MODEL, PEER_HEADS * PEER_DK))],
        out_specs=[row(D_MODEL), row(D_MODEL), row(PEER_HEADS * PEER_DK)],
        out_shape=[jax.ShapeDtypeStruct((t, D_MODEL), F32), jax.ShapeDtypeStruct((t, D_MODEL), BF16),
                   jax.ShapeDtypeStruct((t, PEER_HEADS * PEER_DK), F32)],
        compiler_params=_cparams("parallel"),
        name="out_proj",
    )(x2, o1, o2, w1, w2, gf, wq)


def _top16_rows(scores, val_ref, idx_ref):
    n = scores[0].shape[0]
    rowi = lax.broadcasted_iota(I32, scores[0].shape, 0).astype(F32)

    def body(r, scs):
        nxt = []
        for c, sc in enumerate(scs):
            m = jnp.max(sc, axis=0, keepdims=True)
            ix = jnp.min(jnp.where(sc == m, rowi, float(n)), axis=0, keepdims=True)
            val_ref[c, pl.ds(r, 1), :] = m
            idx_ref[c, pl.ds(r, 1), :] = ix
            nxt.append(jnp.where(rowi == ix, -jnp.inf, sc))
        return tuple(nxt)

    lax.fori_loop(0, PEER_TOPK, body, tuple(scores))


def _pair_candidates(s1, s2, x1, x2):
    sub = lax.broadcasted_iota(I32, (SUBLANES, s1.shape[1]), 0).astype(F32)
    cand, flat, eid = [], [], []
    for a in range(8):
        cand.append(s1[a:a + 1] + s2[0:8])
        flat.append(a * PEER_TOPK + sub)
        eid.append(x1[a:a + 1] * PEER_N_KEYS + x2[0:8])
    cand.append(s1[0:1] + s2[8:16])
    flat.append(8 + sub)
    eid.append(x1[0:1] * PEER_N_KEYS + x2[8:16])
    cand.append(s1[8:16] + s2[0:1])
    flat.append((8 + sub) * PEER_TOPK)
    eid.append(x1[8:16] * PEER_N_KEYS + x2[0:1])
    return jnp.concatenate(cand, axis=0), jnp.concatenate(flat, axis=0), jnp.concatenate(eid, axis=0)


def _peer_route_kernel(q_ref, keys_ref, e1_ref, e2_ref, gate_ref, topv, topi, selv, sele):
    for hp in range(PEER_HEADS // 2):
        heads = (2 * hp, 2 * hp + 1)
        for c, h in enumerate(heads):
            qh = q_ref[:, h * PEER_DK:(h + 1) * PEER_DK].astype(BF16)
            _top16_rows([_dot_nt(keys_ref[h, 0], qh), _dot_nt(keys_ref[h, 1], qh)], topv.at[c], topi.at[c])
        cands, eids = [], []
        for c in range(2):
            cand, flat, eid = _pair_candidates(topv[c, 0], topv[c, 1], topi[c, 0], topi[c, 1])
            cands.append(cand)
            eids.append(eid)

        def body(r, cs):
            nxt = []
            for c, cand in enumerate(cs):
                m = jnp.max(cand, axis=0, keepdims=True)
                f = jnp.min(jnp.where(cand == m, flat, 1e9), axis=0, keepdims=True)
                hit = flat == f
                selv[c, pl.ds(r, 1), :] = m
                sele[c, pl.ds(r, 1), :] = jnp.max(jnp.where(hit, eids[c], -1.0), axis=0, keepdims=True)
                nxt.append(jnp.where(hit, -jnp.inf, cand))
            return tuple(nxt)

        lax.fori_loop(0, PEER_TOPK, body, tuple(cands))
        for c, h in enumerate(heads):
            rows = slice(h * PEER_TOPK, (h + 1) * PEER_TOPK)
            top = selv[c]
            e = jnp.exp(top - top[0:1])
            gate_ref[rows, :] = e * (1.0 / jnp.sum(e, axis=0, keepdims=True))
            ex = sele[c].astype(I32)
            e1_ref[rows, :] = ex >> 7
            e2_ref[rows, :] = ex & (PEER_N_KEYS - 1)


def _peer_route(q2, keys_pad):
    t = q2.shape[0]
    tb = _tile(t, 128)
    nk = PEER_HEADS * PEER_TOPK
    out = pl.BlockSpec((nk, tb), lambda i: (0, i))
    return pl.pallas_call(
        _peer_route_kernel,
        grid=(t // tb,),
        in_specs=[pl.BlockSpec((tb, PEER_HEADS * PEER_DK), lambda i: (i, 0)),
                  _full((PEER_HEADS, 2, PEER_N_KEYS, PEER_DK))],
        out_specs=[out, out, out],
        out_shape=[jax.ShapeDtypeStruct((nk, t), I32)] * 2 + [jax.ShapeDtypeStruct((nk, t), F32)],
        scratch_shapes=[pltpu.VMEM((2, 2, PEER_TOPK, tb), F32)] * 2 + [pltpu.VMEM((2, PEER_TOPK, tb), F32)] * 2,
        compiler_params=_cparams("parallel"),
        name="peer_route",
    )(q2, keys_pad)


def _peer_gate_kernel(e1_ref, e2_ref, gate_ref, o_ref, scr):
    tg = e1_ref.shape[0]
    nk = PEER_HEADS * PEER_TOPK
    sub = lax.broadcasted_iota(I32, (PEER_N_KEYS, nk), 0)

    def body(c, carry):
        t0 = pl.multiple_of(c * SUBLANES, SUBLANES)
        e1 = e1_ref[pl.ds(t0, SUBLANES), :]
        e2 = e2_ref[pl.ds(t0, SUBLANES), :]
        g = gate_ref[pl.ds(t0, SUBLANES), :]
        mats = []
        for j in range(SUBLANES):
            a_t = jnp.where(sub == e1[j:j + 1], g[j:j + 1], 0.0).astype(BF16)
            b_t = jnp.where(sub == e2[j:j + 1], 1.0, 0.0).astype(BF16)
            mats.append(_dot_nt(a_t, b_t))
        scr[:, pl.ds(t0, SUBLANES), :] = jnp.swapaxes(jnp.stack(mats, axis=0), 0, 1)
        return carry

    lax.fori_loop(0, tg // SUBLANES, body, 0)
    o_ref[...] = scr[...].astype(BF16)


def _peer_gates(e1, e2, gate):
    t = e1.shape[0]
    tg = _tile(t, 64)
    row = pl.BlockSpec((tg, PEER_HEADS * PEER_TOPK), lambda i: (i, 0))
    return pl.pallas_call(
        _peer_gate_kernel,
        grid=(t // tg,),
        in_specs=[row, row, row],
        out_specs=pl.BlockSpec((PEER_N_KEYS, tg, PEER_N_KEYS), lambda i: (0, i, 0)),
        out_shape=jax.ShapeDtypeStruct((PEER_N_KEYS, t, PEER_N_KEYS), BF16),
        scratch_shapes=[pltpu.VMEM((PEER_N_KEYS, tg, PEER_N_KEYS), F32)],
        compiler_params=_cparams("parallel"),
        name="peer_gates",
    )(e1, e2, gate)


def _gelu(x):
    return 0.5 * x * (1.0 + lax.erf(x * (2.0 ** -0.5)))


def _peer_dense_kernel(xn_ref, ut_ref, v_ref, g_ref, x_ref, o_ref):
    @pl.when(pl.program_id(1) == 0)
    def _():
        o_ref[...] = x_ref[...]

    h = _gelu(_dot(xn_ref[...], ut_ref[...]))
    nblk = g_ref.shape[0]
    gh = [(h[:, c * LANES:(c + 1) * LANES] * g_ref[c].astype(F32)).astype(BF16) for c in range(nblk)]
    o_ref[...] += _dot(jnp.concatenate(gh, axis=-1), v_ref[...])


def _peer_dense(xn, ut, vtab, gates, x2):
    t = xn.shape[0]
    tb = _tile(t, 1024)
    eb = 512
    return pl.pallas_call(
        _peer_dense_kernel,
        grid=(t // tb, PEER_N_EXPERTS // eb),
        in_specs=[pl.BlockSpec((tb, D_MODEL), lambda i, j: (i, 0)),
                  pl.BlockSpec((D_MODEL, eb), lambda i, j: (0, j)),
                  pl.BlockSpec((eb, D_MODEL), lambda i, j: (j, 0)),
                  pl.BlockSpec((eb // PEER_N_KEYS, tb, PEER_N_KEYS), lambda i, j: (j, i, 0)),
                  pl.BlockSpec((tb, D_MODEL), lambda i, j: (i, 0))],
        out_specs=pl.BlockSpec((tb, D_MODEL), lambda i, j: (i, 0)),
        out_shape=jax.ShapeDtypeStruct((t, D_MODEL), F32),
        compiler_params=_cparams("parallel", "arbitrary"),
        name="peer_dense",
    )(xn, ut, vtab, gates, x2)


def _peer(x2, xn, q2, pw):
    e1, e2, gate = _peer_route(q2, pw["keys"])
    gates = _peer_gates(e1.T, e2.T, gate.T)
    return _peer_dense(xn, pw["ut"], pw["v"], gates, x2)


def _in1_kernel(x_ref, g_ref, wq_ref, wk_ref, wv_ref, wiq_ref, wik_ref, wiw_ref, wd_ref,
                q_ref, k_ref, v_ref, iq_ref, ik_ref, iw_ref, d_ref):
    xn = (_rms(x_ref[...]) * g_ref[...]).astype(BF16)
    for w_ref, o_ref in ((wq_ref, q_ref), (wk_ref, k_ref), (wv_ref, v_ref), (wiq_ref, iq_ref),
                         (wik_ref, ik_ref), (wiw_ref, iw_ref), (wd_ref, d_ref)):
        o_ref[...] = _dot(xn, w_ref[...])


def _in_proj1(x2, w):
    t = x2.shape[0]
    tm = _tile(t, 512)
    widths = [C_HEADS * C_HD, LANES, LANES, C_IDX_HEADS * C_IDX_D, LANES, LANES, 2 * D_WIDTH]
    row = lambda n: pl.BlockSpec((tm, n), lambda i: (i, 0))
    return pl.pallas_call(
        _in1_kernel,
        grid=(t // tm,),
        in_specs=[row(D_MODEL), _full((1, D_MODEL))] + [_full((D_MODEL, n)) for n in widths],
        out_specs=[row(n) for n in widths],
        out_shape=[jax.ShapeDtypeStruct((t, n), F32) for n in widths],
        compiler_params=_cparams("parallel"),
        name="in_proj1",
    )(x2, w["l1_norm"], w["w_cq1"], w["w_ck1"], w["w_cv1"], w["w_ciq"], w["w_cik"], w["w_ciw"], w["w_d"])


def _dsa_prep_kernel(q_ref, k_ref, iq_ref, ik_ref, qg_ref, kg_ref, ig_ref, cos_ref, sin_ref, bd_ref, perm_ref,
                     qo_ref, ko_ref, iqo_ref, iko_ref):
    cos, sin, bd, perm = cos_ref[...], sin_ref[...], bd_ref[...], perm_ref[...]

    def norm(x, gain, n):
        ss = _split_dot(x * x, bd[:n, :n]) * (1.0 / C_HD)
        return x * lax.rsqrt(ss + NORM_EPS) * gain

    def rope(x, n):
        return x * cos[:, :n] + _split_dot(x, perm[:n, :n]) * sin[:, :n]

    nq = C_HEADS * C_HD
    qo_ref[0] = (rope(norm(q_ref[0], qg_ref[...], nq), nq) * (C_HD ** -0.5)).astype(BF16)
    ko_ref[0] = rope(norm(k_ref[0], kg_ref[...], LANES), LANES)
    iqo_ref[0] = (rope(iq_ref[0], nq) * (C_IDX_D ** -0.5)).astype(BF16)
    iko_ref[0] = rope(norm(ik_ref[0], ig_ref[...], LANES), LANES)


def _dsa_prep(q, k, iq, ik, w, cos, sin):
    b, s, _ = q.shape
    ts = _tile(s, 256)
    nq = C_HEADS * C_HD
    blk = lambda n: pl.BlockSpec((1, ts, n), lambda i, j: (i, j, 0))
    tab = pl.BlockSpec((ts, nq), lambda i, j: (j, 0))
    return pl.pallas_call(
        _dsa_prep_kernel,
        grid=(b, s // ts),
        in_specs=[blk(nq), blk(LANES), blk(nq), blk(LANES), _full((1, nq)), _full((1, LANES)), _full((1, LANES)),
                  tab, tab, _full((nq, nq)), _full((nq, nq))],
        out_specs=[blk(nq), blk(LANES), blk(nq), blk(LANES)],
        out_shape=[jax.ShapeDtypeStruct((b, s, nq), BF16), jax.ShapeDtypeStruct((b, s, LANES), F32),
                   jax.ShapeDtypeStruct((b, s, nq), BF16), jax.ShapeDtypeStruct((b, s, LANES), F32)],
        compiler_params=_cparams("parallel", "parallel"),
        name="dsa_prep",
    )(q, k, iq, ik, w["c_q_gain"], w["c_k_gain"], w["c_kidx_gain"], cos, sin, w["bd512"], w["perm_c"])


def _order_key(x):
    bits = pltpu.bitcast(jnp.where(x == 0.0, 0.0, x), I32)
    return jnp.where(bits < 0, bits ^ jnp.int32(0x7FFFFFFF), bits)


def _topk_mask(score, n_sel, su):
    tq, lp = score.shape
    key = _order_key(score)

    def body(i, tau):
        trial = tau + lax.shift_left(jnp.int32(1), 31 - i)
        cnt = jnp.sum(jnp.where(key >= trial, 1.0, 0.0), axis=-1, keepdims=True)
        return jnp.where(cnt >= n_sel, trial, tau)

    tau = lax.fori_loop(0, 32, body, jnp.full((tq, 1), -(2 ** 31), I32))
    gt = jnp.where(key > tau, 1.0, 0.0)
    eq = jnp.where(key == tau, 1.0, 0.0)
    need = n_sel - jnp.sum(gt, axis=-1, keepdims=True)
    parts = []
    before = jnp.zeros((tq, 1), F32)
    for c in range(lp // LANES):
        eqc = eq[:, c * LANES:(c + 1) * LANES]
        rank = before + _dot(eqc.astype(BF16), su)
        parts.append(gt[:, c * LANES:(c + 1) * LANES] + jnp.where(rank < need, eqc, 0.0))
        before = before + jnp.sum(eqc, axis=-1, keepdims=True)
    return jnp.concatenate(parts, axis=-1)


def _dsa_attn_kernel(q_ref, iq_ref, iw_ref, ika_ref, ikb_ref, ka_ref, kb_ref, va_ref, vb_ref, su_ref, o_ref,
                     *, past, true_len, n_sel):
    tq = q_ref.shape[1]

    def run(n, q0):
        vis = _visible(tq, n, q0, true_len)
        iw = iw_ref[0] * (C_IDX_HEADS ** -0.5)
        score = jnp.zeros((tq, n), F32)
        for m in range(C_IDX_HEADS // 2):
            iqp = iq_ref[0, :, m * LANES:(m + 1) * LANES]
            score = score + jnp.maximum(_dot_nt(iqp, ika_ref[0, :n]), 0.0) * iw[:, 2 * m:2 * m + 1]
            score = score + jnp.maximum(_dot_nt(iqp, ikb_ref[0, :n]), 0.0) * iw[:, 2 * m + 1:2 * m + 2]
        score = jnp.where(vis, score, NEG_INF)
        keep = jnp.where(vis, _topk_mask(score, n_sel, su_ref[...]), 0.0) > 0.5
        pairs_per_kv = C_HEADS // C_KV_HEADS // 2
        for m in range(C_HEADS // 2):
            c = m // pairs_per_kv
            qp = q_ref[0, :, m * LANES:(m + 1) * LANES]
            pa, inva = _masked_softmax(_dot_nt(qp, ka_ref[0, c, :n]), keep)
            pb, invb = _masked_softmax(_dot_nt(qp, kb_ref[0, c, :n]), keep)
            o_ref[0, :, m * LANES:(m + 1) * LANES] = (_dot(pa.astype(BF16), va_ref[0, c, :n]) * inva
                                                      + _dot(pb.astype(BF16), vb_ref[0, c, :n]) * invb)

    _for_visible_prefix(run, tq, ika_ref.shape[1], past)


def _dsa_attn(q, iq, iw, ika, ikb, ka, kb, va, vb, su, past, true_len):
    b, s, nq = q.shape
    lp = ika.shape[1]
    tq = _tile(s, 128)
    n_sel = min(C_TOPK, true_len // 4)
    qblk = lambda n: pl.BlockSpec((1, tq, n), lambda i, j: (i, j, 0))
    idx = pl.BlockSpec((1, lp, LANES), lambda i, j: (i, 0, 0))
    kv = pl.BlockSpec((1, C_KV_HEADS, lp, LANES), lambda i, j: (i, 0, 0, 0))
    return pl.pallas_call(
        functools.partial(_dsa_attn_kernel, past=past, true_len=true_len, n_sel=n_sel),
        grid=(b, s // tq),
        in_specs=[qblk(nq), qblk(nq), qblk(LANES), idx, idx, kv, kv, kv, kv, _full((LANES, LANES))],
        out_specs=qblk(nq),
        out_shape=jax.ShapeDtypeStruct((b, s, nq), F32),
        compiler_params=_cparams("parallel", "arbitrary"),
        name="dsa_attn",
    )(q, iq, iw, ika, ikb, ka, kb, va, vb, su)


def _gmlp_kernel(pd_ref, lg_ref, lb_ref, ws_ref, bs_ref, o_ref, dv_ref):
    span = pd_ref.shape[1]
    h = _gelu(pd_ref[0])
    u = h[:, :D_WIDTH]
    v = h[:, D_WIDTH:]
    mu = jnp.mean(v, axis=-1, keepdims=True)
    d = v - mu
    v = d * lax.rsqrt(jnp.mean(d * d, axis=-1, keepdims=True) + NORM_EPS) * lg_ref[...] + lb_ref[...]
    dv_ref[0] = v
    causal = lax.broadcasted_iota(I32, (span, span), 1) <= lax.broadcasted_iota(I32, (span, span), 0)
    gw = D_WIDTH // D_GROUPS
    mixed = [_dot(jnp.where(causal, ws_ref[g], 0.0).astype(BF16), v[:, g * gw:(g + 1) * gw].astype(BF16))
             for g in range(D_GROUPS)]
    o_ref[0] = u * (jnp.concatenate(mixed, axis=-1) + bs_ref[...])


def _gmlp(pd, w):
    b, s, _ = pd.shape
    span = D_SPAN if s % D_SPAN == 0 else s
    ws = w["d_ws"][:, :span, :span]
    bs = jnp.repeat(w["d_bs"][:, :span].T, D_WIDTH // D_GROUPS, axis=1)
    blk = lambda n: pl.BlockSpec((1, span, n), lambda i, j: (i, j, 0))
    return pl.pallas_call(
        _gmlp_kernel,
        grid=(b, s // span),
        in_specs=[blk(2 * D_WIDTH), _full((1, D_WIDTH)), _full((1, D_WIDTH)), _full((D_GROUPS, span, span)),
                  _full((span, D_WIDTH))],
        out_specs=[blk(D_WIDTH), blk(D_WIDTH)],
        out_shape=[jax.ShapeDtypeStruct((b, s, D_WIDTH), F32)] * 2,
        compiler_params=_cparams("parallel", "parallel"),
        name="gmlp",
    )(pd, w["d_ln_g"], w["d_ln_b"], ws, bs)


def _pad_cols(w, groups, width, slot, off=0):
    k = w.shape[0]
    w = w.reshape(k, groups, width)
    out = jnp.zeros((k, groups, slot), w.dtype).at[:, :, off:off + width].set(w)
    return out.reshape(k, groups * slot)


def _prep_weights(p):
    row = lambda v: v.reshape(1, -1).astype(F32)
    bf = lambda v: v.astype(BF16)
    w = {}
    w_in0 = p["l0_w_in"]
    w["l0_norm"] = row(p["l0_norm_mix"])
    w["w_cq"] = bf(w_in0[:, :A_D_CQ])
    w["w_ckv"] = bf(w_in0[:, A_D_CQ:A_D_CQ + A_D_C])
    a_cols = A_D_CQ + A_D_C + A_D_ROPE
    w["w_krp"] = bf(_pad_cols(w_in0[:, A_D_CQ + A_D_C:a_cols], 1, A_D_ROPE, LANES, A_D_NOPE))
    w["w_b"] = bf(w_in0[:, a_cols:])
    w["a_kv_norm"] = row(p["a_kv_norm"])
    w["a_q_norm"] = row(p["a_q_norm"])
    w["w_uq"] = bf(_pad_cols(p["a_w_uq"], A_HEADS, A_D_QK, LANES))
    ukv = p["a_w_ukv"].reshape(A_D_C, A_HEADS, A_D_NOPE + A_D_V)
    w["w_uk"] = bf(_pad_cols(ukv[:, :, :A_D_NOPE].reshape(A_D_C, -1), A_HEADS, A_D_NOPE, LANES))
    uv = ukv[:, :, A_D_NOPE:]
    uv_pad = jnp.zeros((A_D_C, A_HEADS // 2, 2, 2, A_D_V), F32)
    uv = uv.reshape(A_D_C, A_HEADS // 2, 2, A_D_V)
    uv_pad = uv_pad.at[:, :, 0, 0].set(uv[:, :, 0]).at[:, :, 1, 1].set(uv[:, :, 1])
    w["w_uv"] = bf(uv_pad.reshape(A_D_C, A_HEADS * LANES))
    w["a_q_gain"] = jnp.zeros((1, LANES), F32).at[0, :A_D_QK].set(p["a_q_gain"])
    w["a_k_gain"] = jnp.zeros((1, LANES), F32).at[0, :A_D_QK].set(p["a_k_gain"])
    w["perm_a"] = _rope_perm(LANES, LANES, A_D_NOPE, A_D_ROPE // 2)
    w["b_mu"] = row(p["b_mu"])
    w["b_w0"] = row(p["b_w0"])
    w["b_a0"] = row(p["b_a0"])
    zeros = jnp.zeros((B_W_LORA, B_WIDTH), F32)
    w["b_w2"] = bf(jnp.concatenate([p["b_w2"], zeros], axis=0))
    w["b_a2"] = bf(jnp.concatenate([zeros, p["b_a2"]], axis=0))
    w["b_g2"] = bf(p["b_g2"])
    for n in ("b_kk", "b_ka", "b_rk", "b_ln_g", "b_ln_b"):
        w[n] = row(p[n])
    w["bd512"] = _block_ones(B_WIDTH, B_HD)
    w["bd128"] = _block_ones(LANES, B_HD)
    w["w_out0a"] = bf(p["l0_w_out"][:A_HEADS * A_D_V])
    w["w_out0b"] = bf(p["l0_w_out"][A_HEADS * A_D_V:])
    w_in1 = p["l1_w_in"]
    nq = C_HEADS * C_HD
    nkv = C_KV_HEADS * C_HD
    o = 0
    w["w_cq1"] = bf(w_in1[:, o:o + nq]); o += nq
    w["w_ck1"] = bf(w_in1[:, o:o + nkv]); o += nkv
    w["w_cv1"] = bf(w_in1[:, o:o + nkv]); o += nkv
    w["w_ciq"] = bf(w_in1[:, o:o + nq]); o += nq
    w["w_cik"] = bf(_pad_cols(w_in1[:, o:o + C_IDX_D], 1, C_IDX_D, LANES)); o += C_IDX_D
    w["w_ciw"] = bf(_pad_cols(w_in1[:, o:o + C_IDX_HEADS], 1, C_IDX_HEADS, LANES)); o += C_IDX_HEADS
    w["w_d"] = bf(w_in1[:, o:])
    w["l1_norm"] = row(p["l1_norm_mix"])
    w["c_q_gain"] = row(jnp.tile(p["c_q_gain"], C_HEADS))
    w["c_k_gain"] = row(jnp.tile(p["c_k_gain"], C_KV_HEADS))
    w["c_kidx_gain"] = jnp.zeros((1, LANES), F32).at[0, :C_IDX_D].set(p["c_kidx_gain"])
    w["perm_c"] = _rope_perm(nq, C_HD, 0, C_HD // 2)
    w["su"] = (jnp.arange(LANES)[:, None] < jnp.arange(LANES)[None, :]).astype(BF16)
    w["d_ln_g"] = row(p["d_ln_g"])
    w["d_ln_b"] = row(p["d_ln_b"])
    w["d_ws"] = p["d_ws"]
    w["d_bs"] = p["d_bs"]
    w["w_out1a"] = bf(p["l1_w_out"][:nq])
    w["w_out1b"] = bf(p["l1_w_out"][nq:])
    for l in (0, 1):
        keys = p[f"l{l}_peer_keys"]
        half = PEER_DK // 2
        kp = jnp.zeros((PEER_HEADS, 2, PEER_N_KEYS, PEER_DK), F32)
        kp = kp.at[:, 0, :, :half].set(keys[:, 0]).at[:, 1, :, half:].set(keys[:, 1])
        w[f"peer{l}"] = {"keys": bf(kp), "ut": bf(p[f"l{l}_peer_u"]).T, "v": bf(p[f"l{l}_peer_v"]),
                         "norm": row(p[f"l{l}_norm_ffn"]), "wq": bf(p[f"l{l}_peer_wq"])}
    return w


def _pad_keys(x, lp):
    return jnp.pad(x, ((0, 0), (0, lp - x.shape[1])) + ((0, 0),) * (x.ndim - 2))


def _halves(x):
    z = jnp.zeros_like(x)
    return jnp.concatenate([x, z], axis=-1), jnp.concatenate([z, x], axis=-1)


def _trunk(x, st, w):
    b, s, _ = x.shape
    t = b * s
    past = st["a_ckv"].shape[1]
    true_len = past + s
    lp = -(-true_len // LANES) * LANES
    kpos = jnp.arange(lp, dtype=I32)

    cq, ckv, krp, pb = _in_proj0(x.reshape(t, D_MODEL), w)
    cos_a, sin_a = _rope_tables(kpos, A_D_ROPE // 2, LANES, A_D_NOPE, 1)
    q = _mla_q(cq.reshape(b, s, A_D_CQ), w, cos_a[past:true_len], sin_a[past:true_len])
    ckv3 = ckv.reshape(b, s, A_D_C)
    krp3 = krp.reshape(b, s, LANES)
    krope = krp3[:, :, A_D_NOPE:A_D_NOPE + A_D_ROPE]
    cache_krp = jnp.pad(st["a_krope"], ((0, 0), (0, 0), (A_D_NOPE, LANES - A_D_NOPE - A_D_ROPE)))
    ckv_all = _pad_keys(jnp.concatenate([st["a_ckv"], ckv3], axis=1), lp)
    krp_all = _pad_keys(jnp.concatenate([cache_krp, krp3], axis=1), lp)
    k_a, v_a = _mla_kv(ckv_all, krp_all, w, cos_a, sin_a)
    o_a = _mla_attn(q, k_a, v_a, past, true_len)

    pb3 = pb.reshape(b, s, B_PROJ)
    prev = jnp.concatenate([st["b_shift"][:, None, :], pb3[:, :-1]], axis=1)
    r, dec, k2, v, an, bv, g, bonus = _rwkv_prep(pb, prev.reshape(t, B_PROJ), w)
    pairs = B_HEADS // 2
    s0 = st["b_wkv"].reshape(b, pairs, 2, B_HD, B_HD).transpose(0, 1, 3, 2, 4).reshape(b, pairs, B_HD, LANES)
    y, s_out = _rwkv_scan([a.reshape(b, s, B_WIDTH) for a in (r, dec, k2, v, an, bv)], s0, w["bd128"])
    wkv = s_out.reshape(b, pairs, B_HD, 2, B_HD).transpose(0, 1, 3, 2, 4).reshape(b, B_HEADS, B_HD, B_HD)
    o_b = _rwkv_post(y.reshape(t, B_WIDTH), bonus, g, w)
    pw = w["peer0"]
    x1, xn1, pq1 = _out_proj(x.reshape(t, D_MODEL), o_a.reshape(t, -1), o_b, w["w_out0a"], w["w_out0b"],
                             pw["norm"], pw["wq"])
    x2 = _peer(x1, xn1, pq1, pw)

    nq = C_HEADS * C_HD
    cq1, ck1, cv1, ciq, cik, ciw, pd = _in_proj1(x2, w)
    pos_q = kpos[past:true_len]
    cos_c, sin_c = _rope_tables(pos_q, C_HD // 2, C_HD, 0, C_HEADS)
    three = lambda a: a.reshape(b, s, a.shape[-1])
    q_c, k_c, iq_c, ik_c = _dsa_prep(three(cq1), three(ck1), three(ciq), three(cik), w, cos_c, sin_c)
    c_k = k_c.reshape(b, s, C_KV_HEADS, C_HD)
    c_v = cv1.reshape(b, s, C_KV_HEADS, C_HD)
    c_kidx = ik_c[:, :, :C_IDX_D]
    k_all = _pad_keys(jnp.concatenate([st["c_k"], c_k], axis=1), lp).astype(BF16).transpose(0, 2, 1, 3)
    v_all = _pad_keys(jnp.concatenate([st["c_v"], c_v], axis=1), lp).astype(BF16).transpose(0, 2, 1, 3)
    ik_all = _pad_keys(jnp.concatenate([st["c_kidx"], c_kidx], axis=1), lp).astype(BF16)
    ka, kb = _halves(k_all)
    va, vb = _halves(v_all)
    ika, ikb = _halves(ik_all)
    o_c = _dsa_attn(q_c, iq_c, three(ciw), ika, ikb, ka, kb, va, vb, w["su"], past, true_len)
    o_d, d_v = _gmlp(three(pd), w)
    pw = w["peer1"]
    x3, xn3, pq3 = _out_proj(x2, o_c.reshape(t, nq), o_d.reshape(t, D_WIDTH), w["w_out1a"], w["w_out1b"],
                             pw["norm"], pw["wq"])
    y_out = _peer(x3, xn3, pq3, pw).reshape(b, s, D_MODEL)
    return y_out, [ckv3, krope, wkv, pb3[:, -1], c_k, c_v, c_kidx, d_v]


def kernel(x_prompt, x_sample, cache_a_ckv, cache_a_krope, state_b_wkv, state_b_shift, cache_c_k, cache_c_v, cache_c_kidx, l0_norm_mix, l0_w_in, a_q_norm, a_w_uq, a_kv_norm, a_w_ukv, a_q_gain, a_k_gain, b_mu, b_w0, b_w2, b_a0, b_a2, b_g2, b_kk, b_ka, b_rk, b_ln_g, b_ln_b, l0_w_out, l0_norm_ffn, l0_peer_wq, l0_peer_keys, l0_peer_u, l0_peer_v, l1_norm_mix, l1_w_in, c_q_gain, c_k_gain, c_kidx_gain, d_ln_g, d_ln_b, d_ws, d_bs, l1_w_out, l1_norm_ffn, l1_peer_wq, l1_peer_keys, l1_peer_u, l1_peer_v):
    params = dict(
        l0_norm_mix=l0_norm_mix, l0_w_in=l0_w_in, a_q_norm=a_q_norm, a_w_uq=a_w_uq, a_kv_norm=a_kv_norm,
        a_w_ukv=a_w_ukv, a_q_gain=a_q_gain, a_k_gain=a_k_gain, b_mu=b_mu, b_w0=b_w0, b_w2=b_w2, b_a0=b_a0,
        b_a2=b_a2, b_g2=b_g2, b_kk=b_kk, b_ka=b_ka, b_rk=b_rk.reshape(-1), b_ln_g=b_ln_g, b_ln_b=b_ln_b,
        l0_w_out=l0_w_out, l0_norm_ffn=l0_norm_ffn, l0_peer_wq=l0_peer_wq, l0_peer_keys=l0_peer_keys,
        l0_peer_u=l0_peer_u, l0_peer_v=l0_peer_v, l1_norm_mix=l1_norm_mix, l1_w_in=l1_w_in, c_q_gain=c_q_gain,
        c_k_gain=c_k_gain, c_kidx_gain=c_kidx_gain, d_ln_g=d_ln_g, d_ln_b=d_ln_b, d_ws=d_ws, d_bs=d_bs,
        l1_w_out=l1_w_out, l1_norm_ffn=l1_norm_ffn, l1_peer_wq=l1_peer_wq, l1_peer_keys=l1_peer_keys,
        l1_peer_u=l1_peer_u, l1_peer_v=l1_peer_v)
    w = _prep_weights(params)
    bp = x_prompt.shape[0]
    dt = x_prompt.dtype
    prompt_state = dict(
        a_ckv=jnp.zeros((bp, 0, A_D_C), dt), a_krope=jnp.zeros((bp, 0, A_D_ROPE), dt),
        b_wkv=jnp.zeros((bp, B_HEADS, B_HD, B_HD), dt), b_shift=jnp.zeros((bp, B_PROJ), dt),
        c_k=jnp.zeros((bp, 0, C_KV_HEADS, C_HD), dt), c_v=jnp.zeros((bp, 0, C_KV_HEADS, C_HD), dt),
        c_kidx=jnp.zeros((bp, 0, C_IDX_D), dt))
    sample_state = dict(a_ckv=cache_a_ckv, a_krope=cache_a_krope, b_wkv=state_b_wkv, b_shift=state_b_shift,
                        c_k=cache_c_k, c_v=cache_c_v, c_kidx=cache_c_kidx)
    y_p, new_p = _trunk(x_prompt, prompt_state, w)
    y_s, new_s = _trunk(x_sample, sample_state, w)
    return (y_p, y_s, *new_p[:7], *new_s)
```

```python
import functools
import math

import jax
import jax.numpy as jnp
from jax import lax
from jax.experimental import pallas as pl
from jax.experimental.pallas import tpu as pltpu

F32 = jnp.float32
BF16 = jnp.bfloat16
I32 = jnp.int32

D_MODEL = 1024
CHUNK = 64
CHUNK_SHIFT = 6
KEY_BLOCK = 512
ROPE_THETA = 10000.0
NORM_EPS = 1e-6
NEG_INF = -1e30
LANES = 128
SUBLANES = 8

A_HEADS = 8
A_D_NOPE = 64
A_D_ROPE = 32
A_D_QK = A_D_NOPE + A_D_ROPE
A_D_V = 64
A_D_CQ = 256
A_D_C = 128
B_HEADS = 8
B_HD = 64
B_WIDTH = B_HEADS * B_HD
B_W_LORA = 64
B_A_LORA = 64
B_G_LORA = 128
B_PROJ = 3 * B_WIDTH + B_W_LORA + B_A_LORA + B_G_LORA
B_LN_EPS = 64e-5
C_HEADS = 8
C_KV_HEADS = 2
C_HD = 64
C_IDX_HEADS = 8
C_IDX_D = 64
C_TOPK = 256
D_GROUPS = 4
D_WIDTH = 512
D_SPAN = 128
PEER_HEADS = 8
PEER_N_KEYS = 128
PEER_N_EXPERTS = PEER_N_KEYS * PEER_N_KEYS
PEER_DK = 128
PEER_TOPK = 16
N_PAIR_CAND = 80

VMEM_LIMIT = 48 * 1024 * 1024
NT = (((1,), (1,)), ((), ()))


def _tile(n, target):
    t = min(n, target)
    while n % t:
        t -= 8
    return t


def _cparams(*sem):
    return pltpu.CompilerParams(dimension_semantics=sem, vmem_limit_bytes=VMEM_LIMIT)


def _dot(a, b):
    return jnp.dot(a, b, preferred_element_type=F32)


def _dot_nt(a, b):
    return lax.dot_general(a, b, NT, preferred_element_type=F32)


def _split_dot(x, w, passes=2):
    hi = x.astype(BF16)
    acc = _dot(hi, w)
    rem = x - hi.astype(F32)
    for _ in range(passes - 1):
        part = rem.astype(BF16)
        acc = acc + _dot(part, w)
        rem = rem - part.astype(F32)
    return acc


def _rms(x, eps=NORM_EPS):
    return x * lax.rsqrt(jnp.mean(x * x, axis=-1, keepdims=True) + eps)


def _full(shape):
    nd = len(shape)
    return pl.BlockSpec(shape, lambda *_: (0,) * nd)


def _block_ones(n, seg):
    i = jnp.arange(n)
    return (i[:, None] // seg == i[None, :] // seg).astype(BF16)


def _rope_perm(n, seg, off, half):
    r = jnp.arange(n)[:, None]
    c = jnp.arange(n)[None, :]
    same = r // seg == c // seg
    rr = r % seg - off
    cc = c % seg - off
    plus = same & (rr >= 0) & (rr < half) & (cc == rr + half)
    minus = same & (cc >= 0) & (cc < half) & (rr == cc + half)
    return (plus.astype(F32) - minus.astype(F32)).astype(BF16)


def _rope_tables(pos, half, seg, off, reps):
    inv_freq = ROPE_THETA ** (-jnp.arange(half, dtype=F32) / half)
    ang = pos.astype(F32)[:, None] * inv_freq[None, :]
    n = pos.shape[0]
    edge = ((0, 0), (off, seg - off - 2 * half))
    cos = jnp.pad(jnp.tile(jnp.cos(ang), (1, 2)), edge, constant_values=1.0)
    sin = jnp.pad(jnp.tile(jnp.sin(ang), (1, 2)), edge)
    return jnp.tile(cos, (1, reps)), jnp.tile(sin, (1, reps))


def _in0_kernel(x_ref, g_ref, wcq_ref, wckv_ref, wkr_ref, wb_ref, kvn_ref,
                cq_ref, ckv_ref, krp_ref, pb_ref):
    xn = (_rms(x_ref[...]) * g_ref[...]).astype(BF16)
    cq_ref[...] = _dot(xn, wcq_ref[...])
    ckv_ref[...] = _rms(_dot(xn, wckv_ref[...])) * kvn_ref[...]
    krp_ref[...] = _dot(xn, wkr_ref[...])
    pb_ref[...] = _dot(xn, wb_ref[...])


def _in_proj0(x2, w):
    t = x2.shape[0]
    tm = _tile(t, 512)
    row = lambda n: pl.BlockSpec((tm, n), lambda i: (i, 0))
    return pl.pallas_call(
        _in0_kernel,
        grid=(t // tm,),
        in_specs=[row(D_MODEL), _full((1, D_MODEL)), _full((D_MODEL, A_D_CQ)), _full((D_MODEL, A_D_C)),
                  _full((D_MODEL, LANES)), _full((D_MODEL, B_PROJ)), _full((1, A_D_C))],
        out_specs=[row(A_D_CQ), row(A_D_C), row(LANES), row(B_PROJ)],
        out_shape=[jax.ShapeDtypeStruct((t, A_D_CQ), F32), jax.ShapeDtypeStruct((t, A_D_C), F32),
                   jax.ShapeDtypeStruct((t, LANES), F32), jax.ShapeDtypeStruct((t, B_PROJ), F32)],
        compiler_params=_cparams("parallel"),
        name="in_proj0",
    )(x2, w["l0_norm"], w["w_cq"], w["w_ckv"], w["w_krp"], w["w_b"], w["a_kv_norm"])


def _head_norm_rope(xh, gain, cos, sin, perm):
    ss = jnp.sum(xh * xh, axis=-1, keepdims=True) * (1.0 / A_D_QK)
    xh = xh * lax.rsqrt(ss + NORM_EPS) * gain
    return xh * cos + _split_dot(xh, perm) * sin


def _mla_q_kernel(cq_ref, qn_ref, wq_ref, gain_ref, cos_ref, sin_ref, perm_ref, o_ref):
    cqn = (_rms(cq_ref[0]) * qn_ref[...]).astype(BF16)
    q = _dot(cqn, wq_ref[...])
    cos, sin, perm, gain = cos_ref[...], sin_ref[...], perm_ref[...], gain_ref[...]
    for h in range(A_HEADS):
        qh = _head_norm_rope(q[:, h * LANES:(h + 1) * LANES], gain, cos, sin, perm)
        o_ref[0, :, h * LANES:(h + 1) * LANES] = (qh * (A_D_QK ** -0.5)).astype(BF16)


def _mla_q(cq, w, cos, sin):
    b, s, _ = cq.shape
    ts = _tile(s, 256)
    return pl.pallas_call(
        _mla_q_kernel,
        grid=(b, s // ts),
        in_specs=[pl.BlockSpec((1, ts, A_D_CQ), lambda i, j: (i, j, 0)), _full((1, A_D_CQ)),
                  _full((A_D_CQ, A_HEADS * LANES)), _full((1, LANES)),
                  pl.BlockSpec((ts, LANES), lambda i, j: (j, 0)), pl.BlockSpec((ts, LANES), lambda i, j: (j, 0)),
                  _full((LANES, LANES))],
        out_specs=pl.BlockSpec((1, ts, A_HEADS * LANES), lambda i, j: (i, j, 0)),
        out_shape=jax.ShapeDtypeStruct((b, s, A_HEADS * LANES), BF16),
        compiler_params=_cparams("parallel", "parallel"),
        name="mla_q",
    )(cq, w["a_q_norm"], w["w_uq"], w["a_q_gain"], cos, sin, w["perm_a"])


def _mla_kv_kernel(ckv_ref, krp_ref, wk_ref, wv_ref, gain_ref, cos_ref, sin_ref, perm_ref, k_ref, v_ref):
    c = ckv_ref[0].astype(BF16)
    k = _dot(c, wk_ref[...])
    v_ref[0] = _dot(c, wv_ref[...]).astype(BF16)
    krp = krp_ref[0]
    cos, sin, perm, gain = cos_ref[...], sin_ref[...], perm_ref[...], gain_ref[...]
    for h in range(A_HEADS):
        kh = _head_norm_rope(k[:, h * LANES:(h + 1) * LANES] + krp, gain, cos, sin, perm)
        k_ref[0, :, h * LANES:(h + 1) * LANES] = kh.astype(BF16)


def _mla_kv(ckv_all, krp_all, w, cos, sin):
    b, lp, _ = ckv_all.shape
    tl = _tile(lp, 256)
    blk = lambda n: pl.BlockSpec((1, tl, n), lambda i, j: (i, j, 0))
    tab = pl.BlockSpec((tl, LANES), lambda i, j: (j, 0))
    return pl.pallas_call(
        _mla_kv_kernel,
        grid=(b, lp // tl),
        in_specs=[blk(A_D_C), blk(LANES), _full((A_D_C, A_HEADS * LANES)), _full((A_D_C, A_HEADS * LANES)),
                  _full((1, LANES)), tab, tab, _full((LANES, LANES))],
        out_specs=[blk(A_HEADS * LANES), blk(A_HEADS * LANES)],
        out_shape=[jax.ShapeDtypeStruct((b, lp, A_HEADS * LANES), BF16)] * 2,
        compiler_params=_cparams("parallel", "parallel"),
        name="mla_kv",
    )(ckv_all, krp_all, w["w_uk"], w["w_uv"], w["a_k_gain"], cos, sin, w["perm_a"])


def _visible(tq, lp, q0, true_len):
    qpos = q0 + lax.broadcasted_iota(I32, (tq, 1), 0)
    kpos = lax.broadcasted_iota(I32, (1, lp), 1)
    kchunk = jnp.where(kpos < true_len, kpos >> CHUNK_SHIFT, jnp.int32(1 << 30))
    return kchunk <= (qpos >> CHUNK_SHIFT)


def _masked_softmax(s, keep):
    s = jnp.where(keep, s, NEG_INF)
    p = jnp.exp(s - jnp.max(s, axis=-1, keepdims=True))
    return p, 1.0 / jnp.sum(p, axis=-1, keepdims=True)


def _for_visible_prefix(run, tq, lp, past):
    q0 = past + pl.program_id(1) * tq
    need = (((q0 + tq - 1) >> CHUNK_SHIFT) + 1) << CHUNK_SHIFT
    sizes = [min(c * KEY_BLOCK, lp) for c in range(1, -(-lp // KEY_BLOCK) + 1)]
    case = (need + (KEY_BLOCK - 1)) // KEY_BLOCK
    for c, n in enumerate(sizes, start=1):
        pl.when(case == c)(functools.partial(run, n, q0))


def _mla_attn_kernel(q_ref, k_ref, v_ref, o_ref, *, past, true_len):
    tq = q_ref.shape[1]

    def run(n, q0):
        vis = _visible(tq, n, q0, true_len)
        for m in range(A_HEADS // 2):
            acc = None
            for h in (2 * m, 2 * m + 1):
                sl = slice(h * LANES, (h + 1) * LANES)
                p, inv = _masked_softmax(_dot_nt(q_ref[0, :, sl], k_ref[0, :n, sl]), vis)
                o = _dot(p.astype(BF16), v_ref[0, :n, sl]) * inv
                acc = o if acc is None else acc + o
            o_ref[0, :, m * LANES:(m + 1) * LANES] = acc

    _for_visible_prefix(run, tq, k_ref.shape[1], past)


def _mla_attn(q, k, v, past, true_len):
    b, s, _ = q.shape
    lp = k.shape[1]
    tq = _tile(s, 128)
    kv = pl.BlockSpec((1, lp, A_HEADS * LANES), lambda i, j: (i, 0, 0))
    return pl.pallas_call(
        functools.partial(_mla_attn_kernel, past=past, true_len=true_len),
        grid=(b, s // tq),
        in_specs=[pl.BlockSpec((1, tq, A_HEADS * LANES), lambda i, j: (i, j, 0)), kv, kv],
        out_specs=pl.BlockSpec((1, tq, A_HEADS * A_D_V), lambda i, j: (i, j, 0)),
        out_shape=jax.ShapeDtypeStruct((b, s, A_HEADS * A_D_V), F32),
        compiler_params=_cparams("parallel", "arbitrary"),
        name="mla_attn",
    )(q, k, v)


def _rwkv_prep_kernel(pb_ref, shift_ref, mu_ref, w0_ref, w2_ref, a0_ref, a2_ref, g2_ref, kk_ref, ka_ref, rk_ref,
                      bd_ref, r_ref, w_ref, k_ref, v_ref, an_ref, bv_ref, g_ref, bonus_ref, last_ref):
    pb = pb_ref[0]
    ts = pb.shape[0]

    @pl.when(pl.program_id(1) == 0)
    def _():
        last_ref[...] = shift_ref[0]

    first = lax.broadcasted_iota(I32, (ts, 1), 0) == 0
    prev = jnp.where(first, last_ref[...], pltpu.roll(pb, 1, axis=0))
    last_ref[...] = pb[ts - 1:ts, :]
    xs = pb + (prev - pb) * mu_ref[...]
    r = xs[:, 0:B_WIDTH]
    k = xs[:, B_WIDTH:2 * B_WIDTH]
    v = xs[:, 2 * B_WIDTH:3 * B_WIDTH]
    wa = xs[:, 3 * B_WIDTH:3 * B_WIDTH + B_W_LORA + B_A_LORA]
    gl = xs[:, 3 * B_WIDTH + B_W_LORA + B_A_LORA:]
    bd = bd_ref[...]
    z = -(w0_ref[...] + _dot(jnp.tanh(wa).astype(BF16), w2_ref[...]))
    softplus = jnp.maximum(z, 0.0) + jnp.log(1.0 + jnp.exp(-jnp.abs(z)))
    w_ref[0] = jnp.exp(-jnp.exp(-softplus - 0.5))
    a = jax.nn.sigmoid(a0_ref[...] + _dot(wa.astype(BF16), a2_ref[...]))
    g_ref[0] = _dot(jax.nn.sigmoid(gl).astype(BF16), g2_ref[...])
    kk = k * kk_ref[...]
    kk = kk * lax.rsqrt(_split_dot(kk * kk, bd) + 1e-12)
    k2 = k * (1.0 + (a - 1.0) * ka_ref[...])
    r_ref[0] = r
    k_ref[0] = k2
    v_ref[0] = v
    an_ref[0] = -kk
    bv_ref[0] = kk * a
    bonus_ref[0] = _split_dot(r * k2 * rk_ref[...], bd) * v


def _rwkv_prep(pb3, shift, w):
    b, s, _ = pb3.shape
    ts = _tile(s, 256)
    blk = lambda n: pl.BlockSpec((1, ts, n), lambda i, j: (i, j, 0))
    vec = _full((1, B_WIDTH))
    return pl.pallas_call(
        _rwkv_prep_kernel,
        grid=(b, s // ts),
        in_specs=[blk(B_PROJ), pl.BlockSpec((1, 1, B_PROJ), lambda i, j: (i, 0, 0)), _full((1, B_PROJ)), vec,
                  _full((LANES, B_WIDTH)), vec, _full((LANES, B_WIDTH)), _full((B_G_LORA, B_WIDTH)), vec, vec, vec,
                  _full((B_WIDTH, B_WIDTH))],
        out_specs=[blk(B_WIDTH)] * 8,
        out_shape=[jax.ShapeDtypeStruct((b, s, B_WIDTH), F32)] * 8,
        scratch_shapes=[pltpu.VMEM((1, B_PROJ), F32)],
        compiler_params=_cparams("parallel", "arbitrary"),
        name="rwkv_prep",
    )(pb3, shift[:, None, :], w["b_mu"], w["b_w0"], w["b_w2"], w["b_a0"], w["b_a2"], w["b_g2"], w["b_kk"], w["b_ka"],
      w["b_rk"], w["bd512"])


def _rwkv_scan_kernel(r_ref, w_ref, k_ref, v_ref, an_ref, bv_ref, s0_ref, bd_ref, y_ref, sout_ref, st_ref):
    nb, ts, _ = r_ref.shape
    pairs = B_HEADS // 2

    @pl.when(pl.program_id(1) == 0)
    def _():
        st_ref[...] = s0_ref[...]

    bd = bd_ref[...]
    lane = lax.broadcasted_iota(I32, (B_HD, LANES), 1)
    sub = lax.broadcasted_iota(I32, (B_HD, LANES), 0)
    diag = jnp.where((lane & (B_HD - 1)) == sub, 1.0, 0.0).astype(F32)[None]

    def seg_bcast(x, passes):
        return _split_dot(x.reshape(nb * B_HD, LANES), bd, passes).reshape(nb, B_HD, LANES)

    def steps(c, carry):
        t0 = pl.multiple_of(c * SUBLANES, SUBLANES)
        for p in range(pairs):
            sl = slice(p * LANES, (p + 1) * LANES)
            r, w, k, v, an, bv = [ref[:, pl.ds(t0, SUBLANES), sl] for ref in
                                  (r_ref, w_ref, k_ref, v_ref, an_ref, bv_ref)]
            st = st_ref[:, p]
            ys = []
            for j in range(SUBLANES):
                row = lambda x: x[:, j:j + 1, :]
                sa = seg_bcast(st * row(an), 1)
                vcol = seg_bcast(diag * row(v), 1)
                st = st * row(w) + sa * row(bv) + vcol * row(k)
                yb = seg_bcast(st * row(r), 1)
                ys.append(jnp.sum(yb * diag, axis=1, keepdims=True))
            st_ref[:, p] = st
            y_ref[:, pl.ds(t0, SUBLANES), sl] = jnp.concatenate(ys, axis=1)
        return carry

    lax.fori_loop(0, ts // SUBLANES, steps, 0)

    @pl.when(pl.program_id(1) == pl.num_programs(1) - 1)
    def _():
        sout_ref[...] = st_ref[...]


def _rwkv_scan(seqs, s0, bd128):
    b, s, _ = seqs[0].shape
    nb = _tile(b, 8)
    ts = _tile(s, 64)
    pairs = B_HEADS // 2
    seq = pl.BlockSpec((nb, ts, B_WIDTH), lambda i, j: (i, j, 0))
    sts = pl.BlockSpec((nb, pairs, B_HD, LANES), lambda i, j: (i, 0, 0, 0))
    return pl.pallas_call(
        _rwkv_scan_kernel,
        grid=(b // nb, s // ts),
        in_specs=[seq] * 6 + [sts, _full((LANES, LANES))],
        out_specs=[seq, sts],
        out_shape=[jax.ShapeDtypeStruct((b, s, B_WIDTH), F32), jax.ShapeDtypeStruct((b, pairs, B_HD, LANES), F32)],
        scratch_shapes=[pltpu.VMEM((nb, pairs, B_HD, LANES), F32)],
        compiler_params=_cparams("parallel", "arbitrary"),
        name="rwkv_scan",
    )(*seqs, s0, bd128)


def _rwkv_post_kernel(y_ref, bonus_ref, g_ref, lg_ref, lb_ref, bd_ref, o_ref):
    bd = bd_ref[...]
    y = y_ref[...]
    d = y - _split_dot(y, bd) * (1.0 / B_HD)
    var = _split_dot(d * d, bd) * (1.0 / B_HD)
    yn = d * lax.rsqrt(var + B_LN_EPS) * lg_ref[...] + lb_ref[...]
    o_ref[...] = (yn + bonus_ref[...]) * g_ref[...]


def _rwkv_post(y2, bonus2, g2, w):
    t = y2.shape[0]
    tm = _tile(t, 512)
    row = pl.BlockSpec((tm, B_WIDTH), lambda i: (i, 0))
    vec = _full((1, B_WIDTH))
    return pl.pallas_call(
        _rwkv_post_kernel,
        grid=(t // tm,),
        in_specs=[row, row, row, vec, vec, _full((B_WIDTH, B_WIDTH))],
        out_specs=row,
        out_shape=jax.ShapeDtypeStruct((t, B_WIDTH), F32),
        compiler_params=_cparams("parallel"),
        name="rwkv_post",
    )(y2, bonus2, g2, w["b_ln_g"], w["b_ln_b"], w["bd512"])


def _out_proj_kernel(x_ref, o1_ref, o2_ref, w1_ref, w2_ref, gf_ref, wq_ref, xo_ref, xn_ref, q_ref):
    x = x_ref[...] + _dot(o1_ref[...].astype(BF16), w1_ref[...]) + _dot(o2_ref[...].astype(BF16), w2_ref[...])
    xo_ref[...] = x
    xn = (_rms(x) * gf_ref[...]).astype(BF16)
    xn_ref[...] = xn
    q_ref[...] = _dot(xn, wq_ref[...])


def _out_proj(x2, o1, o2, w1, w2, gf, wq):
    t = x2.shape[0]
    tm = _tile(t, 512)
    row = lambda n: pl.BlockSpec((tm, n), lambda i: (i, 0))
    half = o1.shape[1]
    return pl.pallas_call(
        _out_proj_kernel,
        grid=(t // tm,),
        in_specs=[row(D_MODEL), row(half), row(half), _full((half, D_MODEL)), _full((half, D_MODEL)),
                  _full((1, D_MODEL)), _full((D_MODEL, PEER_HEADS * PEER_DK))],
        out_specs=[row(D_MODEL), row(D_MODEL), row(PEER_HEADS * PEER_DK)],
        out_shape=[jax.ShapeDtypeStruct((t, D_MODEL), F32), jax.ShapeDtypeStruct((t, D_MODEL), BF16),
                   jax.ShapeDtypeStruct((t, PEER_HEADS * PEER_DK), F32)],
        compiler_params=_cparams("parallel"),
        name="out_proj",
    )(x2, o1, o2, w1, w2, gf, wq)


def _top16_rows(sc_ref, val_ref, idx_ref):
    nc, n, tb = sc_ref.shape
    rowi = lax.broadcasted_iota(I32, (n, tb), 0).astype(F32)

    def body(r, carry):
        for c in range(nc):
            sc = sc_ref[c]
            m = jnp.max(sc, axis=0, keepdims=True)
            ix = jnp.min(jnp.where(sc == m, rowi, float(n)), axis=0, keepdims=True)
            val_ref[c, pl.ds(r, 1), :] = m
            idx_ref[c, pl.ds(r, 1), :] = ix
            sc_ref[c] = jnp.where(rowi == ix, -jnp.inf, sc)
        return carry

    lax.fori_loop(0, PEER_TOPK, body, 0)


def _pair_candidates(s1, s2, x1, x2):
    sub = lax.broadcasted_iota(I32, (SUBLANES, s1.shape[1]), 0).astype(F32)
    cand, flat, eid = [], [], []
    for a in range(8):
        cand.append(s1[a:a + 1] + s2[0:8])
        flat.append(a * PEER_TOPK + sub)
        eid.append(x1[a:a + 1] * PEER_N_KEYS + x2[0:8])
    cand.append(s1[0:1] + s2[8:16])
    flat.append(8 + sub)
    eid.append(x1[0:1] * PEER_N_KEYS + x2[8:16])
    cand.append(s1[8:16] + s2[0:1])
    flat.append((8 + sub) * PEER_TOPK)
    eid.append(x1[8:16] * PEER_N_KEYS + x2[0:1])
    return jnp.concatenate(cand, axis=0), jnp.concatenate(flat, axis=0), jnp.concatenate(eid, axis=0)


def _peer_route_kernel(q_ref, keys_ref, e1_ref, e2_ref, gate_ref, sc_ref, topv, topi, cand_ref, eid_ref, selv, sele):
    for h in range(PEER_HEADS):
        qh = q_ref[:, h * PEER_DK:(h + 1) * PEER_DK].astype(BF16)
        sc_ref[2 * h] = _dot_nt(keys_ref[h, 0], qh)
        sc_ref[2 * h + 1] = _dot_nt(keys_ref[h, 1], qh)
    _top16_rows(sc_ref, topv, topi)
    for h in range(PEER_HEADS):
        cand, flat, eid = _pair_candidates(topv[2 * h], topv[2 * h + 1], topi[2 * h], topi[2 * h + 1])
        cand_ref[h] = cand
        eid_ref[h] = eid

    def body(r, carry):
        for h in range(PEER_HEADS):
            cand = cand_ref[h]
            m = jnp.max(cand, axis=0, keepdims=True)
            f = jnp.min(jnp.where(cand == m, flat, 1e9), axis=0, keepdims=True)
            hit = flat == f
            selv[h, pl.ds(r, 1), :] = m
            sele[h, pl.ds(r, 1), :] = jnp.max(jnp.where(hit, eid_ref[h], -1.0), axis=0, keepdims=True)
            cand_ref[h] = jnp.where(hit, -jnp.inf, cand)
        return carry

    lax.fori_loop(0, PEER_TOPK, body, 0)
    for h in range(PEER_HEADS):
        rows = slice(h * PEER_TOPK, (h + 1) * PEER_TOPK)
        top = selv[h]
        e = jnp.exp(top - top[0:1])
        gate_ref[rows, :] = e * (1.0 / jnp.sum(e, axis=0, keepdims=True))
        ex = sele[h].astype(I32)
        e1_ref[rows, :] = ex >> 7
        e2_ref[rows, :] = ex & (PEER_N_KEYS - 1)


def _peer_route(q2, keys_pad):
    t = q2.shape[0]
    tb = _tile(t, 128)
    nk = PEER_HEADS * PEER_TOPK
    out = pl.BlockSpec((nk, tb), lambda i: (0, i))
    return pl.pallas_call(
        _peer_route_kernel,
        grid=(t // tb,),
        in_specs=[pl.BlockSpec((tb, PEER_HEADS * PEER_DK), lambda i: (i, 0)),
                  _full((PEER_HEADS, 2, PEER_N_KEYS, PEER_DK))],
        out_specs=[out, out, out],
        out_shape=[jax.ShapeDtypeStruct((nk, t), I32)] * 2 + [jax.ShapeDtypeStruct((nk, t), F32)],
        scratch_shapes=[pltpu.VMEM((2 * PEER_HEADS, PEER_N_KEYS, tb), F32),
                        pltpu.VMEM((2 * PEER_HEADS, PEER_TOPK, tb), F32), pltpu.VMEM((2 * PEER_HEADS, PEER_TOPK, tb), F32),
                        pltpu.VMEM((PEER_HEADS, N_PAIR_CAND, tb), F32), pltpu.VMEM((PEER_HEADS, N_PAIR_CAND, tb), F32),
                        pltpu.VMEM((PEER_HEADS, PEER_TOPK, tb), F32), pltpu.VMEM((PEER_HEADS, PEER_TOPK, tb), F32)],
        compiler_params=_cparams("parallel"),
        name="peer_route",
    )(q2, keys_pad)


def _peer_gate_kernel(e1_ref, e2_ref, gate_ref, o_ref, scr):
    tg = e1_ref.shape[0]
    nk = PEER_HEADS * PEER_TOPK
    sub = lax.broadcasted_iota(I32, (PEER_N_KEYS, nk), 0)

    def body(c, carry):
        t0 = pl.multiple_of(c * SUBLANES, SUBLANES)
        e1 = e1_ref[pl.ds(t0, SUBLANES), :]
        e2 = e2_ref[pl.ds(t0, SUBLANES), :]
        g = gate_ref[pl.ds(t0, SUBLANES), :]
        mats = []
        for j in range(SUBLANES):
            a_t = jnp.where(sub == e1[j:j + 1], g[j:j + 1], 0.0).astype(BF16)
            b_t = jnp.where(sub == e2[j:j + 1], 1.0, 0.0).astype(BF16)
            mats.append(_dot_nt(a_t, b_t))
        scr[:, pl.ds(t0, SUBLANES), :] = jnp.swapaxes(jnp.stack(mats, axis=0), 0, 1)
        return carry

    lax.fori_loop(0, tg // SUBLANES, body, 0, unroll=4)
    o_ref[...] = scr[...].astype(BF16)


def _peer_gates(e1, e2, gate):
    t = e1.shape[0]
    tg = _tile(t, 64)
    row = pl.BlockSpec((tg, PEER_HEADS * PEER_TOPK), lambda i: (i, 0))
    return pl.pallas_call(
        _peer_gate_kernel,
        grid=(t // tg,),
        in_specs=[row, row, row],
        out_specs=pl.BlockSpec((PEER_N_KEYS, tg, PEER_N_KEYS), lambda i: (0, i, 0)),
        out_shape=jax.ShapeDtypeStruct((PEER_N_KEYS, t, PEER_N_KEYS), BF16),
        scratch_shapes=[pltpu.VMEM((PEER_N_KEYS, tg, PEER_N_KEYS), F32)],
        compiler_params=_cparams("parallel"),
        name="peer_gates",
    )(e1, e2, gate)


def _gelu(x):
    return 0.5 * x * (1.0 + lax.erf(x * (2.0 ** -0.5)))


def _peer_dense_kernel(xn_ref, ut_ref, v_ref, g_ref, x_ref, o_ref):
    @pl.when(pl.program_id(1) == 0)
    def _():
        o_ref[...] = x_ref[...]

    h = _gelu(_dot(xn_ref[...], ut_ref[...]))
    nblk = g_ref.shape[0]
    gh = [(h[:, c * LANES:(c + 1) * LANES] * g_ref[c].astype(F32)).astype(BF16) for c in range(nblk)]
    o_ref[...] += _dot(jnp.concatenate(gh, axis=-1), v_ref[...])


def _peer_dense(xn, ut, vtab, gates, x2):
    t = xn.shape[0]
    tb = _tile(t, 1024)
    eb = 1024
    return pl.pallas_call(
        _peer_dense_kernel,
        grid=(t // tb, PEER_N_EXPERTS // eb),
        in_specs=[pl.BlockSpec((tb, D_MODEL), lambda i, j: (i, 0)),
                  pl.BlockSpec((D_MODEL, eb), lambda i, j: (0, j)),
                  pl.BlockSpec((eb, D_MODEL), lambda i, j: (j, 0)),
                  pl.BlockSpec((eb // PEER_N_KEYS, tb, PEER_N_KEYS), lambda i, j: (j, i, 0)),
                  pl.BlockSpec((tb, D_MODEL), lambda i, j: (i, 0))],
        out_specs=pl.BlockSpec((tb, D_MODEL), lambda i, j: (i, 0)),
        out_shape=jax.ShapeDtypeStruct((t, D_MODEL), F32),
        compiler_params=_cparams("parallel", "arbitrary"),
        name="peer_dense",
    )(xn, ut, vtab, gates, x2)


def _peer(x2, xn, q2, pw):
    e1, e2, gate = _peer_route(q2, pw["keys"])
    gates = _peer_gates(e1.T, e2.T, gate.T)
    return _peer_dense(xn, pw["ut"], pw["v"], gates, x2)


def _in1_kernel(x_ref, g_ref, wq_ref, wk_ref, wv_ref, wiq_ref, wik_ref, wiw_ref, wd_ref,
                q_ref, k_ref, v_ref, iq_ref, ik_ref, iw_ref, d_ref):
    xn = (_rms(x_ref[...]) * g_ref[...]).astype(BF16)
    for w_ref, o_ref in ((wq_ref, q_ref), (wk_ref, k_ref), (wv_ref, v_ref), (wiq_ref, iq_ref),
                         (wik_ref, ik_ref), (wiw_ref, iw_ref), (wd_ref, d_ref)):
        o_ref[...] = _dot(xn, w_ref[...])


def _in_proj1(x2, w):
    t = x2.shape[0]
    tm = _tile(t, 512)
    widths = [C_HEADS * C_HD, LANES, LANES, C_IDX_HEADS * C_IDX_D, LANES, LANES, 2 * D_WIDTH]
    row = lambda n: pl.BlockSpec((tm, n), lambda i: (i, 0))
    return pl.pallas_call(
        _in1_kernel,
        grid=(t // tm,),
        in_specs=[row(D_MODEL), _full((1, D_MODEL))] + [_full((D_MODEL, n)) for n in widths],
        out_specs=[row(n) for n in widths],
        out_shape=[jax.ShapeDtypeStruct((t, n), F32) for n in widths],
        compiler_params=_cparams("parallel"),
        name="in_proj1",
    )(x2, w["l1_norm"], w["w_cq1"], w["w_ck1"], w["w_cv1"], w["w_ciq"], w["w_cik"], w["w_ciw"], w["w_d"])


def _dsa_prep_kernel(q_ref, k_ref, iq_ref, ik_ref, qg_ref, kg_ref, ig_ref, cos_ref, sin_ref, bd_ref, perm_ref,
                     qo_ref, ko_ref, iqo_ref, iko_ref):
    cos, sin, bd, perm = cos_ref[...], sin_ref[...], bd_ref[...], perm_ref[...]

    def norm(x, gain, n):
        ss = _split_dot(x * x, bd[:n, :n]) * (1.0 / C_HD)
        return x * lax.rsqrt(ss + NORM_EPS) * gain

    def rope(x, n):
        return x * cos[:, :n] + _split_dot(x, perm[:n, :n]) * sin[:, :n]

    nq = C_HEADS * C_HD
    qo_ref[0] = (rope(norm(q_ref[0], qg_ref[...], nq), nq) * (C_HD ** -0.5)).astype(BF16)
    ko_ref[0] = rope(norm(k_ref[0], kg_ref[...], LANES), LANES)
    iqo_ref[0] = (rope(iq_ref[0], nq) * (C_IDX_D ** -0.5)).astype(BF16)
    iko_ref[0] = rope(norm(ik_ref[0], ig_ref[...], LANES), LANES)


def _dsa_prep(q, k, iq, ik, w, cos, sin):
    b, s, _ = q.shape
    ts = _tile(s, 256)
    nq = C_HEADS * C_HD
    blk = lambda n: pl.BlockSpec((1, ts, n), lambda i, j: (i, j, 0))
    tab = pl.BlockSpec((ts, nq), lambda i, j: (j, 0))
    return pl.pallas_call(
        _dsa_prep_kernel,
        grid=(b, s // ts),
        in_specs=[blk(nq), blk(LANES), blk(nq), blk(LANES), _full((1, nq)), _full((1, LANES)), _full((1, LANES)),
                  tab, tab, _full((nq, nq)), _full((nq, nq))],
        out_specs=[blk(nq), blk(LANES), blk(nq), blk(LANES)],
        out_shape=[jax.ShapeDtypeStruct((b, s, nq), BF16), jax.ShapeDtypeStruct((b, s, LANES), F32),
                   jax.ShapeDtypeStruct((b, s, nq), BF16), jax.ShapeDtypeStruct((b, s, LANES), F32)],
        compiler_params=_cparams("parallel", "parallel"),
        name="dsa_prep",
    )(q, k, iq, ik, w["c_q_gain"], w["c_k_gain"], w["c_kidx_gain"], cos, sin, w["bd512"], w["perm_c"])


def _order_key(x):
    bits = pltpu.bitcast(jnp.where(x == 0.0, 0.0, x), I32)
    return jnp.where(bits < 0, bits ^ jnp.int32(0x7FFFFFFF), bits)


def _topk_mask(score, n_sel, su):
    tq, lp = score.shape
    key = _order_key(score)

    def body(i, tau):
        trial = tau + lax.shift_left(jnp.int32(1), 31 - i)
        cnt = jnp.sum(jnp.where(key >= trial, 1.0, 0.0), axis=-1, keepdims=True)
        return jnp.where(cnt >= n_sel, trial, tau)

    tau = lax.fori_loop(0, 32, body, jnp.full((tq, 1), -(2 ** 31), I32))
    gt = jnp.where(key > tau, 1.0, 0.0)
    eq = jnp.where(key == tau, 1.0, 0.0)
    need = n_sel - jnp.sum(gt, axis=-1, keepdims=True)
    parts = []
    before = jnp.zeros((tq, 1), F32)
    for c in range(lp // LANES):
        eqc = eq[:, c * LANES:(c + 1) * LANES]
        rank = before + _dot(eqc.astype(BF16), su)
        parts.append(gt[:, c * LANES:(c + 1) * LANES] + jnp.where(rank < need, eqc, 0.0))
        before = before + jnp.sum(eqc, axis=-1, keepdims=True)
    return jnp.concatenate(parts, axis=-1)


def _dsa_attn_kernel(q_ref, iq_ref, iw_ref, ik_ref, k_ref, v_ref, su_ref, o_ref, *, past, true_len, n_sel):
    tq = q_ref.shape[1]
    low = lax.broadcasted_iota(I32, (1, LANES), 1) < C_HD
    halves = lambda x: (jnp.where(low, x, jnp.zeros((), x.dtype)), jnp.where(low, jnp.zeros((), x.dtype), x))

    def run(n, q0):
        vis = _visible(tq, n, q0, true_len)
        iw = iw_ref[0] * (C_IDX_HEADS ** -0.5)
        ik2 = ik_ref[0, :n]
        score = jnp.zeros((tq, n), F32)
        for m in range(C_IDX_HEADS // 2):
            iqa, iqb = halves(iq_ref[0, :, m * LANES:(m + 1) * LANES])
            score = score + jnp.maximum(_dot_nt(iqa, ik2), 0.0) * iw[:, 2 * m:2 * m + 1]
            score = score + jnp.maximum(_dot_nt(iqb, ik2), 0.0) * iw[:, 2 * m + 1:2 * m + 2]
        score = jnp.where(vis, score, NEG_INF)
        keep = jnp.where(vis, _topk_mask(score, n_sel, su_ref[...]), 0.0) > 0.5
        k = k_ref[0, :n]
        va, vb = halves(v_ref[0, :n])
        for m in range(C_HEADS // 2):
            qa, qb = halves(q_ref[0, :, m * LANES:(m + 1) * LANES])
            pa, inva = _masked_softmax(_dot_nt(qa, k), keep)
            pb, invb = _masked_softmax(_dot_nt(qb, k), keep)
            o_ref[0, :, m * LANES:(m + 1) * LANES] = (_dot(pa.astype(BF16), va) * inva
                                                      + _dot(pb.astype(BF16), vb) * invb)

    _for_visible_prefix(run, tq, ik_ref.shape[1], past)


def _dsa_attn(q, iq, iw, ik2, k, v, su, past, true_len):
    b, s, nq = q.shape
    lp = ik2.shape[1]
    tq = _tile(s, 128)
    n_sel = min(C_TOPK, true_len // 4)
    qblk = lambda n: pl.BlockSpec((1, tq, n), lambda i, j: (i, j, 0))
    keys = pl.BlockSpec((1, lp, LANES), lambda i, j: (i, 0, 0))
    return pl.pallas_call(
        functools.partial(_dsa_attn_kernel, past=past, true_len=true_len, n_sel=n_sel),
        grid=(b, s // tq),
        in_specs=[qblk(nq), qblk(nq), qblk(LANES), keys, keys, keys, _full((LANES, LANES))],
        out_specs=qblk(nq),
        out_shape=jax.ShapeDtypeStruct((b, s, nq), F32),
        compiler_params=_cparams("parallel", "arbitrary"),
        name="dsa_attn",
    )(q, iq, iw, ik2, k, v, su)


def _gmlp_kernel(pd_ref, lg_ref, lb_ref, ws_ref, bs_ref, o_ref, dv_ref):
    span = pd_ref.shape[1]
    h = _gelu(pd_ref[0])
    u = h[:, :D_WIDTH]
    v = h[:, D_WIDTH:]
    mu = jnp.mean(v, axis=-1, keepdims=True)
    d = v - mu
    v = d * lax.rsqrt(jnp.mean(d * d, axis=-1, keepdims=True) + NORM_EPS) * lg_ref[...] + lb_ref[...]
    dv_ref[0] = v
    causal = lax.broadcasted_iota(I32, (span, span), 1) <= lax.broadcasted_iota(I32, (span, span), 0)
    gw = D_WIDTH // D_GROUPS
    mixed = [_dot(jnp.where(causal, ws_ref[g], 0.0).astype(BF16), v[:, g * gw:(g + 1) * gw].astype(BF16))
             for g in range(D_GROUPS)]
    o_ref[0] = u * (jnp.concatenate(mixed, axis=-1) + bs_ref[...])


def _gmlp(pd, w):
    b, s, _ = pd.shape
    span = D_SPAN if s % D_SPAN == 0 else s
    ws = w["d_ws"][:, :span, :span]
    bs = jnp.repeat(w["d_bs"][:, :span].T, D_WIDTH // D_GROUPS, axis=1)
    blk = lambda n: pl.BlockSpec((1, span, n), lambda i, j: (i, j, 0))
    return pl.pallas_call(
        _gmlp_kernel,
        grid=(b, s // span),
        in_specs=[blk(2 * D_WIDTH), _full((1, D_WIDTH)), _full((1, D_WIDTH)), _full((D_GROUPS, span, span)),
                  _full((span, D_WIDTH))],
        out_specs=[blk(D_WIDTH), blk(D_WIDTH)],
        out_shape=[jax.ShapeDtypeStruct((b, s, D_WIDTH), F32)] * 2,
        compiler_params=_cparams("parallel", "parallel"),
        name="gmlp",
    )(pd, w["d_ln_g"], w["d_ln_b"], ws, bs)


def _pad_cols(w, groups, width, slot, off=0):
    k = w.shape[0]
    w = jnp.pad(w.reshape(k, groups, width), ((0, 0), (0, 0), (off, slot - off - width)))
    return w.reshape(k, groups * slot)


def _prep_weights(p):
    row = lambda v: v.reshape(1, -1).astype(F32)
    bf = lambda v: v.astype(BF16)
    w = {}
    w_in0 = p["l0_w_in"]
    w["l0_norm"] = row(p["l0_norm_mix"])
    w["w_cq"] = bf(w_in0[:, :A_D_CQ])
    w["w_ckv"] = bf(w_in0[:, A_D_CQ:A_D_CQ + A_D_C])
    a_cols = A_D_CQ + A_D_C + A_D_ROPE
    w["w_krp"] = bf(_pad_cols(w_in0[:, A_D_CQ + A_D_C:a_cols], 1, A_D_ROPE, LANES, A_D_NOPE))
    w["w_b"] = bf(w_in0[:, a_cols:])
    w["a_kv_norm"] = row(p["a_kv_norm"])
    w["a_q_norm"] = row(p["a_q_norm"])
    w["w_uq"] = bf(_pad_cols(p["a_w_uq"], A_HEADS, A_D_QK, LANES))
    ukv = p["a_w_ukv"].reshape(A_D_C, A_HEADS, A_D_NOPE + A_D_V)
    w["w_uk"] = bf(_pad_cols(ukv[:, :, :A_D_NOPE].reshape(A_D_C, -1), A_HEADS, A_D_NOPE, LANES))
    uv = ukv[:, :, A_D_NOPE:].reshape(A_D_C, A_HEADS // 2, 2, A_D_V)
    slot = lambda x, off: jnp.pad(x, ((0, 0), (0, 0), (off, LANES - off - A_D_V)))
    uv_pad = jnp.stack([slot(uv[:, :, 0], 0), slot(uv[:, :, 1], A_D_V)], axis=2)
    w["w_uv"] = bf(uv_pad.reshape(A_D_C, A_HEADS * LANES))
    w["a_q_gain"] = jnp.pad(row(p["a_q_gain"]), ((0, 0), (0, LANES - A_D_QK)))
    w["a_k_gain"] = jnp.pad(row(p["a_k_gain"]), ((0, 0), (0, LANES - A_D_QK)))
    w["perm_a"] = _rope_perm(LANES, LANES, A_D_NOPE, A_D_ROPE // 2)
    w["b_mu"] = row(p["b_mu"])
    w["b_w0"] = row(p["b_w0"])
    w["b_a0"] = row(p["b_a0"])
    zeros = jnp.zeros((B_W_LORA, B_WIDTH), F32)
    w["b_w2"] = bf(jnp.concatenate([p["b_w2"], zeros], axis=0))
    w["b_a2"] = bf(jnp.concatenate([zeros, p["b_a2"]], axis=0))
    w["b_g2"] = bf(p["b_g2"])
    for n in ("b_kk", "b_ka", "b_rk", "b_ln_g", "b_ln_b"):
        w[n] = row(p[n])
    w["bd512"] = _block_ones(B_WIDTH, B_HD)
    w["bd128"] = _block_ones(LANES, B_HD)
    w["w_out0a"] = bf(p["l0_w_out"][:A_HEADS * A_D_V])
    w["w_out0b"] = bf(p["l0_w_out"][A_HEADS * A_D_V:])
    w_in1 = p["l1_w_in"]
    nq = C_HEADS * C_HD
    nkv = C_KV_HEADS * C_HD
    o = 0
    order = jnp.arange(C_HEADS).reshape(C_KV_HEADS, C_HEADS // C_KV_HEADS).T.reshape(-1)
    w["w_cq1"] = bf(w_in1[:, o:o + nq].reshape(D_MODEL, C_HEADS, C_HD)[:, order].reshape(D_MODEL, nq)); o += nq
    w["w_ck1"] = bf(w_in1[:, o:o + nkv]); o += nkv
    w["w_cv1"] = bf(w_in1[:, o:o + nkv]); o += nkv
    w["w_ciq"] = bf(w_in1[:, o:o + nq]); o += nq
    w["w_cik"] = bf(_pad_cols(w_in1[:, o:o + C_IDX_D], 1, C_IDX_D, LANES)); o += C_IDX_D
    w["w_ciw"] = bf(_pad_cols(w_in1[:, o:o + C_IDX_HEADS], 1, C_IDX_HEADS, LANES)); o += C_IDX_HEADS
    w["w_d"] = bf(w_in1[:, o:])
    w["l1_norm"] = row(p["l1_norm_mix"])
    w["c_q_gain"] = row(jnp.tile(p["c_q_gain"], C_HEADS))
    w["c_k_gain"] = row(jnp.tile(p["c_k_gain"], C_KV_HEADS))
    w["c_kidx_gain"] = jnp.pad(row(p["c_kidx_gain"]), ((0, 0), (0, LANES - C_IDX_D)))
    w["perm_c"] = _rope_perm(nq, C_HD, 0, C_HD // 2)
    w["su"] = (jnp.arange(LANES)[:, None] < jnp.arange(LANES)[None, :]).astype(BF16)
    w["d_ln_g"] = row(p["d_ln_g"])
    w["d_ln_b"] = row(p["d_ln_b"])
    w["d_ws"] = p["d_ws"]
    w["d_bs"] = p["d_bs"]
    w["w_out1a"] = bf(p["l1_w_out"][:nq].reshape(C_HEADS, C_HD, D_MODEL)[order].reshape(nq, D_MODEL))
    w["w_out1b"] = bf(p["l1_w_out"][nq:])
    for l in (0, 1):
        keys = p[f"l{l}_peer_keys"]
        half = PEER_DK // 2
        edge = lambda lo, hi: ((0, 0), (0, 0), (lo, hi))
        kp = jnp.stack([jnp.pad(keys[:, 0], edge(0, half)), jnp.pad(keys[:, 1], edge(half, 0))], axis=1)
        w[f"peer{l}"] = {"keys": bf(kp), "ut": bf(p[f"l{l}_peer_u"]).T, "v": bf(p[f"l{l}_peer_v"]),
                         "norm": row(p[f"l{l}_norm_ffn"]), "wq": bf(p[f"l{l}_peer_wq"])}
    return w


def _pad_keys(x, lp):
    return jnp.pad(x, ((0, 0), (0, lp - x.shape[1])) + ((0, 0),) * (x.ndim - 2))


def _trunk(x, st, w):
    b, s, _ = x.shape
    t = b * s
    past = st["a_ckv"].shape[1]
    true_len = past + s
    lp = -(-true_len // LANES) * LANES
    kpos = jnp.arange(lp, dtype=I32)

    cq, ckv, krp, pb = _in_proj0(x.reshape(t, D_MODEL), w)
    cos_a, sin_a = _rope_tables(kpos, A_D_ROPE // 2, LANES, A_D_NOPE, 1)
    q = _mla_q(cq.reshape(b, s, A_D_CQ), w, cos_a[past:true_len], sin_a[past:true_len])
    ckv3 = ckv.reshape(b, s, A_D_C)
    krp3 = krp.reshape(b, s, LANES)
    krope = krp3[:, :, A_D_NOPE:A_D_NOPE + A_D_ROPE]
    cache_krp = jnp.pad(st["a_krope"], ((0, 0), (0, 0), (A_D_NOPE, LANES - A_D_NOPE - A_D_ROPE)))
    ckv_all = _pad_keys(jnp.concatenate([st["a_ckv"], ckv3], axis=1), lp)
    krp_all = _pad_keys(jnp.concatenate([cache_krp, krp3], axis=1), lp)
    k_a, v_a = _mla_kv(ckv_all, krp_all, w, cos_a, sin_a)
    o_a = _mla_attn(q, k_a, v_a, past, true_len)

    pb3 = pb.reshape(b, s, B_PROJ)
    r, dec, k2, v, an, bv, g, bonus = _rwkv_prep(pb3, st["b_shift"], w)
    pairs = B_HEADS // 2
    s0 = st["b_wkv"].reshape(b, pairs, 2, B_HD, B_HD).transpose(0, 1, 3, 2, 4).reshape(b, pairs, B_HD, LANES)
    y, s_out = _rwkv_scan([r, dec, k2, v, an, bv], s0, w["bd128"])
    wkv = s_out.reshape(b, pairs, B_HD, 2, B_HD).transpose(0, 1, 3, 2, 4).reshape(b, B_HEADS, B_HD, B_HD)
    o_b = _rwkv_post(y.reshape(t, B_WIDTH), bonus.reshape(t, B_WIDTH), g.reshape(t, B_WIDTH), w)
    pw = w["peer0"]
    x1, xn1, pq1 = _out_proj(x.reshape(t, D_MODEL), o_a.reshape(t, -1), o_b, w["w_out0a"], w["w_out0b"],
                             pw["norm"], pw["wq"])
    x2 = _peer(x1, xn1, pq1, pw)

    nq = C_HEADS * C_HD
    cq1, ck1, cv1, ciq, cik, ciw, pd = _in_proj1(x2, w)
    pos_q = kpos[past:true_len]
    cos_c, sin_c = _rope_tables(pos_q, C_HD // 2, C_HD, 0, C_HEADS)
    three = lambda a: a.reshape(b, s, a.shape[-1])
    q_c, k_c, iq_c, ik_c = _dsa_prep(three(cq1), three(ck1), three(ciq), three(cik), w, cos_c, sin_c)
    c_k = k_c.reshape(b, s, C_KV_HEADS, C_HD)
    c_v = cv1.reshape(b, s, C_KV_HEADS, C_HD)
    c_kidx = ik_c[:, :, :C_IDX_D]
    flat_kv = lambda cache, new: _pad_keys(
        jnp.concatenate([cache.reshape(b, past, LANES), new], axis=1), lp).astype(BF16)
    ik_all = jnp.concatenate([st["c_kidx"], c_kidx], axis=1)
    ik2 = _pad_keys(jnp.concatenate([ik_all, ik_all], axis=-1), lp).astype(BF16)
    o_c = _dsa_attn(q_c, iq_c, three(ciw), ik2, flat_kv(st["c_k"], k_c), flat_kv(st["c_v"], three(cv1)),
                    w["su"], past, true_len)
    o_d, d_v = _gmlp(three(pd), w)
    pw = w["peer1"]
    x3, xn3, pq3 = _out_proj(x2, o_c.reshape(t, nq), o_d.reshape(t, D_WIDTH), w["w_out1a"], w["w_out1b"],
                             pw["norm"], pw["wq"])
    y_out = _peer(x3, xn3, pq3, pw).reshape(b, s, D_MODEL)
    return y_out, [ckv3, krope, wkv, pb3[:, -1], c_k, c_v, c_kidx, d_v]


def kernel(x_prompt, x_sample, cache_a_ckv, cache_a_krope, state_b_wkv, state_b_shift, cache_c_k, cache_c_v, cache_c_kidx, l0_norm_mix, l0_w_in, a_q_norm, a_w_uq, a_kv_norm, a_w_ukv, a_q_gain, a_k_gain, b_mu, b_w0, b_w2, b_a0, b_a2, b_g2, b_kk, b_ka, b_rk, b_ln_g, b_ln_b, l0_w_out, l0_norm_ffn, l0_peer_wq, l0_peer_keys, l0_peer_u, l0_peer_v, l1_norm_mix, l1_w_in, c_q_gain, c_k_gain, c_kidx_gain, d_ln_g, d_ln_b, d_ws, d_bs, l1_w_out, l1_norm_ffn, l1_peer_wq, l1_peer_keys, l1_peer_u, l1_peer_v):
    params = dict(
        l0_norm_mix=l0_norm_mix, l0_w_in=l0_w_in, a_q_norm=a_q_norm, a_w_uq=a_w_uq, a_kv_norm=a_kv_norm,
        a_w_ukv=a_w_ukv, a_q_gain=a_q_gain, a_k_gain=a_k_gain, b_mu=b_mu, b_w0=b_w0, b_w2=b_w2, b_a0=b_a0,
        b_a2=b_a2, b_g2=b_g2, b_kk=b_kk, b_ka=b_ka, b_rk=b_rk.reshape(-1), b_ln_g=b_ln_g, b_ln_b=b_ln_b,
        l0_w_out=l0_w_out, l0_norm_ffn=l0_norm_ffn, l0_peer_wq=l0_peer_wq, l0_peer_keys=l0_peer_keys,
        l0_peer_u=l0_peer_u, l0_peer_v=l0_peer_v, l1_norm_mix=l1_norm_mix, l1_w_in=l1_w_in, c_q_gain=c_q_gain,
        c_k_gain=c_k_gain, c_kidx_gain=c_kidx_gain, d_ln_g=d_ln_g, d_ln_b=d_ln_b, d_ws=d_ws, d_bs=d_bs,
        l1_w_out=l1_w_out, l1_norm_ffn=l1_norm_ffn, l1_peer_wq=l1_peer_wq, l1_peer_keys=l1_peer_keys,
        l1_peer_u=l1_peer_u, l1_peer_v=l1_peer_v)
    w = _prep_weights(params)
    bp = x_prompt.shape[0]
    dt = x_prompt.dtype
    prompt_state = dict(
        a_ckv=jnp.zeros((bp, 0, A_D_C), dt), a_krope=jnp.zeros((bp, 0, A_D_ROPE), dt),
        b_wkv=jnp.zeros((bp, B_HEADS, B_HD, B_HD), dt), b_shift=jnp.zeros((bp, B_PROJ), dt),
        c_k=jnp.zeros((bp, 0, C_KV_HEADS, C_HD), dt), c_v=jnp.zeros((bp, 0, C_KV_HEADS, C_HD), dt),
        c_kidx=jnp.zeros((bp, 0, C_IDX_D), dt))
    sample_state = dict(a_ckv=cache_a_ckv, a_krope=cache_a_krope, b_wkv=state_b_wkv, b_shift=state_b_shift,
                        c_k=cache_c_k, c_v=cache_c_v, c_kidx=cache_c_kidx)
    y_p, new_p = _trunk(x_prompt, prompt_state, w)
    y_s, new_s = _trunk(x_sample, sample_state, w)
    return (y_p, y_s, *new_p[:7], *new_s)
```

```python
import functools
import math

import jax
import jax.numpy as jnp
from jax import lax
from jax.experimental import pallas as pl
from jax.experimental.pallas import tpu as pltpu

F32 = jnp.float32
BF16 = jnp.bfloat16
I32 = jnp.int32

D_MODEL = 1024
CHUNK = 64
CHUNK_SHIFT = 6
KEY_BLOCK = 512
ROPE_THETA = 10000.0
NORM_EPS = 1e-6
NEG_INF = -1e30
LANES = 128
SUBLANES = 8

A_HEADS = 8
A_D_NOPE = 64
A_D_ROPE = 32
A_D_QK = A_D_NOPE + A_D_ROPE
A_D_V = 64
A_D_CQ = 256
A_D_C = 128
B_HEADS = 8
B_HD = 64
B_WIDTH = B_HEADS * B_HD
B_W_LORA = 64
B_A_LORA = 64
B_G_LORA = 128
B_PROJ = 3 * B_WIDTH + B_W_LORA + B_A_LORA + B_G_LORA
B_LN_EPS = 64e-5
RWKV_CHUNK = 128
C_HEADS = 8
C_KV_HEADS = 2
C_HD = 64
C_IDX_HEADS = 8
C_IDX_D = 64
C_TOPK = 256
D_GROUPS = 4
D_WIDTH = 512
D_SPAN = 128
PEER_HEADS = 8
PEER_N_KEYS = 128
PEER_N_EXPERTS = PEER_N_KEYS * PEER_N_KEYS
PEER_DK = 128
PEER_TOPK = 16
N_PAIR_CAND = 80

VMEM_LIMIT = 48 * 1024 * 1024
NT = (((1,), (1,)), ((), ()))


def _tile(n, target):
    t = min(n, target)
    while n % t:
        t -= 8
    return t


def _cparams(*sem):
    return pltpu.CompilerParams(dimension_semantics=sem, vmem_limit_bytes=VMEM_LIMIT)


def _dot(a, b):
    return jnp.dot(a, b, preferred_element_type=F32)


def _dot_nt(a, b):
    return lax.dot_general(a, b, NT, preferred_element_type=F32)


def _split_dot(x, w, passes=2):
    hi = x.astype(BF16)
    acc = _dot(hi, w)
    rem = x - hi.astype(F32)
    for _ in range(passes - 1):
        part = rem.astype(BF16)
        acc = acc + _dot(part, w)
        rem = rem - part.astype(F32)
    return acc


def _rms(x, eps=NORM_EPS):
    return x * lax.rsqrt(jnp.mean(x * x, axis=-1, keepdims=True) + eps)


def _full(shape):
    nd = len(shape)
    return pl.BlockSpec(shape, lambda *_: (0,) * nd)


def _block_ones(n, seg):
    i = jnp.arange(n)
    return (i[:, None] // seg == i[None, :] // seg).astype(BF16)


def _rope_perm(n, seg, off, half):
    r = jnp.arange(n)[:, None]
    c = jnp.arange(n)[None, :]
    same = r // seg == c // seg
    rr = r % seg - off
    cc = c % seg - off
    plus = same & (rr >= 0) & (rr < half) & (cc == rr + half)
    minus = same & (cc >= 0) & (cc < half) & (rr == cc + half)
    return (plus.astype(F32) - minus.astype(F32)).astype(BF16)


def _rope_tables(pos, half, seg, off, reps):
    inv_freq = ROPE_THETA ** (-jnp.arange(half, dtype=F32) / half)
    ang = pos.astype(F32)[:, None] * inv_freq[None, :]
    n = pos.shape[0]
    edge = ((0, 0), (off, seg - off - 2 * half))
    cos = jnp.pad(jnp.tile(jnp.cos(ang), (1, 2)), edge, constant_values=1.0)
    sin = jnp.pad(jnp.tile(jnp.sin(ang), (1, 2)), edge)
    return jnp.tile(cos, (1, reps)), jnp.tile(sin, (1, reps))


def _in0_kernel(x_ref, g_ref, wcq_ref, wckv_ref, wkr_ref, wb_ref, kvn_ref,
                cq_ref, ckv_ref, krp_ref, pb_ref):
    xn = (_rms(x_ref[...]) * g_ref[...]).astype(BF16)
    cq_ref[...] = _dot(xn, wcq_ref[...])
    ckv_ref[...] = _rms(_dot(xn, wckv_ref[...])) * kvn_ref[...]
    krp_ref[...] = _dot(xn, wkr_ref[...])
    pb_ref[...] = _dot(xn, wb_ref[...])


def _in_proj0(x2, w):
    t = x2.shape[0]
    tm = _tile(t, 512)
    row = lambda n: pl.BlockSpec((tm, n), lambda i: (i, 0))
    return pl.pallas_call(
        _in0_kernel,
        grid=(t // tm,),
        in_specs=[row(D_MODEL), _full((1, D_MODEL)), _full((D_MODEL, A_D_CQ)), _full((D_MODEL, A_D_C)),
                  _full((D_MODEL, LANES)), _full((D_MODEL, B_PROJ)), _full((1, A_D_C))],
        out_specs=[row(A_D_CQ), row(A_D_C), row(LANES), row(B_PROJ)],
        out_shape=[jax.ShapeDtypeStruct((t, A_D_CQ), F32), jax.ShapeDtypeStruct((t, A_D_C), F32),
                   jax.ShapeDtypeStruct((t, LANES), F32), jax.ShapeDtypeStruct((t, B_PROJ), F32)],
        compiler_params=_cparams("parallel"),
        name="in_proj0",
    )(x2, w["l0_norm"], w["w_cq"], w["w_ckv"], w["w_krp"], w["w_b"], w["a_kv_norm"])


def _head_norm_rope(xh, gain, cos, sin, perm):
    ss = jnp.sum(xh * xh, axis=-1, keepdims=True) * (1.0 / A_D_QK)
    xh = xh * lax.rsqrt(ss + NORM_EPS) * gain
    return xh * cos + _split_dot(xh, perm) * sin


def _mla_q_kernel(cq_ref, qn_ref, wq_ref, gain_ref, cos_ref, sin_ref, perm_ref, o_ref):
    cqn = (_rms(cq_ref[0]) * qn_ref[...]).astype(BF16)
    q = _dot(cqn, wq_ref[...])
    cos, sin, perm, gain = cos_ref[...], sin_ref[...], perm_ref[...], gain_ref[...]
    for h in range(A_HEADS):
        qh = _head_norm_rope(q[:, h * LANES:(h + 1) * LANES], gain, cos, sin, perm)
        o_ref[0, :, h * LANES:(h + 1) * LANES] = (qh * (A_D_QK ** -0.5)).astype(BF16)


def _mla_q(cq, w, cos, sin):
    b, s, _ = cq.shape
    ts = _tile(s, 256)
    return pl.pallas_call(
        _mla_q_kernel,
        grid=(b, s // ts),
        in_specs=[pl.BlockSpec((1, ts, A_D_CQ), lambda i, j: (i, j, 0)), _full((1, A_D_CQ)),
                  _full((A_D_CQ, A_HEADS * LANES)), _full((1, LANES)),
                  pl.BlockSpec((ts, LANES), lambda i, j: (j, 0)), pl.BlockSpec((ts, LANES), lambda i, j: (j, 0)),
                  _full((LANES, LANES))],
        out_specs=pl.BlockSpec((1, ts, A_HEADS * LANES), lambda i, j: (i, j, 0)),
        out_shape=jax.ShapeDtypeStruct((b, s, A_HEADS * LANES), BF16),
        compiler_params=_cparams("parallel", "parallel"),
        name="mla_q",
    )(cq, w["a_q_norm"], w["w_uq"], w["a_q_gain"], cos, sin, w["perm_a"])


def _mla_kv_kernel(ckv_ref, krp_ref, wk_ref, wv_ref, gain_ref, cos_ref, sin_ref, perm_ref, k_ref, v_ref):
    c = ckv_ref[0].astype(BF16)
    k = _dot(c, wk_ref[...])
    v_ref[0] = _dot(c, wv_ref[...]).astype(BF16)
    krp = krp_ref[0]
    cos, sin, perm, gain = cos_ref[...], sin_ref[...], perm_ref[...], gain_ref[...]
    for h in range(A_HEADS):
        kh = _head_norm_rope(k[:, h * LANES:(h + 1) * LANES] + krp, gain, cos, sin, perm)
        k_ref[0, :, h * LANES:(h + 1) * LANES] = kh.astype(BF16)


def _mla_kv(ckv_all, krp_all, w, cos, sin):
    b, lp, _ = ckv_all.shape
    tl = _tile(lp, 256)
    blk = lambda n: pl.BlockSpec((1, tl, n), lambda i, j: (i, j, 0))
    tab = pl.BlockSpec((tl, LANES), lambda i, j: (j, 0))
    return pl.pallas_call(
        _mla_kv_kernel,
        grid=(b, lp // tl),
        in_specs=[blk(A_D_C), blk(LANES), _full((A_D_C, A_HEADS * LANES)), _full((A_D_C, A_HEADS * LANES)),
                  _full((1, LANES)), tab, tab, _full((LANES, LANES))],
        out_specs=[blk(A_HEADS * LANES), blk(A_HEADS * LANES)],
        out_shape=[jax.ShapeDtypeStruct((b, lp, A_HEADS * LANES), BF16)] * 2,
        compiler_params=_cparams("parallel", "parallel"),
        name="mla_kv",
    )(ckv_all, krp_all, w["w_uk"], w["w_uv"], w["a_k_gain"], cos, sin, w["perm_a"])


def _visible(tq, lp, q0, true_len):
    qpos = q0 + lax.broadcasted_iota(I32, (tq, 1), 0)
    kpos = lax.broadcasted_iota(I32, (1, lp), 1)
    kchunk = jnp.where(kpos < true_len, kpos >> CHUNK_SHIFT, jnp.int32(1 << 30))
    return kchunk <= (qpos >> CHUNK_SHIFT)


def _masked_softmax(s, keep):
    s = jnp.where(keep, s, NEG_INF)
    p = jnp.exp(s - jnp.max(s, axis=-1, keepdims=True))
    return p, 1.0 / jnp.sum(p, axis=-1, keepdims=True)


def _for_visible_prefix(run, tq, lp, past):
    q0 = past + pl.program_id(1) * tq
    need = (((q0 + tq - 1) >> CHUNK_SHIFT) + 1) << CHUNK_SHIFT
    sizes = [min(c * KEY_BLOCK, lp) for c in range(1, -(-lp // KEY_BLOCK) + 1)]
    case = (need + (KEY_BLOCK - 1)) // KEY_BLOCK
    for c, n in enumerate(sizes, start=1):
        pl.when(case == c)(functools.partial(run, n, q0))


def _mla_attn_kernel(q_ref, k_ref, v_ref, o_ref, *, past, true_len):
    tq = q_ref.shape[1]

    def run(n, q0):
        vis = _visible(tq, n, q0, true_len)
        for m in range(A_HEADS // 2):
            acc = None
            for h in (2 * m, 2 * m + 1):
                sl = slice(h * LANES, (h + 1) * LANES)
                p, inv = _masked_softmax(_dot_nt(q_ref[0, :, sl], k_ref[0, :n, sl]), vis)
                o = _dot(p.astype(BF16), v_ref[0, :n, sl]) * inv
                acc = o if acc is None else acc + o
            o_ref[0, :, m * LANES:(m + 1) * LANES] = acc

    _for_visible_prefix(run, tq, k_ref.shape[1], past)


def _mla_attn(q, k, v, past, true_len):
    b, s, _ = q.shape
    lp = k.shape[1]
    tq = _tile(s, 128)
    kv = pl.BlockSpec((1, lp, A_HEADS * LANES), lambda i, j: (i, 0, 0))
    return pl.pallas_call(
        functools.partial(_mla_attn_kernel, past=past, true_len=true_len),
        grid=(b, s // tq),
        in_specs=[pl.BlockSpec((1, tq, A_HEADS * LANES), lambda i, j: (i, j, 0)), kv, kv],
        out_specs=pl.BlockSpec((1, tq, A_HEADS * A_D_V), lambda i, j: (i, j, 0)),
        out_shape=jax.ShapeDtypeStruct((b, s, A_HEADS * A_D_V), F32),
        compiler_params=_cparams("parallel", "arbitrary"),
        name="mla_attn",
    )(q, k, v)


def _rwkv_prep_kernel(pb_ref, shift_ref, mu_ref, w0_ref, w2_ref, a0_ref, a2_ref, g2_ref, kk_ref, ka_ref, rk_ref,
                      bd_ref, r_ref, w_ref, k_ref, v_ref, an_ref, bv_ref, g_ref, bonus_ref, last_ref):
    pb = pb_ref[0]
    ts = pb.shape[0]

    @pl.when(pl.program_id(1) == 0)
    def _():
        last_ref[...] = shift_ref[0]

    first = lax.broadcasted_iota(I32, (ts, 1), 0) == 0
    prev = jnp.where(first, last_ref[...], pltpu.roll(pb, 1, axis=0))
    last_ref[...] = pb[ts - 1:ts, :]
    xs = pb + (prev - pb) * mu_ref[...]
    r = xs[:, 0:B_WIDTH]
    k = xs[:, B_WIDTH:2 * B_WIDTH]
    v = xs[:, 2 * B_WIDTH:3 * B_WIDTH]
    wa = xs[:, 3 * B_WIDTH:3 * B_WIDTH + B_W_LORA + B_A_LORA]
    gl = xs[:, 3 * B_WIDTH + B_W_LORA + B_A_LORA:]
    bd = bd_ref[...]
    z = -(w0_ref[...] + _dot(jnp.tanh(wa).astype(BF16), w2_ref[...]))
    softplus = jnp.maximum(z, 0.0) + jnp.log(1.0 + jnp.exp(-jnp.abs(z)))
    w_ref[0] = -jnp.exp(-softplus - 0.5)
    a = jax.nn.sigmoid(a0_ref[...] + _dot(wa.astype(BF16), a2_ref[...]))
    g_ref[0] = _dot(jax.nn.sigmoid(gl).astype(BF16), g2_ref[...])
    kk = k * kk_ref[...]
    kk = kk * lax.rsqrt(_split_dot(kk * kk, bd) + 1e-12)
    k2 = k * (1.0 + (a - 1.0) * ka_ref[...])
    r_ref[0] = r
    k_ref[0] = k2
    v_ref[0] = v
    an_ref[0] = -kk
    bv_ref[0] = kk * a
    bonus_ref[0] = _split_dot(r * k2 * rk_ref[...], bd) * v


def _rwkv_prep(pb3, shift, w):
    b, s, _ = pb3.shape
    ts = _tile(s, 256)
    blk = lambda n: pl.BlockSpec((1, ts, n), lambda i, j: (i, j, 0))
    vec = _full((1, B_WIDTH))
    return pl.pallas_call(
        _rwkv_prep_kernel,
        grid=(b, s // ts),
        in_specs=[blk(B_PROJ), pl.BlockSpec((1, 1, B_PROJ), lambda i, j: (i, 0, 0)), _full((1, B_PROJ)), vec,
                  _full((LANES, B_WIDTH)), vec, _full((LANES, B_WIDTH)), _full((B_G_LORA, B_WIDTH)), vec, vec, vec,
                  _full((B_WIDTH, B_WIDTH))],
        out_specs=[blk(B_WIDTH)] * 8,
        out_shape=[jax.ShapeDtypeStruct((b, s, B_WIDTH), F32)] * 8,
        scratch_shapes=[pltpu.VMEM((1, B_PROJ), F32)],
        compiler_params=_cparams("parallel", "arbitrary"),
        name="rwkv_prep",
    )(pb3, shift[:, None, :], w["b_mu"], w["b_w0"], w["b_w2"], w["b_a0"], w["b_a2"], w["b_g2"], w["b_kk"], w["b_ka"],
      w["b_rk"], w["bd512"])


def _rwkv_scan_kernel(r_ref, lw_ref, k_ref, v_ref, an_ref, bv_ref, s0_ref, y_ref, sout_ref, st_ref):
    nb, c, _ = r_ref.shape
    pairs = B_HEADS // 2

    @pl.when(pl.program_id(1) == 0)
    def _():
        st_ref[...] = s0_ref[...]

    bf = lambda x: x.astype(BF16)
    step = lax.broadcasted_iota(I32, (c, c), 0)
    prev = lax.broadcasted_iota(I32, (c, c), 1)
    lower = prev <= step
    strict = prev < step
    tril = bf(jnp.where(lower, 1.0, 0.0))
    head_of_lane = lax.broadcasted_iota(I32, (1, LANES), 1) >> 6
    same_head = (lax.broadcasted_iota(I32, (LANES, LANES), 0) >> 6) == (lax.broadcasted_iota(I32, (LANES, LANES), 1) >> 6)
    levels = max(1, (c - 1).bit_length())
    pair_ids = [(b, p) for b in range(nb) for p in range(pairs)]
    head_ids = [(q, hh) for q in range(len(pair_ids)) for hh in range(2)]
    load = lambda ref: [ref[b, :, p * LANES:(p + 1) * LANES] for b, p in pair_ids]
    r, lw, k, v, an, bv = (load(ref) for ref in (r_ref, lw_ref, k_ref, v_ref, an_ref, bv_ref))
    pieces = []
    rem = lw
    for _ in range(3):
        pieces.append([bf(x) for x in rem])
        rem = [x - pc.astype(F32) for x, pc in zip(rem, pieces[-1])]
    logp = [_dot(tril, pc) for pc in pieces[0]]
    for level in pieces[1:]:
        logp = [acc + _dot(tril, pc) for acc, pc in zip(logp, level)]
    p_inc = [jnp.exp(x) for x in logp]
    p_inv = [jnp.exp(-x) for x in logp]
    a_t = [a * jnp.exp(lp - l) for a, lp, l in zip(an, logp, lw)]
    b_t = [bf(x * pi) for x, pi in zip(bv, p_inv)]
    k_t = [bf(x * pi) for x, pi in zip(k, p_inv)]
    r_t = [x * pi for x, pi in zip(r, p_inc)]
    s2 = [st_ref[b, p] for b, p in pair_ids]
    s2b = [bf(x) for x in s2]
    a_s = [_dot_nt(bf(x), s) for x, s in zip(a_t, s2b)]
    r_s = [_dot_nt(bf(x), s) for x, s in zip(r_t, s2b)]
    mine = [head_of_lane == hh for _, hh in head_ids]
    only = lambda xs: [bf(jnp.where(m, xs[q], 0.0)) for (q, _), m in zip(head_ids, mine)]
    a_h, r_h, v_h = only(a_t), only(r_t), only(v)
    n_pow = [bf(jnp.where(strict, _dot_nt(a, b_t[q]), 0.0)) for a, (q, _) in zip(a_h, head_ids)]
    l_ak = [bf(jnp.where(strict, _dot_nt(a, k_t[q]), 0.0)) for a, (q, _) in zip(a_h, head_ids)]
    m_rb = [bf(jnp.where(lower, _dot_nt(x, b_t[q]), 0.0)) for x, (q, _) in zip(r_h, head_ids)]
    m_rk = [bf(jnp.where(lower, _dot_nt(x, k_t[q]), 0.0)) for x, (q, _) in zip(r_h, head_ids)]
    x = [jnp.where(m, a_s[q], 0.0) + _dot(l, vh) for (q, _), m, l, vh in zip(head_ids, mine, l_ak, v_h)]
    for lv in range(levels):
        x = [xi + _dot(n, bf(xi)) for xi, n in zip(x, n_pow)]
        if lv + 1 < levels:
            n_pow = [bf(_dot(n, n)) for n in n_pow]
    y_h = [jnp.where(m, r_s[q], 0.0) + _dot(mb, bf(xi)) + _dot(mk, vh)
           for (q, _), m, mb, mk, xi, vh in zip(head_ids, mine, m_rb, m_rk, x, v_h)]
    for q, (b, p) in enumerate(pair_ids):
        y_ref[b, :, p * LANES:(p + 1) * LANES] = y_h[2 * q] + y_h[2 * q + 1]
        uv_t = bf(jnp.concatenate([x[2 * q] + x[2 * q + 1], v[q]], axis=0).T)
        add = _dot(uv_t, jnp.concatenate([b_t[q], k_t[q]], axis=0))
        st_ref[b, p] = (s2[q] + jnp.where(same_head, add, 0.0)) * p_inc[q][c - 1:c, :]

    @pl.when(pl.program_id(1) == pl.num_programs(1) - 1)
    def _():
        sout_ref[...] = st_ref[...]


def _rwkv_scan(seqs, s0):
    b, s, _ = seqs[0].shape
    nb = 1
    c = _tile(s, RWKV_CHUNK)
    pairs = B_HEADS // 2
    seq = pl.BlockSpec((nb, c, B_WIDTH), lambda i, j: (i, j, 0))
    sts = pl.BlockSpec((nb, pairs, LANES, LANES), lambda i, j: (i, 0, 0, 0))
    return pl.pallas_call(
        _rwkv_scan_kernel,
        grid=(b // nb, s // c),
        in_specs=[seq] * 6 + [sts],
        out_specs=[seq, sts],
        out_shape=[jax.ShapeDtypeStruct((b, s, B_WIDTH), F32), jax.ShapeDtypeStruct((b, pairs, LANES, LANES), F32)],
        scratch_shapes=[pltpu.VMEM((nb, pairs, LANES, LANES), F32)],
        compiler_params=_cparams("parallel", "arbitrary"),
        name="rwkv_scan",
    )(*seqs, s0)


def _rwkv_post_kernel(y_ref, bonus_ref, g_ref, lg_ref, lb_ref, bd_ref, o_ref):
    bd = bd_ref[...]
    y = y_ref[...]
    d = y - _split_dot(y, bd) * (1.0 / B_HD)
    var = _split_dot(d * d, bd) * (1.0 / B_HD)
    yn = d * lax.rsqrt(var + B_LN_EPS) * lg_ref[...] + lb_ref[...]
    o_ref[...] = (yn + bonus_ref[...]) * g_ref[...]


def _rwkv_post(y2, bonus2, g2, w):
    t = y2.shape[0]
    tm = _tile(t, 512)
    row = pl.BlockSpec((tm, B_WIDTH), lambda i: (i, 0))
    vec = _full((1, B_WIDTH))
    return pl.pallas_call(
        _rwkv_post_kernel,
        grid=(t // tm,),
        in_specs=[row, row, row, vec, vec, _full((B_WIDTH, B_WIDTH))],
        out_specs=row,
        out_shape=jax.ShapeDtypeStruct((t, B_WIDTH), F32),
        compiler_params=_cparams("parallel"),
        name="rwkv_post",
    )(y2, bonus2, g2, w["b_ln_g"], w["b_ln_b"], w["bd512"])


def _out_proj_kernel(x_ref, o1_ref, o2_ref, w1_ref, w2_ref, gf_ref, wq_ref, xo_ref, xn_ref, q_ref):
    x = x_ref[...] + _dot(o1_ref[...].astype(BF16), w1_ref[...]) + _dot(o2_ref[...].astype(BF16), w2_ref[...])
    xo_ref[...] = x
    xn = (_rms(x) * gf_ref[...]).astype(BF16)
    xn_ref[...] = xn
    q_ref[...] = _dot(xn, wq_ref[...])


def _out_proj(x2, o1, o2, w1, w2, gf, wq):
    t = x2.shape[0]
    tm = _tile(t, 512)
    row = lambda n: pl.BlockSpec((tm, n), lambda i: (i, 0))
    half = o1.shape[1]
    return pl.pallas_call(
        _out_proj_kernel,
        grid=(t // tm,),
        in_specs=[row(D_MODEL), row(half), row(half), _full((half, D_MODEL)), _full((half, D_MODEL)),
                  _full((1, D_MODEL)), _full((D_MODEL, PEER_HEADS * PEER_DK))],
        out_specs=[row(D_MODEL), row(D_MODEL), row(PEER_HEADS * PEER_DK)],
        out_shape=[jax.ShapeDtypeStruct((t, D_MODEL), F32), jax.ShapeDtypeStruct((t, D_MODEL), BF16),
                   jax.ShapeDtypeStruct((t, PEER_HEADS * PEER_DK), F32)],
        compiler_params=_cparams("parallel"),
        name="out_proj",
    )(x2, o1, o2, w1, w2, gf, wq)


def _top16_rows(sc_ref, val_ref, idx_ref):
    nc, n, tb = sc_ref.shape
    rowi = lax.broadcasted_iota(I32, (n, tb), 0).astype(F32)

    def body(r, carry):
        for c in range(nc):
            sc = sc_ref[c]
            m = jnp.max(sc, axis=0, keepdims=True)
            ix = jnp.min(jnp.where(sc == m, rowi, float(n)), axis=0, keepdims=True)
            val_ref[c, pl.ds(r, 1), :] = m
            idx_ref[c, pl.ds(r, 1), :] = ix
            sc_ref[c] = jnp.where(rowi == ix, -jnp.inf, sc)
        return carry

    lax.fori_loop(0, PEER_TOPK, body, 0)


def _pair_candidates(s1, s2, x1, x2):
    sub = lax.broadcasted_iota(I32, (SUBLANES, s1.shape[1]), 0).astype(F32)
    cand, flat, eid = [], [], []
    for a in range(8):
        cand.append(s1[a:a + 1] + s2[0:8])
        flat.append(a * PEER_TOPK + sub)
        eid.append(x1[a:a + 1] * PEER_N_KEYS + x2[0:8])
    cand.append(s1[0:1] + s2[8:16])
    flat.append(8 + sub)
    eid.append(x1[0:1] * PEER_N_KEYS + x2[8:16])
    cand.append(s1[8:16] + s2[0:1])
    flat.append((8 + sub) * PEER_TOPK)
    eid.append(x1[8:16] * PEER_N_KEYS + x2[0:1])
    return jnp.concatenate(cand, axis=0), jnp.concatenate(flat, axis=0), jnp.concatenate(eid, axis=0)


def _peer_route_kernel(q_ref, keys_ref, e1_ref, e2_ref, gate_ref, sc_ref, topv, topi, cand_ref, eid_ref, selv, sele):
    for h in range(PEER_HEADS):
        qh = q_ref[:, h * PEER_DK:(h + 1) * PEER_DK].astype(BF16)
        sc_ref[2 * h] = _dot_nt(keys_ref[h, 0], qh)
        sc_ref[2 * h + 1] = _dot_nt(keys_ref[h, 1], qh)
    _top16_rows(sc_ref, topv, topi)
    for h in range(PEER_HEADS):
        cand, flat, eid = _pair_candidates(topv[2 * h], topv[2 * h + 1], topi[2 * h], topi[2 * h + 1])
        cand_ref[h] = cand
        eid_ref[h] = eid

    def body(r, carry):
        for h in range(PEER_HEADS):
            cand = cand_ref[h]
            m = jnp.max(cand, axis=0, keepdims=True)
            f = jnp.min(jnp.where(cand == m, flat, 1e9), axis=0, keepdims=True)
            hit = flat == f
            selv[h, pl.ds(r, 1), :] = m
            sele[h, pl.ds(r, 1), :] = jnp.max(jnp.where(hit, eid_ref[h], -1.0), axis=0, keepdims=True)
            cand_ref[h] = jnp.where(hit, -jnp.inf, cand)
        return carry

    lax.fori_loop(0, PEER_TOPK, body, 0)
    for h in range(PEER_HEADS):
        rows = slice(h * PEER_TOPK, (h + 1) * PEER_TOPK)
        top = selv[h]
        e = jnp.exp(top - top[0:1])
        gate_ref[rows, :] = e * (1.0 / jnp.sum(e, axis=0, keepdims=True))
        ex = sele[h].astype(I32)
        e1_ref[rows, :] = ex >> 7
        e2_ref[rows, :] = ex & (PEER_N_KEYS - 1)


def _peer_route(q2, keys_pad):
    t = q2.shape[0]
    tb = _tile(t, 128)
    nk = PEER_HEADS * PEER_TOPK
    out = pl.BlockSpec((nk, tb), lambda i: (0, i))
    return pl.pallas_call(
        _peer_route_kernel,
        grid=(t // tb,),
        in_specs=[pl.BlockSpec((tb, PEER_HEADS * PEER_DK), lambda i: (i, 0)),
                  _full((PEER_HEADS, 2, PEER_N_KEYS, PEER_DK))],
        out_specs=[out, out, out],
        out_shape=[jax.ShapeDtypeStruct((nk, t), I32)] * 2 + [jax.ShapeDtypeStruct((nk, t), F32)],
        scratch_shapes=[pltpu.VMEM((2 * PEER_HEADS, PEER_N_KEYS, tb), F32),
                        pltpu.VMEM((2 * PEER_HEADS, PEER_TOPK, tb), F32), pltpu.VMEM((2 * PEER_HEADS, PEER_TOPK, tb), F32),
                        pltpu.VMEM((PEER_HEADS, N_PAIR_CAND, tb), F32), pltpu.VMEM((PEER_HEADS, N_PAIR_CAND, tb), F32),
                        pltpu.VMEM((PEER_HEADS, PEER_TOPK, tb), F32), pltpu.VMEM((PEER_HEADS, PEER_TOPK, tb), F32)],
        compiler_params=_cparams("parallel"),
        name="peer_route",
    )(q2, keys_pad)


def _peer_gate_kernel(e1_ref, e2_ref, gate_ref, o_ref, scr):
    tg = e1_ref.shape[0]
    nk = PEER_HEADS * PEER_TOPK
    sub = lax.broadcasted_iota(I32, (PEER_N_KEYS, nk), 0)

    def body(c, carry):
        t0 = pl.multiple_of(c * SUBLANES, SUBLANES)
        e1 = e1_ref[pl.ds(t0, SUBLANES), :]
        e2 = e2_ref[pl.ds(t0, SUBLANES), :]
        g = gate_ref[pl.ds(t0, SUBLANES), :]
        mats = []
        for j in range(SUBLANES):
            a_t = jnp.where(sub == e1[j:j + 1], g[j:j + 1], 0.0).astype(BF16)
            b_t = jnp.where(sub == e2[j:j + 1], 1.0, 0.0).astype(BF16)
            mats.append(_dot_nt(a_t, b_t))
        scr[:, pl.ds(t0, SUBLANES), :] = jnp.swapaxes(jnp.stack(mats, axis=0), 0, 1)
        return carry

    lax.fori_loop(0, tg // SUBLANES, body, 0, unroll=4)
    o_ref[...] = scr[...].astype(BF16)


def _peer_gates(e1, e2, gate):
    t = e1.shape[0]
    tg = _tile(t, 64)
    row = pl.BlockSpec((tg, PEER_HEADS * PEER_TOPK), lambda i: (i, 0))
    return pl.pallas_call(
        _peer_gate_kernel,
        grid=(t // tg,),
        in_specs=[row, row, row],
        out_specs=pl.BlockSpec((PEER_N_KEYS, tg, PEER_N_KEYS), lambda i: (0, i, 0)),
        out_shape=jax.ShapeDtypeStruct((PEER_N_KEYS, t, PEER_N_KEYS), BF16),
        scratch_shapes=[pltpu.VMEM((PEER_N_KEYS, tg, PEER_N_KEYS), F32)],
        compiler_params=_cparams("parallel"),
        name="peer_gates",
    )(e1, e2, gate)


def _gelu(x):
    return 0.5 * x * (1.0 + lax.erf(x * (2.0 ** -0.5)))


def _peer_dense_kernel(xn_ref, ut_ref, v_ref, g_ref, x_ref, o_ref):
    @pl.when(pl.program_id(1) == 0)
    def _():
        o_ref[...] = x_ref[...]

    h = _gelu(_dot(xn_ref[...], ut_ref[...]))
    nblk = g_ref.shape[0]
    gh = [(h[:, c * LANES:(c + 1) * LANES] * g_ref[c].astype(F32)).astype(BF16) for c in range(nblk)]
    o_ref[...] += _dot(jnp.concatenate(gh, axis=-1), v_ref[...])


def _peer_dense(xn, ut, vtab, gates, x2):
    t = xn.shape[0]
    tb = _tile(t, 1024)
    eb = 1024
    return pl.pallas_call(
        _peer_dense_kernel,
        grid=(t // tb, PEER_N_EXPERTS // eb),
        in_specs=[pl.BlockSpec((tb, D_MODEL), lambda i, j: (i, 0)),
                  pl.BlockSpec((D_MODEL, eb), lambda i, j: (0, j)),
                  pl.BlockSpec((eb, D_MODEL), lambda i, j: (j, 0)),
                  pl.BlockSpec((eb // PEER_N_KEYS, tb, PEER_N_KEYS), lambda i, j: (j, i, 0)),
                  pl.BlockSpec((tb, D_MODEL), lambda i, j: (i, 0))],
        out_specs=pl.BlockSpec((tb, D_MODEL), lambda i, j: (i, 0)),
        out_shape=jax.ShapeDtypeStruct((t, D_MODEL), F32),
        compiler_params=_cparams("parallel", "arbitrary"),
        name="peer_dense",
    )(xn, ut, vtab, gates, x2)


def _peer(x2, xn, q2, pw):
    e1, e2, gate = _peer_route(q2, pw["keys"])
    gates = _peer_gates(e1.T, e2.T, gate.T)
    return _peer_dense(xn, pw["ut"], pw["v"], gates, x2)


def _in1_kernel(x_ref, g_ref, wq_ref, wk_ref, wv_ref, wiq_ref, wik_ref, wiw_ref, wd_ref,
                q_ref, k_ref, v_ref, iq_ref, ik_ref, iw_ref, d_ref):
    xn = (_rms(x_ref[...]) * g_ref[...]).astype(BF16)
    for w_ref, o_ref in ((wq_ref, q_ref), (wk_ref, k_ref), (wv_ref, v_ref), (wiq_ref, iq_ref),
                         (wik_ref, ik_ref), (wiw_ref, iw_ref), (wd_ref, d_ref)):
        o_ref[...] = _dot(xn, w_ref[...])


def _in_proj1(x2, w):
    t = x2.shape[0]
    tm = _tile(t, 512)
    widths = [C_HEADS * C_HD, LANES, LANES, C_IDX_HEADS * C_IDX_D, LANES, LANES, 2 * D_WIDTH]
    row = lambda n: pl.BlockSpec((tm, n), lambda i: (i, 0))
    return pl.pallas_call(
        _in1_kernel,
        grid=(t // tm,),
        in_specs=[row(D_MODEL), _full((1, D_MODEL))] + [_full((D_MODEL, n)) for n in widths],
        out_specs=[row(n) for n in widths],
        out_shape=[jax.ShapeDtypeStruct((t, n), F32) for n in widths],
        compiler_params=_cparams("parallel"),
        name="in_proj1",
    )(x2, w["l1_norm"], w["w_cq1"], w["w_ck1"], w["w_cv1"], w["w_ciq"], w["w_cik"], w["w_ciw"], w["w_d"])


def _dsa_prep_kernel(q_ref, k_ref, iq_ref, ik_ref, qg_ref, kg_ref, ig_ref, cos_ref, sin_ref, bd_ref, perm_ref,
                     qo_ref, ko_ref, iqo_ref, iko_ref):
    cos, sin, bd, perm = cos_ref[...], sin_ref[...], bd_ref[...], perm_ref[...]

    def norm(x, gain, n):
        ss = _split_dot(x * x, bd[:n, :n]) * (1.0 / C_HD)
        return x * lax.rsqrt(ss + NORM_EPS) * gain

    def rope(x, n):
        return x * cos[:, :n] + _split_dot(x, perm[:n, :n]) * sin[:, :n]

    nq = C_HEADS * C_HD
    qo_ref[0] = (rope(norm(q_ref[0], qg_ref[...], nq), nq) * (C_HD ** -0.5)).astype(BF16)
    ko_ref[0] = rope(norm(k_ref[0], kg_ref[...], LANES), LANES)
    iqo_ref[0] = (rope(iq_ref[0], nq) * (C_IDX_D ** -0.5)).astype(BF16)
    iko_ref[0] = rope(norm(ik_ref[0], ig_ref[...], LANES), LANES)


def _dsa_prep(q, k, iq, ik, w, cos, sin):
    b, s, _ = q.shape
    ts = _tile(s, 256)
    nq = C_HEADS * C_HD
    blk = lambda n: pl.BlockSpec((1, ts, n), lambda i, j: (i, j, 0))
    tab = pl.BlockSpec((ts, nq), lambda i, j: (j, 0))
    return pl.pallas_call(
        _dsa_prep_kernel,
        grid=(b, s // ts),
        in_specs=[blk(nq), blk(LANES), blk(nq), blk(LANES), _full((1, nq)), _full((1, LANES)), _full((1, LANES)),
                  tab, tab, _full((nq, nq)), _full((nq, nq))],
        out_specs=[blk(nq), blk(LANES), blk(nq), blk(LANES)],
        out_shape=[jax.ShapeDtypeStruct((b, s, nq), BF16), jax.ShapeDtypeStruct((b, s, LANES), F32),
                   jax.ShapeDtypeStruct((b, s, nq), BF16), jax.ShapeDtypeStruct((b, s, LANES), F32)],
        compiler_params=_cparams("parallel", "parallel"),
        name="dsa_prep",
    )(q, k, iq, ik, w["c_q_gain"], w["c_k_gain"], w["c_kidx_gain"], cos, sin, w["bd512"], w["perm_c"])


KEY_OF_NEG_INF = -(2 ** 31) + 0x7FFFFF


def _key_to_float(key):
    return pltpu.bitcast(jnp.where(key < 0, key ^ jnp.int32(0x7FFFFFFF), key), F32)


def _topk_mask(score, n_sel, su):
    tq, lp = score.shape

    def body(i, tau):
        trial = tau + lax.shift_left(jnp.int32(1), 31 - i)
        cnt = jnp.sum(jnp.where(score >= _key_to_float(trial), 1.0, 0.0), axis=-1, keepdims=True)
        return jnp.where(trial <= KEY_OF_NEG_INF, trial, jnp.where(cnt >= n_sel, trial, tau))

    kth = _key_to_float(lax.fori_loop(0, 32, body, jnp.full((tq, 1), -(2 ** 31), I32)))
    gt = jnp.where(score > kth, 1.0, 0.0)
    eq = jnp.where(score == kth, 1.0, 0.0)
    need = n_sel - jnp.sum(gt, axis=-1, keepdims=True)
    parts = []
    before = jnp.zeros((tq, 1), F32)
    for c in range(lp // LANES):
        eqc = eq[:, c * LANES:(c + 1) * LANES]
        rank = before + _dot(eqc.astype(BF16), su)
        parts.append(gt[:, c * LANES:(c + 1) * LANES] + jnp.where(rank < need, eqc, 0.0))
        before = before + jnp.sum(eqc, axis=-1, keepdims=True)
    return jnp.concatenate(parts, axis=-1)


def _dsa_attn_kernel(q_ref, iq_ref, iw_ref, ik_ref, k_ref, v_ref, su_ref, o_ref, *, past, true_len, n_sel):
    tq = q_ref.shape[1]
    low = lax.broadcasted_iota(I32, (1, LANES), 1) < C_HD
    halves = lambda x: (jnp.where(low, x, jnp.zeros((), x.dtype)), jnp.where(low, jnp.zeros((), x.dtype), x))

    def run(n, q0):
        vis = _visible(tq, n, q0, true_len)
        iw = iw_ref[0] * (C_IDX_HEADS ** -0.5)
        ik2 = ik_ref[0, :n]
        score = jnp.zeros((tq, n), F32)
        for m in range(C_IDX_HEADS // 2):
            iqa, iqb = halves(iq_ref[0, :, m * LANES:(m + 1) * LANES])
            score = score + jnp.maximum(_dot_nt(iqa, ik2), 0.0) * iw[:, 2 * m:2 * m + 1]
            score = score + jnp.maximum(_dot_nt(iqb, ik2), 0.0) * iw[:, 2 * m + 1:2 * m + 2]
        score = jnp.where(vis, score, NEG_INF)
        keep = jnp.where(vis, _topk_mask(score, n_sel, su_ref[...]), 0.0) > 0.5
        k = k_ref[0, :n]
        va, vb = halves(v_ref[0, :n])
        for m in range(C_HEADS // 2):
            qa, qb = halves(q_ref[0, :, m * LANES:(m + 1) * LANES])
            pa, inva = _masked_softmax(_dot_nt(qa, k), keep)
            pb, invb = _masked_softmax(_dot_nt(qb, k), keep)
            o_ref[0, :, m * LANES:(m + 1) * LANES] = (_dot(pa.astype(BF16), va) * inva
                                                      + _dot(pb.astype(BF16), vb) * invb)

    _for_visible_prefix(run, tq, ik_ref.shape[1], past)


def _dsa_attn(q, iq, iw, ik2, k, v, su, past, true_len):
    b, s, nq = q.shape
    lp = ik2.shape[1]
    tq = _tile(s, 128)
    n_sel = min(C_TOPK, true_len // 4)
    qblk = lambda n: pl.BlockSpec((1, tq, n), lambda i, j: (i, j, 0))
    keys = pl.BlockSpec((1, lp, LANES), lambda i, j: (i, 0, 0))
    return pl.pallas_call(
        functools.partial(_dsa_attn_kernel, past=past, true_len=true_len, n_sel=n_sel),
        grid=(b, s // tq),
        in_specs=[qblk(nq), qblk(nq), qblk(LANES), keys, keys, keys, _full((LANES, LANES))],
        out_specs=qblk(nq),
        out_shape=jax.ShapeDtypeStruct((b, s, nq), F32),
        compiler_params=_cparams("parallel", "arbitrary"),
        name="dsa_attn",
    )(q, iq, iw, ik2, k, v, su)


def _gmlp_kernel(pd_ref, lg_ref, lb_ref, ws_ref, bs_ref, o_ref, dv_ref):
    span = pd_ref.shape[1]
    h = _gelu(pd_ref[0])
    u = h[:, :D_WIDTH]
    v = h[:, D_WIDTH:]
    mu = jnp.mean(v, axis=-1, keepdims=True)
    d = v - mu
    v = d * lax.rsqrt(jnp.mean(d * d, axis=-1, keepdims=True) + NORM_EPS) * lg_ref[...] + lb_ref[...]
    dv_ref[0] = v
    causal = lax.broadcasted_iota(I32, (span, span), 1) <= lax.broadcasted_iota(I32, (span, span), 0)
    gw = D_WIDTH // D_GROUPS
    mixed = [_dot(jnp.where(causal, ws_ref[g], 0.0).astype(BF16), v[:, g * gw:(g + 1) * gw].astype(BF16))
             for g in range(D_GROUPS)]
    o_ref[0] = u * (jnp.concatenate(mixed, axis=-1) + bs_ref[...])


def _gmlp(pd, w):
    b, s, _ = pd.shape
    span = D_SPAN if s % D_SPAN == 0 else s
    ws = w["d_ws"][:, :span, :span]
    bs = jnp.repeat(w["d_bs"][:, :span].T, D_WIDTH // D_GROUPS, axis=1)
    blk = lambda n: pl.BlockSpec((1, span, n), lambda i, j: (i, j, 0))
    return pl.pallas_call(
        _gmlp_kernel,
        grid=(b, s // span),
        in_specs=[blk(2 * D_WIDTH), _full((1, D_WIDTH)), _full((1, D_WIDTH)), _full((D_GROUPS, span, span)),
                  _full((span, D_WIDTH))],
        out_specs=[blk(D_WIDTH), blk(D_WIDTH)],
        out_shape=[jax.ShapeDtypeStruct((b, s, D_WIDTH), F32)] * 2,
        compiler_params=_cparams("parallel", "parallel"),
        name="gmlp",
    )(pd, w["d_ln_g"], w["d_ln_b"], ws, bs)


def _pad_cols(w, groups, width, slot, off=0):
    k = w.shape[0]
    w = jnp.pad(w.reshape(k, groups, width), ((0, 0), (0, 0), (off, slot - off - width)))
    return w.reshape(k, groups * slot)


def _prep_weights(p):
    row = lambda v: v.reshape(1, -1).astype(F32)
    bf = lambda v: v.astype(BF16)
    w = {}
    w_in0 = p["l0_w_in"]
    w["l0_norm"] = row(p["l0_norm_mix"])
    w["w_cq"] = bf(w_in0[:, :A_D_CQ])
    w["w_ckv"] = bf(w_in0[:, A_D_CQ:A_D_CQ + A_D_C])
    a_cols = A_D_CQ + A_D_C + A_D_ROPE
    w["w_krp"] = bf(_pad_cols(w_in0[:, A_D_CQ + A_D_C:a_cols], 1, A_D_ROPE, LANES, A_D_NOPE))
    w["w_b"] = bf(w_in0[:, a_cols:])
    w["a_kv_norm"] = row(p["a_kv_norm"])
    w["a_q_norm"] = row(p["a_q_norm"])
    w["w_uq"] = bf(_pad_cols(p["a_w_uq"], A_HEADS, A_D_QK, LANES))
    ukv = p["a_w_ukv"].reshape(A_D_C, A_HEADS, A_D_NOPE + A_D_V)
    w["w_uk"] = bf(_pad_cols(ukv[:, :, :A_D_NOPE].reshape(A_D_C, -1), A_HEADS, A_D_NOPE, LANES))
    uv = ukv[:, :, A_D_NOPE:].reshape(A_D_C, A_HEADS // 2, 2, A_D_V)
    slot = lambda x, off: jnp.pad(x, ((0, 0), (0, 0), (off, LANES - off - A_D_V)))
    uv_pad = jnp.stack([slot(uv[:, :, 0], 0), slot(uv[:, :, 1], A_D_V)], axis=2)
    w["w_uv"] = bf(uv_pad.reshape(A_D_C, A_HEADS * LANES))
    w["a_q_gain"] = jnp.pad(row(p["a_q_gain"]), ((0, 0), (0, LANES - A_D_QK)))
    w["a_k_gain"] = jnp.pad(row(p["a_k_gain"]), ((0, 0), (0, LANES - A_D_QK)))
    w["perm_a"] = _rope_perm(LANES, LANES, A_D_NOPE, A_D_ROPE // 2)
    w["b_mu"] = row(p["b_mu"])
    w["b_w0"] = row(p["b_w0"])
    w["b_a0"] = row(p["b_a0"])
    zeros = jnp.zeros((B_W_LORA, B_WIDTH), F32)
    w["b_w2"] = bf(jnp.concatenate([p["b_w2"], zeros], axis=0))
    w["b_a2"] = bf(jnp.concatenate([zeros, p["b_a2"]], axis=0))
    w["b_g2"] = bf(p["b_g2"])
    for n in ("b_kk", "b_ka", "b_rk", "b_ln_g", "b_ln_b"):
        w[n] = row(p[n])
    w["bd512"] = _block_ones(B_WIDTH, B_HD)
    w["w_out0a"] = bf(p["l0_w_out"][:A_HEADS * A_D_V])
    w["w_out0b"] = bf(p["l0_w_out"][A_HEADS * A_D_V:])
    w_in1 = p["l1_w_in"]
    nq = C_HEADS * C_HD
    nkv = C_KV_HEADS * C_HD
    o = 0
    order = jnp.arange(C_HEADS).reshape(C_KV_HEADS, C_HEADS // C_KV_HEADS).T.reshape(-1)
    w["w_cq1"] = bf(w_in1[:, o:o + nq].reshape(D_MODEL, C_HEADS, C_HD)[:, order].reshape(D_MODEL, nq)); o += nq
    w["w_ck1"] = bf(w_in1[:, o:o + nkv]); o += nkv
    w["w_cv1"] = bf(w_in1[:, o:o + nkv]); o += nkv
    w["w_ciq"] = bf(w_in1[:, o:o + nq]); o += nq
    w["w_cik"] = bf(_pad_cols(w_in1[:, o:o + C_IDX_D], 1, C_IDX_D, LANES)); o += C_IDX_D
    w["w_ciw"] = bf(_pad_cols(w_in1[:, o:o + C_IDX_HEADS], 1, C_IDX_HEADS, LANES)); o += C_IDX_HEADS
    w["w_d"] = bf(w_in1[:, o:])
    w["l1_norm"] = row(p["l1_norm_mix"])
    w["c_q_gain"] = row(jnp.tile(p["c_q_gain"], C_HEADS))
    w["c_k_gain"] = row(jnp.tile(p["c_k_gain"], C_KV_HEADS))
    w["c_kidx_gain"] = jnp.pad(row(p["c_kidx_gain"]), ((0, 0), (0, LANES - C_IDX_D)))
    w["perm_c"] = _rope_perm(nq, C_HD, 0, C_HD // 2)
    w["su"] = (jnp.arange(LANES)[:, None] < jnp.arange(LANES)[None, :]).astype(BF16)
    w["d_ln_g"] = row(p["d_ln_g"])
    w["d_ln_b"] = row(p["d_ln_b"])
    w["d_ws"] = p["d_ws"]
    w["d_bs"] = p["d_bs"]
    w["w_out1a"] = bf(p["l1_w_out"][:nq].reshape(C_HEADS, C_HD, D_MODEL)[order].reshape(nq, D_MODEL))
    w["w_out1b"] = bf(p["l1_w_out"][nq:])
    for l in (0, 1):
        keys = p[f"l{l}_peer_keys"]
        half = PEER_DK // 2
        edge = lambda lo, hi: ((0, 0), (0, 0), (lo, hi))
        kp = jnp.stack([jnp.pad(keys[:, 0], edge(0, half)), jnp.pad(keys[:, 1], edge(half, 0))], axis=1)
        w[f"peer{l}"] = {"keys": bf(kp), "ut": bf(p[f"l{l}_peer_u"]).T, "v": bf(p[f"l{l}_peer_v"]),
                         "norm": row(p[f"l{l}_norm_ffn"]), "wq": bf(p[f"l{l}_peer_wq"])}
    return w


def _pad_keys(x, lp):
    return jnp.pad(x, ((0, 0), (0, lp - x.shape[1])) + ((0, 0),) * (x.ndim - 2))


def _trunk(x, st, w):
    b, s, _ = x.shape
    t = b * s
    past = st["a_ckv"].shape[1]
    true_len = past + s
    lp = -(-true_len // LANES) * LANES
    kpos = jnp.arange(lp, dtype=I32)

    cq, ckv, krp, pb = _in_proj0(x.reshape(t, D_MODEL), w)
    cos_a, sin_a = _rope_tables(kpos, A_D_ROPE // 2, LANES, A_D_NOPE, 1)
    q = _mla_q(cq.reshape(b, s, A_D_CQ), w, cos_a[past:true_len], sin_a[past:true_len])
    ckv3 = ckv.reshape(b, s, A_D_C)
    krp3 = krp.reshape(b, s, LANES)
    krope = krp3[:, :, A_D_NOPE:A_D_NOPE + A_D_ROPE]
    cache_krp = jnp.pad(st["a_krope"], ((0, 0), (0, 0), (A_D_NOPE, LANES - A_D_NOPE - A_D_ROPE)))
    ckv_all = _pad_keys(jnp.concatenate([st["a_ckv"], ckv3], axis=1), lp)
    krp_all = _pad_keys(jnp.concatenate([cache_krp, krp3], axis=1), lp)
    k_a, v_a = _mla_kv(ckv_all, krp_all, w, cos_a, sin_a)
    o_a = _mla_attn(q, k_a, v_a, past, true_len)

    pb3 = pb.reshape(b, s, B_PROJ)
    r, logw, k2, v, an, bv, g, bonus = _rwkv_prep(pb3, st["b_shift"], w)
    pairs = B_HEADS // 2
    sp = st["b_wkv"].reshape(b, pairs, 2, B_HD, B_HD)
    edge = lambda lo, hi: ((0, 0), (0, 0), (0, 0), (lo, hi))
    s0 = jnp.concatenate([jnp.pad(sp[:, :, 0], edge(0, B_HD)), jnp.pad(sp[:, :, 1], edge(B_HD, 0))], axis=2)
    y, s_out = _rwkv_scan([r, logw, k2, v, an, bv], s0)
    wkv = jnp.stack([s_out[:, :, :B_HD, :B_HD], s_out[:, :, B_HD:, B_HD:]], axis=2).reshape(b, B_HEADS, B_HD, B_HD)
    o_b = _rwkv_post(y.reshape(t, B_WIDTH), bonus.reshape(t, B_WIDTH), g.reshape(t, B_WIDTH), w)
    pw = w["peer0"]
    x1, xn1, pq1 = _out_proj(x.reshape(t, D_MODEL), o_a.reshape(t, -1), o_b, w["w_out0a"], w["w_out0b"],
                             pw["norm"], pw["wq"])
    x2 = _peer(x1, xn1, pq1, pw)

    nq = C_HEADS * C_HD
    cq1, ck1, cv1, ciq, cik, ciw, pd = _in_proj1(x2, w)
    pos_q = kpos[past:true_len]
    cos_c, sin_c = _rope_tables(pos_q, C_HD // 2, C_HD, 0, C_HEADS)
    three = lambda a: a.reshape(b, s, a.shape[-1])
    q_c, k_c, iq_c, ik_c = _dsa_prep(three(cq1), three(ck1), three(ciq), three(cik), w, cos_c, sin_c)
    c_k = k_c.reshape(b, s, C_KV_HEADS, C_HD)
    c_v = cv1.reshape(b, s, C_KV_HEADS, C_HD)
    c_kidx = ik_c[:, :, :C_IDX_D]
    flat_kv = lambda cache, new: _pad_keys(
        jnp.concatenate([cache.reshape(b, past, LANES), new], axis=1), lp).astype(BF16)
    ik_all = jnp.concatenate([st["c_kidx"], c_kidx], axis=1)
    ik2 = _pad_keys(jnp.concatenate([ik_all, ik_all], axis=-1), lp).astype(BF16)
    o_c = _dsa_attn(q_c, iq_c, three(ciw), ik2, flat_kv(st["c_k"], k_c), flat_kv(st["c_v"], three(cv1)),
                    w["su"], past, true_len)
    o_d, d_v = _gmlp(three(pd), w)
    pw = w["peer1"]
    x3, xn3, pq3 = _out_proj(x2, o_c.reshape(t, nq), o_d.reshape(t, D_WIDTH), w["w_out1a"], w["w_out1b"],
                             pw["norm"], pw["wq"])
    y_out = _peer(x3, xn3, pq3, pw).reshape(b, s, D_MODEL)
    return y_out, [ckv3, krope, wkv, pb3[:, -1], c_k, c_v, c_kidx, d_v]


def kernel(x_prompt, x_sample, cache_a_ckv, cache_a_krope, state_b_wkv, state_b_shift, cache_c_k, cache_c_v, cache_c_kidx, l0_norm_mix, l0_w_in, a_q_norm, a_w_uq, a_kv_norm, a_w_ukv, a_q_gain, a_k_gain, b_mu, b_w0, b_w2, b_a0, b_a2, b_g2, b_kk, b_ka, b_rk, b_ln_g, b_ln_b, l0_w_out, l0_norm_ffn, l0_peer_wq, l0_peer_keys, l0_peer_u, l0_peer_v, l1_norm_mix, l1_w_in, c_q_gain, c_k_gain, c_kidx_gain, d_ln_g, d_ln_b, d_ws, d_bs, l1_w_out, l1_norm_ffn, l1_peer_wq, l1_peer_keys, l1_peer_u, l1_peer_v):
    params = dict(
        l0_norm_mix=l0_norm_mix, l0_w_in=l0_w_in, a_q_norm=a_q_norm, a_w_uq=a_w_uq, a_kv_norm=a_kv_norm,
        a_w_ukv=a_w_ukv, a_q_gain=a_q_gain, a_k_gain=a_k_gain, b_mu=b_mu, b_w0=b_w0, b_w2=b_w2, b_a0=b_a0,
        b_a2=b_a2, b_g2=b_g2, b_kk=b_kk, b_ka=b_ka, b_rk=b_rk.reshape(-1), b_ln_g=b_ln_g, b_ln_b=b_ln_b,
        l0_w_out=l0_w_out, l0_norm_ffn=l0_norm_ffn, l0_peer_wq=l0_peer_wq, l0_peer_keys=l0_peer_keys,
        l0_peer_u=l0_peer_u, l0_peer_v=l0_peer_v, l1_norm_mix=l1_norm_mix, l1_w_in=l1_w_in, c_q_gain=c_q_gain,
        c_k_gain=c_k_gain, c_kidx_gain=c_kidx_gain, d_ln_g=d_ln_g, d_ln_b=d_ln_b, d_ws=d_ws, d_bs=d_bs,
        l1_w_out=l1_w_out, l1_norm_ffn=l1_norm_ffn, l1_peer_wq=l1_peer_wq, l1_peer_keys=l1_peer_keys,
        l1_peer_u=l1_peer_u, l1_peer_v=l1_peer_v)
    w = _prep_weights(params)
    bp = x_prompt.shape[0]
    dt = x_prompt.dtype
    prompt_state = dict(
        a_ckv=jnp.zeros((bp, 0, A_D_C), dt), a_krope=jnp.zeros((bp, 0, A_D_ROPE), dt),
        b_wkv=jnp.zeros((bp, B_HEADS, B_HD, B_HD), dt), b_shift=jnp.zeros((bp, B_PROJ), dt),
        c_k=jnp.zeros((bp, 0, C_KV_HEADS, C_HD), dt), c_v=jnp.zeros((bp, 0, C_KV_HEADS, C_HD), dt),
        c_kidx=jnp.zeros((bp, 0, C_IDX_D), dt))
    sample_state = dict(a_ckv=cache_a_ckv, a_krope=cache_a_krope, b_wkv=state_b_wkv, b_shift=state_b_shift,
                        c_k=cache_c_k, c_v=cache_c_v, c_kidx=cache_c_kidx)
    y_p, new_p = _trunk(x_prompt, prompt_state, w)
    y_s, new_s = _trunk(x_sample, sample_state, w)
    return (y_p, y_s, *new_p[:7], *new_s)
```

```python
import functools
import math

import jax
import jax.numpy as jnp
from jax import lax
from jax.experimental import pallas as pl
from jax.experimental.pallas import tpu as pltpu

F32 = jnp.float32
BF16 = jnp.bfloat16
I32 = jnp.int32

D_MODEL = 1024
CHUNK = 64
CHUNK_SHIFT = 6
KEY_BLOCK = 512
ROPE_THETA = 10000.0
NORM_EPS = 1e-6
NEG_INF = -1e30
LANES = 128
SUBLANES = 8

A_HEADS = 8
A_D_NOPE = 64
A_D_ROPE = 32
A_D_QK = A_D_NOPE + A_D_ROPE
A_D_V = 64
A_D_CQ = 256
A_D_C = 128
B_HEADS = 8
B_HD = 64
B_WIDTH = B_HEADS * B_HD
B_W_LORA = 64
B_A_LORA = 64
B_G_LORA = 128
B_PROJ = 3 * B_WIDTH + B_W_LORA + B_A_LORA + B_G_LORA
B_LN_EPS = 64e-5
RWKV_CHUNK = 128
C_HEADS = 8
C_KV_HEADS = 2
C_HD = 64
C_IDX_HEADS = 8
C_IDX_D = 64
C_TOPK = 256
D_GROUPS = 4
D_WIDTH = 512
D_SPAN = 128
PEER_HEADS = 8
PEER_N_KEYS = 128
PEER_N_EXPERTS = PEER_N_KEYS * PEER_N_KEYS
PEER_DK = 128
PEER_TOPK = 16
N_PAIR_CAND = 80

VMEM_LIMIT = 48 * 1024 * 1024
NT = (((1,), (1,)), ((), ()))


def _tile(n, target):
    t = min(n, target)
    while n % t:
        t -= 8
    return t


def _cparams(*sem):
    return pltpu.CompilerParams(dimension_semantics=sem, vmem_limit_bytes=VMEM_LIMIT)


def _dot(a, b):
    return jnp.dot(a, b, preferred_element_type=F32)


def _dot_nt(a, b):
    return lax.dot_general(a, b, NT, preferred_element_type=F32)


def _split_dot(x, w, passes=2):
    hi = x.astype(BF16)
    acc = _dot(hi, w)
    rem = x - hi.astype(F32)
    for _ in range(passes - 1):
        part = rem.astype(BF16)
        acc = acc + _dot(part, w)
        rem = rem - part.astype(F32)
    return acc


def _rms(x, eps=NORM_EPS):
    return x * lax.rsqrt(jnp.mean(x * x, axis=-1, keepdims=True) + eps)


def _full(shape):
    nd = len(shape)
    return pl.BlockSpec(shape, lambda *_: (0,) * nd)


def _block_ones(n, seg):
    i = jnp.arange(n)
    return (i[:, None] // seg == i[None, :] // seg).astype(BF16)


def _rope_perm(n, seg, off, half):
    r = jnp.arange(n)[:, None]
    c = jnp.arange(n)[None, :]
    same = r // seg == c // seg
    rr = r % seg - off
    cc = c % seg - off
    plus = same & (rr >= 0) & (rr < half) & (cc == rr + half)
    minus = same & (cc >= 0) & (cc < half) & (rr == cc + half)
    return (plus.astype(F32) - minus.astype(F32)).astype(BF16)


def _rope_tables(pos, half, seg, off, reps):
    inv_freq = ROPE_THETA ** (-jnp.arange(half, dtype=F32) / half)
    ang = pos.astype(F32)[:, None] * inv_freq[None, :]
    n = pos.shape[0]
    edge = ((0, 0), (off, seg - off - 2 * half))
    cos = jnp.pad(jnp.tile(jnp.cos(ang), (1, 2)), edge, constant_values=1.0)
    sin = jnp.pad(jnp.tile(jnp.sin(ang), (1, 2)), edge)
    return jnp.tile(cos, (1, reps)), jnp.tile(sin, (1, reps))


def _in0_kernel(x_ref, g_ref, wcq_ref, wckv_ref, wkr_ref, wb_ref, kvn_ref,
                cq_ref, ckv_ref, krp_ref, pb_ref):
    xn = (_rms(x_ref[...]) * g_ref[...]).astype(BF16)
    cq_ref[...] = _dot(xn, wcq_ref[...])
    ckv_ref[...] = _rms(_dot(xn, wckv_ref[...])) * kvn_ref[...]
    krp_ref[...] = _dot(xn, wkr_ref[...])
    pb_ref[...] = _dot(xn, wb_ref[...])


def _in_proj0(x2, w):
    t = x2.shape[0]
    tm = _tile(t, 512)
    row = lambda n: pl.BlockSpec((tm, n), lambda i: (i, 0))
    return pl.pallas_call(
        _in0_kernel,
        grid=(t // tm,),
        in_specs=[row(D_MODEL), _full((1, D_MODEL)), _full((D_MODEL, A_D_CQ)), _full((D_MODEL, A_D_C)),
                  _full((D_MODEL, LANES)), _full((D_MODEL, B_PROJ)), _full((1, A_D_C))],
        out_specs=[row(A_D_CQ), row(A_D_C), row(LANES), row(B_PROJ)],
        out_shape=[jax.ShapeDtypeStruct((t, A_D_CQ), F32), jax.ShapeDtypeStruct((t, A_D_C), F32),
                   jax.ShapeDtypeStruct((t, LANES), F32), jax.ShapeDtypeStruct((t, B_PROJ), F32)],
        compiler_params=_cparams("parallel"),
        name="in_proj0",
    )(x2, w["l0_norm"], w["w_cq"], w["w_ckv"], w["w_krp"], w["w_b"], w["a_kv_norm"])


def _head_norm_rope(xh, gain, cos, sin, perm):
    ss = jnp.sum(xh * xh, axis=-1, keepdims=True) * (1.0 / A_D_QK)
    xh = xh * lax.rsqrt(ss + NORM_EPS) * gain
    return xh * cos + _split_dot(xh, perm) * sin


def _mla_q_kernel(cq_ref, qn_ref, wq_ref, gain_ref, cos_ref, sin_ref, perm_ref, o_ref):
    cqn = (_rms(cq_ref[0]) * qn_ref[...]).astype(BF16)
    q = _dot(cqn, wq_ref[...])
    cos, sin, perm, gain = cos_ref[...], sin_ref[...], perm_ref[...], gain_ref[...]
    for h in range(A_HEADS):
        qh = _head_norm_rope(q[:, h * LANES:(h + 1) * LANES], gain, cos, sin, perm)
        o_ref[0, :, h * LANES:(h + 1) * LANES] = (qh * (A_D_QK ** -0.5)).astype(BF16)


def _mla_q(cq, w, cos, sin):
    b, s, _ = cq.shape
    ts = _tile(s, 256)
    return pl.pallas_call(
        _mla_q_kernel,
        grid=(b, s // ts),
        in_specs=[pl.BlockSpec((1, ts, A_D_CQ), lambda i, j: (i, j, 0)), _full((1, A_D_CQ)),
                  _full((A_D_CQ, A_HEADS * LANES)), _full((1, LANES)),
                  pl.BlockSpec((ts, LANES), lambda i, j: (j, 0)), pl.BlockSpec((ts, LANES), lambda i, j: (j, 0)),
                  _full((LANES, LANES))],
        out_specs=pl.BlockSpec((1, ts, A_HEADS * LANES), lambda i, j: (i, j, 0)),
        out_shape=jax.ShapeDtypeStruct((b, s, A_HEADS * LANES), BF16),
        compiler_params=_cparams("parallel", "parallel"),
        name="mla_q",
    )(cq, w["a_q_norm"], w["w_uq"], w["a_q_gain"], cos, sin, w["perm_a"])


def _mla_kv_kernel(ckv_ref, krp_ref, wk_ref, wv_ref, gain_ref, cos_ref, sin_ref, perm_ref, k_ref, v_ref):
    c = ckv_ref[0].astype(BF16)
    k = _dot(c, wk_ref[...])
    v_ref[0] = _dot(c, wv_ref[...]).astype(BF16)
    krp = krp_ref[0]
    cos, sin, perm, gain = cos_ref[...], sin_ref[...], perm_ref[...], gain_ref[...]
    for h in range(A_HEADS):
        kh = _head_norm_rope(k[:, h * LANES:(h + 1) * LANES] + krp, gain, cos, sin, perm)
        k_ref[0, :, h * LANES:(h + 1) * LANES] = kh.astype(BF16)


def _mla_kv(ckv_all, krp_all, w, cos, sin):
    b, lp, _ = ckv_all.shape
    tl = _tile(lp, 256)
    blk = lambda n: pl.BlockSpec((1, tl, n), lambda i, j: (i, j, 0))
    tab = pl.BlockSpec((tl, LANES), lambda i, j: (j, 0))
    return pl.pallas_call(
        _mla_kv_kernel,
        grid=(b, lp // tl),
        in_specs=[blk(A_D_C), blk(LANES), _full((A_D_C, A_HEADS * LANES)), _full((A_D_C, A_HEADS * LANES)),
                  _full((1, LANES)), tab, tab, _full((LANES, LANES))],
        out_specs=[blk(A_HEADS * LANES), blk(A_HEADS * LANES)],
        out_shape=[jax.ShapeDtypeStruct((b, lp, A_HEADS * LANES), BF16)] * 2,
        compiler_params=_cparams("parallel", "parallel"),
        name="mla_kv",
    )(ckv_all, krp_all, w["w_uk"], w["w_uv"], w["a_k_gain"], cos, sin, w["perm_a"])


def _visible(tq, lp, q0, true_len):
    qpos = q0 + lax.broadcasted_iota(I32, (tq, 1), 0)
    kpos = lax.broadcasted_iota(I32, (1, lp), 1)
    kchunk = jnp.where(kpos < true_len, kpos >> CHUNK_SHIFT, jnp.int32(1 << 30))
    return kchunk <= (qpos >> CHUNK_SHIFT)


def _masked_softmax(s, keep):
    s = jnp.where(keep, s, NEG_INF)
    p = jnp.exp(s - jnp.max(s, axis=-1, keepdims=True))
    return p, 1.0 / jnp.sum(p, axis=-1, keepdims=True)


def _for_visible_prefix(run, tq, lp, past):
    q0 = past + pl.program_id(1) * tq
    need = (((q0 + tq - 1) >> CHUNK_SHIFT) + 1) << CHUNK_SHIFT
    sizes = [min(c * KEY_BLOCK, lp) for c in range(1, -(-lp // KEY_BLOCK) + 1)]
    case = (need + (KEY_BLOCK - 1)) // KEY_BLOCK
    for c, n in enumerate(sizes, start=1):
        pl.when(case == c)(functools.partial(run, n, q0))


def _mla_attn_kernel(q_ref, k_ref, v_ref, o_ref, *, past, true_len):
    tq = q_ref.shape[1]

    def run(n, q0):
        vis = _visible(tq, n, q0, true_len)
        for m in range(A_HEADS // 2):
            acc = None
            for h in (2 * m, 2 * m + 1):
                sl = slice(h * LANES, (h + 1) * LANES)
                p, inv = _masked_softmax(_dot_nt(q_ref[0, :, sl], k_ref[0, :n, sl]), vis)
                o = _dot(p.astype(BF16), v_ref[0, :n, sl]) * inv
                acc = o if acc is None else acc + o
            o_ref[0, :, m * LANES:(m + 1) * LANES] = acc

    _for_visible_prefix(run, tq, k_ref.shape[1], past)


def _mla_attn(q, k, v, past, true_len):
    b, s, _ = q.shape
    lp = k.shape[1]
    tq = _tile(s, 256)
    kv = pl.BlockSpec((1, lp, A_HEADS * LANES), lambda i, j: (i, 0, 0))
    return pl.pallas_call(
        functools.partial(_mla_attn_kernel, past=past, true_len=true_len),
        grid=(b, s // tq),
        in_specs=[pl.BlockSpec((1, tq, A_HEADS * LANES), lambda i, j: (i, j, 0)), kv, kv],
        out_specs=pl.BlockSpec((1, tq, A_HEADS * A_D_V), lambda i, j: (i, j, 0)),
        out_shape=jax.ShapeDtypeStruct((b, s, A_HEADS * A_D_V), F32),
        compiler_params=_cparams("parallel", "arbitrary"),
        name="mla_attn",
    )(q, k, v)


def _rwkv_prep_kernel(pb_ref, shift_ref, mu_ref, w0_ref, w2_ref, a0_ref, a2_ref, g2_ref, kk_ref, ka_ref, rk_ref,
                      bd_ref, r_ref, w_ref, k_ref, v_ref, an_ref, bv_ref, g_ref, bonus_ref, last_ref):
    pb = pb_ref[0]
    ts = pb.shape[0]

    @pl.when(pl.program_id(1) == 0)
    def _():
        last_ref[...] = shift_ref[0]

    first = lax.broadcasted_iota(I32, (ts, 1), 0) == 0
    prev = jnp.where(first, last_ref[...], pltpu.roll(pb, 1, axis=0))
    last_ref[...] = pb[ts - 1:ts, :]
    xs = pb + (prev - pb) * mu_ref[...]
    r = xs[:, 0:B_WIDTH]
    k = xs[:, B_WIDTH:2 * B_WIDTH]
    v = xs[:, 2 * B_WIDTH:3 * B_WIDTH]
    wa = xs[:, 3 * B_WIDTH:3 * B_WIDTH + B_W_LORA + B_A_LORA]
    gl = xs[:, 3 * B_WIDTH + B_W_LORA + B_A_LORA:]
    bd = bd_ref[...]
    z = -(w0_ref[...] + _dot(jnp.tanh(wa).astype(BF16), w2_ref[...]))
    softplus = jnp.maximum(z, 0.0) + jnp.log(1.0 + jnp.exp(-jnp.abs(z)))
    w_ref[0] = -jnp.exp(-softplus - 0.5)
    a = jax.nn.sigmoid(a0_ref[...] + _dot(wa.astype(BF16), a2_ref[...]))
    g_ref[0] = _dot(jax.nn.sigmoid(gl).astype(BF16), g2_ref[...])
    kk = k * kk_ref[...]
    kk = kk * lax.rsqrt(_split_dot(kk * kk, bd) + 1e-12)
    k2 = k * (1.0 + (a - 1.0) * ka_ref[...])
    r_ref[0] = r
    k_ref[0] = k2
    v_ref[0] = v
    an_ref[0] = -kk
    bv_ref[0] = kk * a
    bonus_ref[0] = _split_dot(r * k2 * rk_ref[...], bd) * v


def _rwkv_prep(pb3, shift, w):
    b, s, _ = pb3.shape
    ts = _tile(s, 256)
    blk = lambda n: pl.BlockSpec((1, ts, n), lambda i, j: (i, j, 0))
    vec = _full((1, B_WIDTH))
    return pl.pallas_call(
        _rwkv_prep_kernel,
        grid=(b, s // ts),
        in_specs=[blk(B_PROJ), pl.BlockSpec((1, 1, B_PROJ), lambda i, j: (i, 0, 0)), _full((1, B_PROJ)), vec,
                  _full((LANES, B_WIDTH)), vec, _full((LANES, B_WIDTH)), _full((B_G_LORA, B_WIDTH)), vec, vec, vec,
                  _full((B_WIDTH, B_WIDTH))],
        out_specs=[blk(B_WIDTH)] * 8,
        out_shape=[jax.ShapeDtypeStruct((b, s, B_WIDTH), F32)] * 8,
        scratch_shapes=[pltpu.VMEM((1, B_PROJ), F32)],
        compiler_params=_cparams("parallel", "arbitrary"),
        name="rwkv_prep",
    )(pb3, shift[:, None, :], w["b_mu"], w["b_w0"], w["b_w2"], w["b_a0"], w["b_a2"], w["b_g2"], w["b_kk"], w["b_ka"],
      w["b_rk"], w["bd512"])


def _rwkv_scan_kernel(r_ref, lw_ref, k_ref, v_ref, an_ref, bv_ref, s0_ref, y_ref, sout_ref, st_ref):
    nb, c, _ = r_ref.shape
    pairs = B_HEADS // 2

    @pl.when(pl.program_id(1) == 0)
    def _():
        st_ref[...] = s0_ref[...]

    bf = lambda x: x.astype(BF16)
    step = lax.broadcasted_iota(I32, (c, c), 0)
    prev = lax.broadcasted_iota(I32, (c, c), 1)
    lower = prev <= step
    strict = prev < step
    tril = bf(jnp.where(lower, 1.0, 0.0))
    head_of_lane = lax.broadcasted_iota(I32, (1, LANES), 1) >> 6
    same_head = (lax.broadcasted_iota(I32, (LANES, LANES), 0) >> 6) == (lax.broadcasted_iota(I32, (LANES, LANES), 1) >> 6)
    levels = max(1, (c - 1).bit_length())
    pair_ids = [(b, p) for b in range(nb) for p in range(pairs)]
    head_ids = [(q, hh) for q in range(len(pair_ids)) for hh in range(2)]
    load = lambda ref: [ref[b, :, p * LANES:(p + 1) * LANES] for b, p in pair_ids]
    r, lw, k, v, an, bv = (load(ref) for ref in (r_ref, lw_ref, k_ref, v_ref, an_ref, bv_ref))
    pieces = []
    rem = lw
    for _ in range(3):
        pieces.append([bf(x) for x in rem])
        rem = [x - pc.astype(F32) for x, pc in zip(rem, pieces[-1])]
    logp = [_dot(tril, pc) for pc in pieces[0]]
    for level in pieces[1:]:
        logp = [acc + _dot(tril, pc) for acc, pc in zip(logp, level)]
    p_inc = [jnp.exp(x) for x in logp]
    p_inv = [jnp.exp(-x) for x in logp]
    a_t = [a * jnp.exp(lp - l) for a, lp, l in zip(an, logp, lw)]
    b_t = [bf(x * pi) for x, pi in zip(bv, p_inv)]
    k_t = [bf(x * pi) for x, pi in zip(k, p_inv)]
    r_t = [x * pi for x, pi in zip(r, p_inc)]
    s2 = [st_ref[b, p] for b, p in pair_ids]
    s2b = [bf(x) for x in s2]
    a_s = [_dot_nt(bf(x), s) for x, s in zip(a_t, s2b)]
    r_s = [_dot_nt(bf(x), s) for x, s in zip(r_t, s2b)]
    mine = [head_of_lane == hh for _, hh in head_ids]
    only = lambda xs: [bf(jnp.where(m, xs[q], 0.0)) for (q, _), m in zip(head_ids, mine)]
    a_h, r_h, v_h = only(a_t), only(r_t), only(v)
    n_pow = [bf(jnp.where(strict, _dot_nt(a, b_t[q]), 0.0)) for a, (q, _) in zip(a_h, head_ids)]
    l_ak = [bf(jnp.where(strict, _dot_nt(a, k_t[q]), 0.0)) for a, (q, _) in zip(a_h, head_ids)]
    m_rb = [bf(jnp.where(lower, _dot_nt(x, b_t[q]), 0.0)) for x, (q, _) in zip(r_h, head_ids)]
    m_rk = [bf(jnp.where(lower, _dot_nt(x, k_t[q]), 0.0)) for x, (q, _) in zip(r_h, head_ids)]
    x = [jnp.where(m, a_s[q], 0.0) + _dot(l, vh) for (q, _), m, l, vh in zip(head_ids, mine, l_ak, v_h)]
    for lv in range(levels):
        x = [xi + _dot(n, bf(xi)) for xi, n in zip(x, n_pow)]
        if lv + 1 < levels:
            n_pow = [bf(_dot(n, n)) for n in n_pow]
    y_h = [jnp.where(m, r_s[q], 0.0) + _dot(mb, bf(xi)) + _dot(mk, vh)
           for (q, _), m, mb, mk, xi, vh in zip(head_ids, mine, m_rb, m_rk, x, v_h)]
    for q, (b, p) in enumerate(pair_ids):
        y_ref[b, :, p * LANES:(p + 1) * LANES] = y_h[2 * q] + y_h[2 * q + 1]
        uv_t = bf(jnp.concatenate([x[2 * q] + x[2 * q + 1], v[q]], axis=0).T)
        add = _dot(uv_t, jnp.concatenate([b_t[q], k_t[q]], axis=0))
        st_ref[b, p] = (s2[q] + jnp.where(same_head, add, 0.0)) * p_inc[q][c - 1:c, :]

    @pl.when(pl.program_id(1) == pl.num_programs(1) - 1)
    def _():
        sout_ref[...] = st_ref[...]


def _rwkv_scan(seqs, s0):
    b, s, _ = seqs[0].shape
    nb = 1
    c = _tile(s, RWKV_CHUNK)
    pairs = B_HEADS // 2
    seq = pl.BlockSpec((nb, c, B_WIDTH), lambda i, j: (i, j, 0))
    sts = pl.BlockSpec((nb, pairs, LANES, LANES), lambda i, j: (i, 0, 0, 0))
    return pl.pallas_call(
        _rwkv_scan_kernel,
        grid=(b // nb, s // c),
        in_specs=[seq] * 6 + [sts],
        out_specs=[seq, sts],
        out_shape=[jax.ShapeDtypeStruct((b, s, B_WIDTH), F32), jax.ShapeDtypeStruct((b, pairs, LANES, LANES), F32)],
        scratch_shapes=[pltpu.VMEM((nb, pairs, LANES, LANES), F32)],
        compiler_params=_cparams("parallel", "arbitrary"),
        name="rwkv_scan",
    )(*seqs, s0)


def _rwkv_post_kernel(y_ref, bonus_ref, g_ref, lg_ref, lb_ref, bd_ref, o_ref):
    bd = bd_ref[...]
    y = y_ref[...]
    d = y - _split_dot(y, bd) * (1.0 / B_HD)
    var = _split_dot(d * d, bd) * (1.0 / B_HD)
    yn = d * lax.rsqrt(var + B_LN_EPS) * lg_ref[...] + lb_ref[...]
    o_ref[...] = (yn + bonus_ref[...]) * g_ref[...]


def _rwkv_post(y2, bonus2, g2, w):
    t = y2.shape[0]
    tm = _tile(t, 512)
    row = pl.BlockSpec((tm, B_WIDTH), lambda i: (i, 0))
    vec = _full((1, B_WIDTH))
    return pl.pallas_call(
        _rwkv_post_kernel,
        grid=(t // tm,),
        in_specs=[row, row, row, vec, vec, _full((B_WIDTH, B_WIDTH))],
        out_specs=row,
        out_shape=jax.ShapeDtypeStruct((t, B_WIDTH), F32),
        compiler_params=_cparams("parallel"),
        name="rwkv_post",
    )(y2, bonus2, g2, w["b_ln_g"], w["b_ln_b"], w["bd512"])


def _out_proj_kernel(x_ref, o1_ref, o2_ref, w1_ref, w2_ref, gf_ref, wq_ref, xo_ref, xn_ref, q_ref):
    x = x_ref[...] + _dot(o1_ref[...].astype(BF16), w1_ref[...]) + _dot(o2_ref[...].astype(BF16), w2_ref[...])
    xo_ref[...] = x
    xn = (_rms(x) * gf_ref[...]).astype(BF16)
    xn_ref[...] = xn
    q_ref[...] = _dot(xn, wq_ref[...])


def _out_proj(x2, o1, o2, w1, w2, gf, wq):
    t = x2.shape[0]
    tm = _tile(t, 512)
    row = lambda n: pl.BlockSpec((tm, n), lambda i: (i, 0))
    half = o1.shape[1]
    return pl.pallas_call(
        _out_proj_kernel,
        grid=(t // tm,),
        in_specs=[row(D_MODEL), row(half), row(half), _full((half, D_MODEL)), _full((half, D_MODEL)),
                  _full((1, D_MODEL)), _full((D_MODEL, PEER_HEADS * PEER_DK))],
        out_specs=[row(D_MODEL), row(D_MODEL), row(PEER_HEADS * PEER_DK)],
        out_shape=[jax.ShapeDtypeStruct((t, D_MODEL), F32), jax.ShapeDtypeStruct((t, D_MODEL), BF16),
                   jax.ShapeDtypeStruct((t, PEER_HEADS * PEER_DK), F32)],
        compiler_params=_cparams("parallel"),
        name="out_proj",
    )(x2, o1, o2, w1, w2, gf, wq)


def _top16_rows(sc_ref, val_ref, idx_ref):
    nc, n, tb = sc_ref.shape
    rowi = lax.broadcasted_iota(I32, (n, tb), 0).astype(F32)

    def body(r, carry):
        for c in range(nc):
            sc = sc_ref[c]
            m = jnp.max(sc, axis=0, keepdims=True)
            ix = jnp.min(jnp.where(sc == m, rowi, float(n)), axis=0, keepdims=True)
            val_ref[c, pl.ds(r, 1), :] = m
            idx_ref[c, pl.ds(r, 1), :] = ix
            sc_ref[c] = jnp.where(rowi == ix, -jnp.inf, sc)
        return carry

    lax.fori_loop(0, PEER_TOPK, body, 0)


def _pair_candidates(s1, s2, x1, x2):
    sub = lax.broadcasted_iota(I32, (SUBLANES, s1.shape[1]), 0).astype(F32)
    cand, flat, eid = [], [], []
    for a in range(8):
        cand.append(s1[a:a + 1] + s2[0:8])
        flat.append(a * PEER_TOPK + sub)
        eid.append(x1[a:a + 1] * PEER_N_KEYS + x2[0:8])
    cand.append(s1[0:1] + s2[8:16])
    flat.append(8 + sub)
    eid.append(x1[0:1] * PEER_N_KEYS + x2[8:16])
    cand.append(s1[8:16] + s2[0:1])
    flat.append((8 + sub) * PEER_TOPK)
    eid.append(x1[8:16] * PEER_N_KEYS + x2[0:1])
    return jnp.concatenate(cand, axis=0), jnp.concatenate(flat, axis=0), jnp.concatenate(eid, axis=0)


def _peer_route_kernel(q_ref, keys_ref, e1_ref, e2_ref, gate_ref, sc_ref, topv, topi, cand_ref, eid_ref, selv, sele):
    for h in range(PEER_HEADS):
        qh = q_ref[:, h * PEER_DK:(h + 1) * PEER_DK].astype(BF16)
        sc_ref[2 * h] = _dot_nt(keys_ref[h, 0], qh)
        sc_ref[2 * h + 1] = _dot_nt(keys_ref[h, 1], qh)
    _top16_rows(sc_ref, topv, topi)
    for h in range(PEER_HEADS):
        cand, flat, eid = _pair_candidates(topv[2 * h], topv[2 * h + 1], topi[2 * h], topi[2 * h + 1])
        cand_ref[h] = cand
        eid_ref[h] = eid

    def body(r, carry):
        for h in range(PEER_HEADS):
            cand = cand_ref[h]
            m = jnp.max(cand, axis=0, keepdims=True)
            f = jnp.min(jnp.where(cand == m, flat, 1e9), axis=0, keepdims=True)
            hit = flat == f
            selv[h, pl.ds(r, 1), :] = m
            sele[h, pl.ds(r, 1), :] = jnp.max(jnp.where(hit, eid_ref[h], -1.0), axis=0, keepdims=True)
            cand_ref[h] = jnp.where(hit, -jnp.inf, cand)
        return carry

    lax.fori_loop(0, PEER_TOPK, body, 0)
    for h in range(PEER_HEADS):
        rows = slice(h * PEER_TOPK, (h + 1) * PEER_TOPK)
        top = selv[h]
        e = jnp.exp(top - top[0:1])
        gate_ref[rows, :] = e * (1.0 / jnp.sum(e, axis=0, keepdims=True))
        ex = sele[h].astype(I32)
        e1_ref[rows, :] = ex >> 7
        e2_ref[rows, :] = ex & (PEER_N_KEYS - 1)


def _peer_route(q2, keys_pad):
    t = q2.shape[0]
    tb = _tile(t, 128)
    nk = PEER_HEADS * PEER_TOPK
    out = pl.BlockSpec((nk, tb), lambda i: (0, i))
    return pl.pallas_call(
        _peer_route_kernel,
        grid=(t // tb,),
        in_specs=[pl.BlockSpec((tb, PEER_HEADS * PEER_DK), lambda i: (i, 0)),
                  _full((PEER_HEADS, 2, PEER_N_KEYS, PEER_DK))],
        out_specs=[out, out, out],
        out_shape=[jax.ShapeDtypeStruct((nk, t), I32)] * 2 + [jax.ShapeDtypeStruct((nk, t), F32)],
        scratch_shapes=[pltpu.VMEM((2 * PEER_HEADS, PEER_N_KEYS, tb), F32),
                        pltpu.VMEM((2 * PEER_HEADS, PEER_TOPK, tb), F32), pltpu.VMEM((2 * PEER_HEADS, PEER_TOPK, tb), F32),
                        pltpu.VMEM((PEER_HEADS, N_PAIR_CAND, tb), F32), pltpu.VMEM((PEER_HEADS, N_PAIR_CAND, tb), F32),
                        pltpu.VMEM((PEER_HEADS, PEER_TOPK, tb), F32), pltpu.VMEM((PEER_HEADS, PEER_TOPK, tb), F32)],
        compiler_params=_cparams("parallel"),
        name="peer_route",
    )(q2, keys_pad)


def _peer_gate_kernel(e1_ref, e2_ref, gate_ref, o_ref, scr):
    tg = e1_ref.shape[0]
    nk = PEER_HEADS * PEER_TOPK
    sub = lax.broadcasted_iota(I32, (PEER_N_KEYS, nk), 0)

    def body(c, carry):
        t0 = pl.multiple_of(c * SUBLANES, SUBLANES)
        e1 = e1_ref[pl.ds(t0, SUBLANES), :]
        e2 = e2_ref[pl.ds(t0, SUBLANES), :]
        g = gate_ref[pl.ds(t0, SUBLANES), :]
        mats = []
        for j in range(SUBLANES):
            a_t = jnp.where(sub == e1[j:j + 1], g[j:j + 1], 0.0).astype(BF16)
            b_t = jnp.where(sub == e2[j:j + 1], 1.0, 0.0).astype(BF16)
            mats.append(_dot_nt(a_t, b_t))
        scr[:, pl.ds(t0, SUBLANES), :] = jnp.swapaxes(jnp.stack(mats, axis=0), 0, 1)
        return carry

    lax.fori_loop(0, tg // SUBLANES, body, 0, unroll=4)
    o_ref[...] = scr[...].astype(BF16)


def _peer_gates(e1, e2, gate):
    t = e1.shape[0]
    tg = _tile(t, 64)
    row = pl.BlockSpec((tg, PEER_HEADS * PEER_TOPK), lambda i: (i, 0))
    return pl.pallas_call(
        _peer_gate_kernel,
        grid=(t // tg,),
        in_specs=[row, row, row],
        out_specs=pl.BlockSpec((PEER_N_KEYS, tg, PEER_N_KEYS), lambda i: (0, i, 0)),
        out_shape=jax.ShapeDtypeStruct((PEER_N_KEYS, t, PEER_N_KEYS), BF16),
        scratch_shapes=[pltpu.VMEM((PEER_N_KEYS, tg, PEER_N_KEYS), F32)],
        compiler_params=_cparams("parallel"),
        name="peer_gates",
    )(e1, e2, gate)


def _gelu(x):
    return 0.5 * x * (1.0 + lax.erf(x * (2.0 ** -0.5)))


def _peer_dense_kernel(xn_ref, ut_ref, v_ref, g_ref, x_ref, o_ref):
    @pl.when(pl.program_id(1) == 0)
    def _():
        o_ref[...] = x_ref[...]

    h = _gelu(_dot(xn_ref[...], ut_ref[...]))
    nblk = g_ref.shape[0]
    gh = [(h[:, c * LANES:(c + 1) * LANES] * g_ref[c].astype(F32)).astype(BF16) for c in range(nblk)]
    o_ref[...] += _dot(jnp.concatenate(gh, axis=-1), v_ref[...])


def _peer_dense(xn, ut, vtab, gates, x2):
    t = xn.shape[0]
    tb = _tile(t, 1024)
    eb = 1024
    return pl.pallas_call(
        _peer_dense_kernel,
        grid=(t // tb, PEER_N_EXPERTS // eb),
        in_specs=[pl.BlockSpec((tb, D_MODEL), lambda i, j: (i, 0)),
                  pl.BlockSpec((D_MODEL, eb), lambda i, j: (0, j)),
                  pl.BlockSpec((eb, D_MODEL), lambda i, j: (j, 0)),
                  pl.BlockSpec((eb // PEER_N_KEYS, tb, PEER_N_KEYS), lambda i, j: (j, i, 0)),
                  pl.BlockSpec((tb, D_MODEL), lambda i, j: (i, 0))],
        out_specs=pl.BlockSpec((tb, D_MODEL), lambda i, j: (i, 0)),
        out_shape=jax.ShapeDtypeStruct((t, D_MODEL), F32),
        compiler_params=_cparams("parallel", "arbitrary"),
        name="peer_dense",
    )(xn, ut, vtab, gates, x2)


def _peer(x2, xn, q2, pw):
    e1, e2, gate = _peer_route(q2, pw["keys"])
    gates = _peer_gates(e1.T, e2.T, gate.T)
    return _peer_dense(xn, pw["ut"], pw["v"], gates, x2)


def _in1_kernel(x_ref, g_ref, wq_ref, wk_ref, wv_ref, wiq_ref, wik_ref, wiw_ref, wd_ref,
                q_ref, k_ref, v_ref, iq_ref, ik_ref, iw_ref, d_ref):
    xn = (_rms(x_ref[...]) * g_ref[...]).astype(BF16)
    for w_ref, o_ref in ((wq_ref, q_ref), (wk_ref, k_ref), (wv_ref, v_ref), (wiq_ref, iq_ref),
                         (wik_ref, ik_ref), (wiw_ref, iw_ref), (wd_ref, d_ref)):
        o_ref[...] = _dot(xn, w_ref[...])


def _in_proj1(x2, w):
    t = x2.shape[0]
    tm = _tile(t, 512)
    widths = [C_HEADS * C_HD, LANES, LANES, C_IDX_HEADS * C_IDX_D, LANES, LANES, 2 * D_WIDTH]
    row = lambda n: pl.BlockSpec((tm, n), lambda i: (i, 0))
    return pl.pallas_call(
        _in1_kernel,
        grid=(t // tm,),
        in_specs=[row(D_MODEL), _full((1, D_MODEL))] + [_full((D_MODEL, n)) for n in widths],
        out_specs=[row(n) for n in widths],
        out_shape=[jax.ShapeDtypeStruct((t, n), F32) for n in widths],
        compiler_params=_cparams("parallel"),
        name="in_proj1",
    )(x2, w["l1_norm"], w["w_cq1"], w["w_ck1"], w["w_cv1"], w["w_ciq"], w["w_cik"], w["w_ciw"], w["w_d"])


def _dsa_prep_kernel(q_ref, k_ref, iq_ref, ik_ref, qg_ref, kg_ref, ig_ref, cos_ref, sin_ref, bd_ref, perm_ref,
                     qo_ref, ko_ref, iqo_ref, iko_ref):
    cos, sin, bd, perm = cos_ref[...], sin_ref[...], bd_ref[...], perm_ref[...]

    def norm(x, gain, n):
        ss = _split_dot(x * x, bd[:n, :n]) * (1.0 / C_HD)
        return x * lax.rsqrt(ss + NORM_EPS) * gain

    def rope(x, n):
        return x * cos[:, :n] + _split_dot(x, perm[:n, :n]) * sin[:, :n]

    nq = C_HEADS * C_HD
    qo_ref[0] = (rope(norm(q_ref[0], qg_ref[...], nq), nq) * (C_HD ** -0.5)).astype(BF16)
    ko_ref[0] = rope(norm(k_ref[0], kg_ref[...], LANES), LANES)
    iqo_ref[0] = (rope(iq_ref[0], nq) * (C_IDX_D ** -0.5)).astype(BF16)
    iko_ref[0] = rope(norm(ik_ref[0], ig_ref[...], LANES), LANES)


def _dsa_prep(q, k, iq, ik, w, cos, sin):
    b, s, _ = q.shape
    ts = _tile(s, 256)
    nq = C_HEADS * C_HD
    blk = lambda n: pl.BlockSpec((1, ts, n), lambda i, j: (i, j, 0))
    tab = pl.BlockSpec((ts, nq), lambda i, j: (j, 0))
    return pl.pallas_call(
        _dsa_prep_kernel,
        grid=(b, s // ts),
        in_specs=[blk(nq), blk(LANES), blk(nq), blk(LANES), _full((1, nq)), _full((1, LANES)), _full((1, LANES)),
                  tab, tab, _full((nq, nq)), _full((nq, nq))],
        out_specs=[blk(nq), blk(LANES), blk(nq), blk(LANES)],
        out_shape=[jax.ShapeDtypeStruct((b, s, nq), BF16), jax.ShapeDtypeStruct((b, s, LANES), F32),
                   jax.ShapeDtypeStruct((b, s, nq), BF16), jax.ShapeDtypeStruct((b, s, LANES), F32)],
        compiler_params=_cparams("parallel", "parallel"),
        name="dsa_prep",
    )(q, k, iq, ik, w["c_q_gain"], w["c_k_gain"], w["c_kidx_gain"], cos, sin, w["bd512"], w["perm_c"])


KEY_OF_NEG_INF = -(2 ** 31) + 0x7FFFFF


def _key_to_float(key):
    return pltpu.bitcast(jnp.where(key < 0, key ^ jnp.int32(0x7FFFFFFF), key), F32)


def _topk_mask(score, n_sel, su):
    tq, lp = score.shape

    def body(i, tau):
        trial = tau + lax.shift_left(jnp.int32(1), 31 - i)
        cnt = jnp.sum(jnp.where(score >= _key_to_float(trial), 1.0, 0.0), axis=-1, keepdims=True)
        return jnp.where(trial <= KEY_OF_NEG_INF, trial, jnp.where(cnt >= n_sel, trial, tau))

    kth = _key_to_float(lax.fori_loop(0, 32, body, jnp.full((tq, 1), -(2 ** 31), I32)))
    gt = jnp.where(score > kth, 1.0, 0.0)
    eq = jnp.where(score == kth, 1.0, 0.0)
    need = n_sel - jnp.sum(gt, axis=-1, keepdims=True)
    parts = []
    before = jnp.zeros((tq, 1), F32)
    for c in range(lp // LANES):
        eqc = eq[:, c * LANES:(c + 1) * LANES]
        rank = before + _dot(eqc.astype(BF16), su)
        parts.append(gt[:, c * LANES:(c + 1) * LANES] + jnp.where(rank < need, eqc, 0.0))
        before = before + jnp.sum(eqc, axis=-1, keepdims=True)
    return jnp.concatenate(parts, axis=-1)


def _dsa_attn_kernel(q_ref, iq_ref, iw_ref, ik_ref, k_ref, v_ref, su_ref, o_ref, *, past, true_len, n_sel):
    tq = q_ref.shape[1]
    low = lax.broadcasted_iota(I32, (1, LANES), 1) < C_HD
    halves = lambda x: (jnp.where(low, x, jnp.zeros((), x.dtype)), jnp.where(low, jnp.zeros((), x.dtype), x))

    def run(n, q0):
        vis = _visible(tq, n, q0, true_len)
        iw = iw_ref[0] * (C_IDX_HEADS ** -0.5)
        ik2 = ik_ref[0, :n]
        score = jnp.zeros((tq, n), F32)
        for m in range(C_IDX_HEADS // 2):
            iqa, iqb = halves(iq_ref[0, :, m * LANES:(m + 1) * LANES])
            score = score + jnp.maximum(_dot_nt(iqa, ik2), 0.0) * iw[:, 2 * m:2 * m + 1]
            score = score + jnp.maximum(_dot_nt(iqb, ik2), 0.0) * iw[:, 2 * m + 1:2 * m + 2]
        score = jnp.where(vis, score, NEG_INF)
        keep = jnp.where(vis, _topk_mask(score, n_sel, su_ref[...]), 0.0) > 0.5
        k = k_ref[0, :n]
        va, vb = halves(v_ref[0, :n])
        for m in range(C_HEADS // 2):
            qa, qb = halves(q_ref[0, :, m * LANES:(m + 1) * LANES])
            pa, inva = _masked_softmax(_dot_nt(qa, k), keep)
            pb, invb = _masked_softmax(_dot_nt(qb, k), keep)
            o_ref[0, :, m * LANES:(m + 1) * LANES] = (_dot(pa.astype(BF16), va) * inva
                                                      + _dot(pb.astype(BF16), vb) * invb)

    _for_visible_prefix(run, tq, ik_ref.shape[1], past)


def _dsa_attn(q, iq, iw, ik2, k, v, su, past, true_len):
    b, s, nq = q.shape
    lp = ik2.shape[1]
    tq = _tile(s, 256)
    n_sel = min(C_TOPK, true_len // 4)
    qblk = lambda n: pl.BlockSpec((1, tq, n), lambda i, j: (i, j, 0))
    keys = pl.BlockSpec((1, lp, LANES), lambda i, j: (i, 0, 0))
    return pl.pallas_call(
        functools.partial(_dsa_attn_kernel, past=past, true_len=true_len, n_sel=n_sel),
        grid=(b, s // tq),
        in_specs=[qblk(nq), qblk(nq), qblk(LANES), keys, keys, keys, _full((LANES, LANES))],
        out_specs=qblk(nq),
        out_shape=jax.ShapeDtypeStruct((b, s, nq), F32),
        compiler_params=_cparams("parallel", "arbitrary"),
        name="dsa_attn",
    )(q, iq, iw, ik2, k, v, su)


def _gmlp_kernel(pd_ref, lg_ref, lb_ref, ws_ref, bs_ref, o_ref, dv_ref):
    span = pd_ref.shape[1]
    h = _gelu(pd_ref[0])
    u = h[:, :D_WIDTH]
    v = h[:, D_WIDTH:]
    mu = jnp.mean(v, axis=-1, keepdims=True)
    d = v - mu
    v = d * lax.rsqrt(jnp.mean(d * d, axis=-1, keepdims=True) + NORM_EPS) * lg_ref[...] + lb_ref[...]
    dv_ref[0] = v
    causal = lax.broadcasted_iota(I32, (span, span), 1) <= lax.broadcasted_iota(I32, (span, span), 0)
    gw = D_WIDTH // D_GROUPS
    mixed = [_dot(jnp.where(causal, ws_ref[g], 0.0).astype(BF16), v[:, g * gw:(g + 1) * gw].astype(BF16))
             for g in range(D_GROUPS)]
    o_ref[0] = u * (jnp.concatenate(mixed, axis=-1) + bs_ref[...])


def _gmlp(pd, w):
    b, s, _ = pd.shape
    span = D_SPAN if s % D_SPAN == 0 else s
    ws = w["d_ws"][:, :span, :span]
    bs = jnp.repeat(w["d_bs"][:, :span].T, D_WIDTH // D_GROUPS, axis=1)
    blk = lambda n: pl.BlockSpec((1, span, n), lambda i, j: (i, j, 0))
    return pl.pallas_call(
        _gmlp_kernel,
        grid=(b, s // span),
        in_specs=[blk(2 * D_WIDTH), _full((1, D_WIDTH)), _full((1, D_WIDTH)), _full((D_GROUPS, span, span)),
                  _full((span, D_WIDTH))],
        out_specs=[blk(D_WIDTH), blk(D_WIDTH)],
        out_shape=[jax.ShapeDtypeStruct((b, s, D_WIDTH), F32)] * 2,
        compiler_params=_cparams("parallel", "parallel"),
        name="gmlp",
    )(pd, w["d_ln_g"], w["d_ln_b"], ws, bs)


def _pad_cols(w, groups, width, slot, off=0):
    k = w.shape[0]
    w = jnp.pad(w.reshape(k, groups, width), ((0, 0), (0, 0), (off, slot - off - width)))
    return w.reshape(k, groups * slot)


def _prep_weights(p):
    row = lambda v: v.reshape(1, -1).astype(F32)
    bf = lambda v: v.astype(BF16)
    w = {}
    w_in0 = p["l0_w_in"]
    w["l0_norm"] = row(p["l0_norm_mix"])
    w["w_cq"] = bf(w_in0[:, :A_D_CQ])
    w["w_ckv"] = bf(w_in0[:, A_D_CQ:A_D_CQ + A_D_C])
    a_cols = A_D_CQ + A_D_C + A_D_ROPE
    w["w_krp"] = bf(_pad_cols(w_in0[:, A_D_CQ + A_D_C:a_cols], 1, A_D_ROPE, LANES, A_D_NOPE))
    w["w_b"] = bf(w_in0[:, a_cols:])
    w["a_kv_norm"] = row(p["a_kv_norm"])
    w["a_q_norm"] = row(p["a_q_norm"])
    w["w_uq"] = bf(_pad_cols(p["a_w_uq"], A_HEADS, A_D_QK, LANES))
    ukv = p["a_w_ukv"].reshape(A_D_C, A_HEADS, A_D_NOPE + A_D_V)
    w["w_uk"] = bf(_pad_cols(ukv[:, :, :A_D_NOPE].reshape(A_D_C, -1), A_HEADS, A_D_NOPE, LANES))
    uv = ukv[:, :, A_D_NOPE:].reshape(A_D_C, A_HEADS // 2, 2, A_D_V)
    slot = lambda x, off: jnp.pad(x, ((0, 0), (0, 0), (off, LANES - off - A_D_V)))
    uv_pad = jnp.stack([slot(uv[:, :, 0], 0), slot(uv[:, :, 1], A_D_V)], axis=2)
    w["w_uv"] = bf(uv_pad.reshape(A_D_C, A_HEADS * LANES))
    w["a_q_gain"] = jnp.pad(row(p["a_q_gain"]), ((0, 0), (0, LANES - A_D_QK)))
    w["a_k_gain"] = jnp.pad(row(p["a_k_gain"]), ((0, 0), (0, LANES - A_D_QK)))
    w["perm_a"] = _rope_perm(LANES, LANES, A_D_NOPE, A_D_ROPE // 2)
    w["b_mu"] = row(p["b_mu"])
    w["b_w0"] = row(p["b_w0"])
    w["b_a0"] = row(p["b_a0"])
    zeros = jnp.zeros((B_W_LORA, B_WIDTH), F32)
    w["b_w2"] = bf(jnp.concatenate([p["b_w2"], zeros], axis=0))
    w["b_a2"] = bf(jnp.concatenate([zeros, p["b_a2"]], axis=0))
    w["b_g2"] = bf(p["b_g2"])
    for n in ("b_kk", "b_ka", "b_rk", "b_ln_g", "b_ln_b"):
        w[n] = row(p[n])
    w["bd512"] = _block_ones(B_WIDTH, B_HD)
    w["w_out0a"] = bf(p["l0_w_out"][:A_HEADS * A_D_V])
    w["w_out0b"] = bf(p["l0_w_out"][A_HEADS * A_D_V:])
    w_in1 = p["l1_w_in"]
    nq = C_HEADS * C_HD
    nkv = C_KV_HEADS * C_HD
    o = 0
    order = jnp.arange(C_HEADS).reshape(C_KV_HEADS, C_HEADS // C_KV_HEADS).T.reshape(-1)
    w["w_cq1"] = bf(w_in1[:, o:o + nq].reshape(D_MODEL, C_HEADS, C_HD)[:, order].reshape(D_MODEL, nq)); o += nq
    w["w_ck1"] = bf(w_in1[:, o:o + nkv]); o += nkv
    w["w_cv1"] = bf(w_in1[:, o:o + nkv]); o += nkv
    w["w_ciq"] = bf(w_in1[:, o:o + nq]); o += nq
    w["w_cik"] = bf(_pad_cols(w_in1[:, o:o + C_IDX_D], 1, C_IDX_D, LANES)); o += C_IDX_D
    w["w_ciw"] = bf(_pad_cols(w_in1[:, o:o + C_IDX_HEADS], 1, C_IDX_HEADS, LANES)); o += C_IDX_HEADS
    w["w_d"] = bf(w_in1[:, o:])
    w["l1_norm"] = row(p["l1_norm_mix"])
    w["c_q_gain"] = row(jnp.tile(p["c_q_gain"], C_HEADS))
    w["c_k_gain"] = row(jnp.tile(p["c_k_gain"], C_KV_HEADS))
    w["c_kidx_gain"] = jnp.pad(row(p["c_kidx_gain"]), ((0, 0), (0, LANES - C_IDX_D)))
    w["perm_c"] = _rope_perm(nq, C_HD, 0, C_HD // 2)
    w["su"] = (jnp.arange(LANES)[:, None] < jnp.arange(LANES)[None, :]).astype(BF16)
    w["d_ln_g"] = row(p["d_ln_g"])
    w["d_ln_b"] = row(p["d_ln_b"])
    w["d_ws"] = p["d_ws"]
    w["d_bs"] = p["d_bs"]
    w["w_out1a"] = bf(p["l1_w_out"][:nq].reshape(C_HEADS, C_HD, D_MODEL)[order].reshape(nq, D_MODEL))
    w["w_out1b"] = bf(p["l1_w_out"][nq:])
    for l in (0, 1):
        keys = p[f"l{l}_peer_keys"]
        half = PEER_DK // 2
        edge = lambda lo, hi: ((0, 0), (0, 0), (lo, hi))
        kp = jnp.stack([jnp.pad(keys[:, 0], edge(0, half)), jnp.pad(keys[:, 1], edge(half, 0))], axis=1)
        w[f"peer{l}"] = {"keys": bf(kp), "ut": bf(p[f"l{l}_peer_u"]).T, "v": bf(p[f"l{l}_peer_v"]),
                         "norm": row(p[f"l{l}_norm_ffn"]), "wq": bf(p[f"l{l}_peer_wq"])}
    return w


def _pad_keys(x, lp):
    return jnp.pad(x, ((0, 0), (0, lp - x.shape[1])) + ((0, 0),) * (x.ndim - 2))


def _trunk(x, st, w):
    b, s, _ = x.shape
    t = b * s
    past = st["a_ckv"].shape[1]
    true_len = past + s
    lp = -(-true_len // LANES) * LANES
    kpos = jnp.arange(lp, dtype=I32)

    cq, ckv, krp, pb = _in_proj0(x.reshape(t, D_MODEL), w)
    cos_a, sin_a = _rope_tables(kpos, A_D_ROPE // 2, LANES, A_D_NOPE, 1)
    q = _mla_q(cq.reshape(b, s, A_D_CQ), w, cos_a[past:true_len], sin_a[past:true_len])
    ckv3 = ckv.reshape(b, s, A_D_C)
    krp3 = krp.reshape(b, s, LANES)
    krope = krp3[:, :, A_D_NOPE:A_D_NOPE + A_D_ROPE]
    cache_krp = jnp.pad(st["a_krope"], ((0, 0), (0, 0), (A_D_NOPE, LANES - A_D_NOPE - A_D_ROPE)))
    ckv_all = _pad_keys(jnp.concatenate([st["a_ckv"], ckv3], axis=1), lp)
    krp_all = _pad_keys(jnp.concatenate([cache_krp, krp3], axis=1), lp)
    k_a, v_a = _mla_kv(ckv_all, krp_all, w, cos_a, sin_a)
    o_a = _mla_attn(q, k_a, v_a, past, true_len)

    pb3 = pb.reshape(b, s, B_PROJ)
    r, logw, k2, v, an, bv, g, bonus = _rwkv_prep(pb3, st["b_shift"], w)
    pairs = B_HEADS // 2
    sp = st["b_wkv"].reshape(b, pairs, 2, B_HD, B_HD)
    edge = lambda lo, hi: ((0, 0), (0, 0), (0, 0), (lo, hi))
    s0 = jnp.concatenate([jnp.pad(sp[:, :, 0], edge(0, B_HD)), jnp.pad(sp[:, :, 1], edge(B_HD, 0))], axis=2)
    y, s_out = _rwkv_scan([r, logw, k2, v, an, bv], s0)
    wkv = jnp.stack([s_out[:, :, :B_HD, :B_HD], s_out[:, :, B_HD:, B_HD:]], axis=2).reshape(b, B_HEADS, B_HD, B_HD)
    o_b = _rwkv_post(y.reshape(t, B_WIDTH), bonus.reshape(t, B_WIDTH), g.reshape(t, B_WIDTH), w)
    pw = w["peer0"]
    x1, xn1, pq1 = _out_proj(x.reshape(t, D_MODEL), o_a.reshape(t, -1), o_b, w["w_out0a"], w["w_out0b"],
                             pw["norm"], pw["wq"])
    x2 = _peer(x1, xn1, pq1, pw)

    nq = C_HEADS * C_HD
    cq1, ck1, cv1, ciq, cik, ciw, pd = _in_proj1(x2, w)
    pos_q = kpos[past:true_len]
    cos_c, sin_c = _rope_tables(pos_q, C_HD // 2, C_HD, 0, C_HEADS)
    three = lambda a: a.reshape(b, s, a.shape[-1])
    q_c, k_c, iq_c, ik_c = _dsa_prep(three(cq1), three(ck1), three(ciq), three(cik), w, cos_c, sin_c)
    c_k = k_c.reshape(b, s, C_KV_HEADS, C_HD)
    c_v = cv1.reshape(b, s, C_KV_HEADS, C_HD)
    c_kidx = ik_c[:, :, :C_IDX_D]
    flat_kv = lambda cache, new: _pad_keys(
        jnp.concatenate([cache.reshape(b, past, LANES), new], axis=1), lp).astype(BF16)
    ik_all = jnp.concatenate([st["c_kidx"], c_kidx], axis=1)
    ik2 = _pad_keys(jnp.concatenate([ik_all, ik_all], axis=-1), lp).astype(BF16)
    o_c = _dsa_attn(q_c, iq_c, three(ciw), ik2, flat_kv(st["c_k"], k_c), flat_kv(st["c_v"], three(cv1)),
                    w["su"], past, true_len)
    o_d, d_v = _gmlp(three(pd), w)
    pw = w["peer1"]
    x3, xn3, pq3 = _out_proj(x2, o_c.reshape(t, nq), o_d.reshape(t, D_WIDTH), w["w_out1a"], w["w_out1b"],
                             pw["norm"], pw["wq"])
    y_out = _peer(x3, xn3, pq3, pw).reshape(b, s, D_MODEL)
    return y_out, [ckv3, krope, wkv, pb3[:, -1], c_k, c_v, c_kidx, d_v]


def kernel(x_prompt, x_sample, cache_a_ckv, cache_a_krope, state_b_wkv, state_b_shift, cache_c_k, cache_c_v, cache_c_kidx, l0_norm_mix, l0_w_in, a_q_norm, a_w_uq, a_kv_norm, a_w_ukv, a_q_gain, a_k_gain, b_mu, b_w0, b_w2, b_a0, b_a2, b_g2, b_kk, b_ka, b_rk, b_ln_g, b_ln_b, l0_w_out, l0_norm_ffn, l0_peer_wq, l0_peer_keys, l0_peer_u, l0_peer_v, l1_norm_mix, l1_w_in, c_q_gain, c_k_gain, c_kidx_gain, d_ln_g, d_ln_b, d_ws, d_bs, l1_w_out, l1_norm_ffn, l1_peer_wq, l1_peer_keys, l1_peer_u, l1_peer_v):
    params = dict(
        l0_norm_mix=l0_norm_mix, l0_w_in=l0_w_in, a_q_norm=a_q_norm, a_w_uq=a_w_uq, a_kv_norm=a_kv_norm,
        a_w_ukv=a_w_ukv, a_q_gain=a_q_gain, a_k_gain=a_k_gain, b_mu=b_mu, b_w0=b_w0, b_w2=b_w2, b_a0=b_a0,
        b_a2=b_a2, b_g2=b_g2, b_kk=b_kk, b_ka=b_ka, b_rk=b_rk.reshape(-1), b_ln_g=b_ln_g, b_ln_b=b_ln_b,
        l0_w_out=l0_w_out, l0_norm_ffn=l0_norm_ffn, l0_peer_wq=l0_peer_wq, l0_peer_keys=l0_peer_keys,
        l0_peer_u=l0_peer_u, l0_peer_v=l0_peer_v, l1_norm_mix=l1_norm_mix, l1_w_in=l1_w_in, c_q_gain=c_q_gain,
        c_k_gain=c_k_gain, c_kidx_gain=c_kidx_gain, d_ln_g=d_ln_g, d_ln_b=d_ln_b, d_ws=d_ws, d_bs=d_bs,
        l1_w_out=l1_w_out, l1_norm_ffn=l1_norm_ffn, l1_peer_wq=l1_peer_wq, l1_peer_keys=l1_peer_keys,
        l1_peer_u=l1_peer_u, l1_peer_v=l1_peer_v)
    w = _prep_weights(params)
    bp = x_prompt.shape[0]
    dt = x_prompt.dtype
    prompt_state = dict(
        a_ckv=jnp.zeros((bp, 0, A_D_C), dt), a_krope=jnp.zeros((bp, 0, A_D_ROPE), dt),
        b_wkv=jnp.zeros((bp, B_HEADS, B_HD, B_HD), dt), b_shift=jnp.zeros((bp, B_PROJ), dt),
        c_k=jnp.zeros((bp, 0, C_KV_HEADS, C_HD), dt), c_v=jnp.zeros((bp, 0, C_KV_HEADS, C_HD), dt),
        c_kidx=jnp.zeros((bp, 0, C_IDX_D), dt))
    sample_state = dict(a_ckv=cache_a_ckv, a_krope=cache_a_krope, b_wkv=state_b_wkv, b_shift=state_b_shift,
                        c_k=cache_c_k, c_v=cache_c_v, c_kidx=cache_c_kidx)
    y_p, new_p = _trunk(x_prompt, prompt_state, w)
    y_s, new_s = _trunk(x_sample, sample_state, w)
    return (y_p, y_s, *new_p[:7], *new_s)
```

```python
import functools
import math

import jax
import jax.numpy as jnp
from jax import lax
from jax.experimental import pallas as pl
from jax.experimental.pallas import tpu as pltpu

F32 = jnp.float32
BF16 = jnp.bfloat16
I32 = jnp.int32

D_MODEL = 1024
CHUNK = 64
CHUNK_SHIFT = 6
KEY_BLOCK = 512
ROPE_THETA = 10000.0
NORM_EPS = 1e-6
NEG_INF = -1e30
LANES = 128
SUBLANES = 8

A_HEADS = 8
A_D_NOPE = 64
A_D_ROPE = 32
A_D_QK = A_D_NOPE + A_D_ROPE
A_D_V = 64
A_D_CQ = 256
A_D_C = 128
B_HEADS = 8
B_HD = 64
B_WIDTH = B_HEADS * B_HD
B_W_LORA = 64
B_A_LORA = 64
B_G_LORA = 128
B_PROJ = 3 * B_WIDTH + B_W_LORA + B_A_LORA + B_G_LORA
B_LN_EPS = 64e-5
RWKV_CHUNK = 128
C_HEADS = 8
C_KV_HEADS = 2
C_HD = 64
C_IDX_HEADS = 8
C_IDX_D = 64
C_TOPK = 256
D_GROUPS = 4
D_WIDTH = 512
D_SPAN = 128
PEER_HEADS = 8
PEER_N_KEYS = 128
PEER_N_EXPERTS = PEER_N_KEYS * PEER_N_KEYS
PEER_DK = 128
PEER_TOPK = 16
N_PAIR_CAND = 64

VMEM_LIMIT = 48 * 1024 * 1024
NT = (((1,), (1,)), ((), ()))


def _tile(n, target):
    t = min(n, target)
    while n % t:
        t -= 8
    return t


def _cparams(*sem):
    return pltpu.CompilerParams(dimension_semantics=sem, vmem_limit_bytes=VMEM_LIMIT)


def _dot(a, b):
    return jnp.dot(a, b, preferred_element_type=F32)


def _dot_nt(a, b):
    return lax.dot_general(a, b, NT, preferred_element_type=F32)


def _split_dot(x, w, passes=2):
    hi = x.astype(BF16)
    acc = _dot(hi, w)
    rem = x - hi.astype(F32)
    for _ in range(passes - 1):
        part = rem.astype(BF16)
        acc = acc + _dot(part, w)
        rem = rem - part.astype(F32)
    return acc


def _rms(x, eps=NORM_EPS):
    return x * lax.rsqrt(jnp.mean(x * x, axis=-1, keepdims=True) + eps)


def _full(shape):
    nd = len(shape)
    return pl.BlockSpec(shape, lambda *_: (0,) * nd)


def _block_ones(n, seg):
    i = jnp.arange(n)
    return (i[:, None] // seg == i[None, :] // seg).astype(BF16)


def _rope_perm(n, seg, off, half):
    r = jnp.arange(n)[:, None]
    c = jnp.arange(n)[None, :]
    same = r // seg == c // seg
    rr = r % seg - off
    cc = c % seg - off
    plus = same & (rr >= 0) & (rr < half) & (cc == rr + half)
    minus = same & (cc >= 0) & (cc < half) & (rr == cc + half)
    return (plus.astype(F32) - minus.astype(F32)).astype(BF16)


def _rope_tables(pos, half, seg, off, reps):
    inv_freq = ROPE_THETA ** (-jnp.arange(half, dtype=F32) / half)
    ang = pos.astype(F32)[:, None] * inv_freq[None, :]
    n = pos.shape[0]
    edge = ((0, 0), (off, seg - off - 2 * half))
    cos = jnp.pad(jnp.tile(jnp.cos(ang), (1, 2)), edge, constant_values=1.0)
    sin = jnp.pad(jnp.tile(jnp.sin(ang), (1, 2)), edge)
    return jnp.tile(cos, (1, reps)), jnp.tile(sin, (1, reps))


def _in0_kernel(x_ref, g_ref, wcq_ref, wckv_ref, wkr_ref, wb_ref, kvn_ref,
                cq_ref, ckv_ref, krp_ref, pb_ref):
    xn = (_rms(x_ref[...]) * g_ref[...]).astype(BF16)
    cq_ref[...] = _dot(xn, wcq_ref[...])
    ckv_ref[...] = _rms(_dot(xn, wckv_ref[...])) * kvn_ref[...]
    krp_ref[...] = _dot(xn, wkr_ref[...])
    pb_ref[...] = _dot(xn, wb_ref[...])


def _in_proj0(x2, w):
    t = x2.shape[0]
    tm = _tile(t, 512)
    row = lambda n: pl.BlockSpec((tm, n), lambda i: (i, 0))
    return pl.pallas_call(
        _in0_kernel,
        grid=(t // tm,),
        in_specs=[row(D_MODEL), _full((1, D_MODEL)), _full((D_MODEL, A_D_CQ)), _full((D_MODEL, A_D_C)),
                  _full((D_MODEL, LANES)), _full((D_MODEL, B_PROJ)), _full((1, A_D_C))],
        out_specs=[row(A_D_CQ), row(A_D_C), row(LANES), row(B_PROJ)],
        out_shape=[jax.ShapeDtypeStruct((t, A_D_CQ), F32), jax.ShapeDtypeStruct((t, A_D_C), F32),
                   jax.ShapeDtypeStruct((t, LANES), F32), jax.ShapeDtypeStruct((t, B_PROJ), F32)],
        compiler_params=_cparams("parallel"),
        name="in_proj0",
    )(x2, w["l0_norm"], w["w_cq"], w["w_ckv"], w["w_krp"], w["w_b"], w["a_kv_norm"])


def _head_norm_rope(xh, gain, cos, sin, perm):
    ss = jnp.sum(xh * xh, axis=-1, keepdims=True) * (1.0 / A_D_QK)
    xh = xh * lax.rsqrt(ss + NORM_EPS) * gain
    return xh * cos + _split_dot(xh, perm) * sin


def _mla_q_kernel(cq_ref, qn_ref, wq_ref, gain_ref, cos_ref, sin_ref, perm_ref, o_ref):
    cqn = (_rms(cq_ref[0]) * qn_ref[...]).astype(BF16)
    q = _dot(cqn, wq_ref[...])
    cos, sin, perm, gain = cos_ref[...], sin_ref[...], perm_ref[...], gain_ref[...]
    for h in range(A_HEADS):
        qh = _head_norm_rope(q[:, h * LANES:(h + 1) * LANES], gain, cos, sin, perm)
        o_ref[0, :, h * LANES:(h + 1) * LANES] = (qh * (A_D_QK ** -0.5)).astype(BF16)


def _mla_q(cq, w, cos, sin):
    b, s, _ = cq.shape
    ts = _tile(s, 256)
    return pl.pallas_call(
        _mla_q_kernel,
        grid=(b, s // ts),
        in_specs=[pl.BlockSpec((1, ts, A_D_CQ), lambda i, j: (i, j, 0)), _full((1, A_D_CQ)),
                  _full((A_D_CQ, A_HEADS * LANES)), _full((1, LANES)),
                  pl.BlockSpec((ts, LANES), lambda i, j: (j, 0)), pl.BlockSpec((ts, LANES), lambda i, j: (j, 0)),
                  _full((LANES, LANES))],
        out_specs=pl.BlockSpec((1, ts, A_HEADS * LANES), lambda i, j: (i, j, 0)),
        out_shape=jax.ShapeDtypeStruct((b, s, A_HEADS * LANES), BF16),
        compiler_params=_cparams("parallel", "parallel"),
        name="mla_q",
    )(cq, w["a_q_norm"], w["w_uq"], w["a_q_gain"], cos, sin, w["perm_a"])


def _mla_kv_kernel(ckv_ref, krp_ref, wk_ref, wv_ref, gain_ref, cos_ref, sin_ref, perm_ref, k_ref, v_ref):
    c = ckv_ref[0].astype(BF16)
    k = _dot(c, wk_ref[...])
    v_ref[0] = _dot(c, wv_ref[...]).astype(BF16)
    krp = krp_ref[0]
    cos, sin, perm, gain = cos_ref[...], sin_ref[...], perm_ref[...], gain_ref[...]
    for h in range(A_HEADS):
        kh = _head_norm_rope(k[:, h * LANES:(h + 1) * LANES] + krp, gain, cos, sin, perm)
        k_ref[0, :, h * LANES:(h + 1) * LANES] = kh.astype(BF16)


def _mla_kv(ckv_all, krp_all, w, cos, sin):
    b, lp, _ = ckv_all.shape
    tl = _tile(lp, 256)
    blk = lambda n: pl.BlockSpec((1, tl, n), lambda i, j: (i, j, 0))
    tab = pl.BlockSpec((tl, LANES), lambda i, j: (j, 0))
    return pl.pallas_call(
        _mla_kv_kernel,
        grid=(b, lp // tl),
        in_specs=[blk(A_D_C), blk(LANES), _full((A_D_C, A_HEADS * LANES)), _full((A_D_C, A_HEADS * LANES)),
                  _full((1, LANES)), tab, tab, _full((LANES, LANES))],
        out_specs=[blk(A_HEADS * LANES), blk(A_HEADS * LANES)],
        out_shape=[jax.ShapeDtypeStruct((b, lp, A_HEADS * LANES), BF16)] * 2,
        compiler_params=_cparams("parallel", "parallel"),
        name="mla_kv",
    )(ckv_all, krp_all, w["w_uk"], w["w_uv"], w["a_k_gain"], cos, sin, w["perm_a"])


def _visible(tq, lp, q0, true_len):
    qpos = q0 + lax.broadcasted_iota(I32, (tq, 1), 0)
    kpos = lax.broadcasted_iota(I32, (1, lp), 1)
    kchunk = jnp.where(kpos < true_len, kpos >> CHUNK_SHIFT, jnp.int32(1 << 30))
    return kchunk <= (qpos >> CHUNK_SHIFT)


def _masked_softmax(s, keep):
    s = jnp.where(keep, s, NEG_INF)
    p = jnp.exp(s - jnp.max(s, axis=-1, keepdims=True))
    return p, 1.0 / jnp.sum(p, axis=-1, keepdims=True)


def _for_visible_prefix(run, tq, lp, past):
    q0 = past + pl.program_id(1) * tq
    need = (((q0 + tq - 1) >> CHUNK_SHIFT) + 1) << CHUNK_SHIFT
    sizes = [min(c * KEY_BLOCK, lp) for c in range(1, -(-lp // KEY_BLOCK) + 1)]
    case = (need + (KEY_BLOCK - 1)) // KEY_BLOCK
    for c, n in enumerate(sizes, start=1):
        pl.when(case == c)(functools.partial(run, n, q0))


def _mla_attn_kernel(q_ref, k_ref, v_ref, o_ref, *, past, true_len):
    tq = q_ref.shape[1]

    def run(n, q0):
        vis = _visible(tq, n, q0, true_len)
        for m in range(A_HEADS // 2):
            acc = None
            for h in (2 * m, 2 * m + 1):
                sl = slice(h * LANES, (h + 1) * LANES)
                p, inv = _masked_softmax(_dot_nt(q_ref[0, :, sl], k_ref[0, :n, sl]), vis)
                o = _dot(p.astype(BF16), v_ref[0, :n, sl]) * inv
                acc = o if acc is None else acc + o
            o_ref[0, :, m * LANES:(m + 1) * LANES] = acc

    _for_visible_prefix(run, tq, k_ref.shape[1], past)


def _mla_attn(q, k, v, past, true_len):
    b, s, _ = q.shape
    lp = k.shape[1]
    tq = _tile(s, 256)
    kv = pl.BlockSpec((1, lp, A_HEADS * LANES), lambda i, j: (i, 0, 0))
    return pl.pallas_call(
        functools.partial(_mla_attn_kernel, past=past, true_len=true_len),
        grid=(b, s // tq),
        in_specs=[pl.BlockSpec((1, tq, A_HEADS * LANES), lambda i, j: (i, j, 0)), kv, kv],
        out_specs=pl.BlockSpec((1, tq, A_HEADS * A_D_V), lambda i, j: (i, j, 0)),
        out_shape=jax.ShapeDtypeStruct((b, s, A_HEADS * A_D_V), F32),
        compiler_params=_cparams("parallel", "arbitrary"),
        name="mla_attn",
    )(q, k, v)


def _rwkv_prep_kernel(pb_ref, shift_ref, mu_ref, w0_ref, w2_ref, a0_ref, a2_ref, g2_ref, kk_ref, ka_ref, rk_ref,
                      bd_ref, r_ref, w_ref, k_ref, v_ref, an_ref, bv_ref, g_ref, bonus_ref, last_ref):
    pb = pb_ref[0]
    ts = pb.shape[0]

    @pl.when(pl.program_id(1) == 0)
    def _():
        last_ref[...] = shift_ref[0]

    first = lax.broadcasted_iota(I32, (ts, 1), 0) == 0
    prev = jnp.where(first, last_ref[...], pltpu.roll(pb, 1, axis=0))
    last_ref[...] = pb[ts - 1:ts, :]
    xs = pb + (prev - pb) * mu_ref[...]
    r = xs[:, 0:B_WIDTH]
    k = xs[:, B_WIDTH:2 * B_WIDTH]
    v = xs[:, 2 * B_WIDTH:3 * B_WIDTH]
    wa = xs[:, 3 * B_WIDTH:3 * B_WIDTH + B_W_LORA + B_A_LORA]
    gl = xs[:, 3 * B_WIDTH + B_W_LORA + B_A_LORA:]
    bd = bd_ref[...]
    z = -(w0_ref[...] + _dot(jnp.tanh(wa).astype(BF16), w2_ref[...]))
    softplus = jnp.maximum(z, 0.0) + jnp.log(1.0 + jnp.exp(-jnp.abs(z)))
    w_ref[0] = -jnp.exp(-softplus - 0.5)
    a = jax.nn.sigmoid(a0_ref[...] + _dot(wa.astype(BF16), a2_ref[...]))
    g_ref[0] = _dot(jax.nn.sigmoid(gl).astype(BF16), g2_ref[...])
    kk = k * kk_ref[...]
    kk = kk * lax.rsqrt(_split_dot(kk * kk, bd) + 1e-12)
    k2 = k * (1.0 + (a - 1.0) * ka_ref[...])
    r_ref[0] = r
    k_ref[0] = k2
    v_ref[0] = v
    an_ref[0] = -kk
    bv_ref[0] = kk * a
    bonus_ref[0] = _split_dot(r * k2 * rk_ref[...], bd) * v


def _rwkv_prep(pb3, shift, w):
    b, s, _ = pb3.shape
    ts = _tile(s, 256)
    blk = lambda n: pl.BlockSpec((1, ts, n), lambda i, j: (i, j, 0))
    vec = _full((1, B_WIDTH))
    return pl.pallas_call(
        _rwkv_prep_kernel,
        grid=(b, s // ts),
        in_specs=[blk(B_PROJ), pl.BlockSpec((1, 1, B_PROJ), lambda i, j: (i, 0, 0)), _full((1, B_PROJ)), vec,
                  _full((LANES, B_WIDTH)), vec, _full((LANES, B_WIDTH)), _full((B_G_LORA, B_WIDTH)), vec, vec, vec,
                  _full((B_WIDTH, B_WIDTH))],
        out_specs=[blk(B_WIDTH)] * 8,
        out_shape=[jax.ShapeDtypeStruct((b, s, B_WIDTH), F32)] * 8,
        scratch_shapes=[pltpu.VMEM((1, B_PROJ), F32)],
        compiler_params=_cparams("parallel", "arbitrary"),
        name="rwkv_prep",
    )(pb3, shift[:, None, :], w["b_mu"], w["b_w0"], w["b_w2"], w["b_a0"], w["b_a2"], w["b_g2"], w["b_kk"], w["b_ka"],
      w["b_rk"], w["bd512"])


def _rwkv_scan_kernel(r_ref, lw_ref, k_ref, v_ref, an_ref, bv_ref, s0_ref, y_ref, sout_ref, st_ref):
    nb, c, _ = r_ref.shape
    pairs = B_HEADS // 2

    @pl.when(pl.program_id(1) == 0)
    def _():
        st_ref[...] = s0_ref[...]

    bf = lambda x: x.astype(BF16)
    step = lax.broadcasted_iota(I32, (c, c), 0)
    prev = lax.broadcasted_iota(I32, (c, c), 1)
    lower = prev <= step
    strict = prev < step
    tril = bf(jnp.where(lower, 1.0, 0.0))
    head_of_lane = lax.broadcasted_iota(I32, (1, LANES), 1) >> 6
    same_head = (lax.broadcasted_iota(I32, (LANES, LANES), 0) >> 6) == (lax.broadcasted_iota(I32, (LANES, LANES), 1) >> 6)
    levels = max(1, (c - 1).bit_length())
    pair_ids = [(b, p) for b in range(nb) for p in range(pairs)]
    head_ids = [(q, hh) for q in range(len(pair_ids)) for hh in range(2)]
    load = lambda ref: [ref[b, :, p * LANES:(p + 1) * LANES] for b, p in pair_ids]
    r, lw, k, v, an, bv = (load(ref) for ref in (r_ref, lw_ref, k_ref, v_ref, an_ref, bv_ref))
    pieces = []
    rem = lw
    for _ in range(3):
        pieces.append([bf(x) for x in rem])
        rem = [x - pc.astype(F32) for x, pc in zip(rem, pieces[-1])]
    logp = [_dot(tril, pc) for pc in pieces[0]]
    for level in pieces[1:]:
        logp = [acc + _dot(tril, pc) for acc, pc in zip(logp, level)]
    p_inc = [jnp.exp(x) for x in logp]
    p_inv = [jnp.exp(-x) for x in logp]
    a_t = [a * jnp.exp(lp - l) for a, lp, l in zip(an, logp, lw)]
    b_t = [bf(x * pi) for x, pi in zip(bv, p_inv)]
    k_t = [bf(x * pi) for x, pi in zip(k, p_inv)]
    r_t = [x * pi for x, pi in zip(r, p_inc)]
    s2 = [st_ref[b, p] for b, p in pair_ids]
    s2b = [bf(x) for x in s2]
    a_s = [_dot_nt(bf(x), s) for x, s in zip(a_t, s2b)]
    r_s = [_dot_nt(bf(x), s) for x, s in zip(r_t, s2b)]
    mine = [head_of_lane == hh for _, hh in head_ids]
    only = lambda xs: [bf(jnp.where(m, xs[q], 0.0)) for (q, _), m in zip(head_ids, mine)]
    a_h, r_h, v_h = only(a_t), only(r_t), only(v)
    n_pow = [bf(jnp.where(strict, _dot_nt(a, b_t[q]), 0.0)) for a, (q, _) in zip(a_h, head_ids)]
    l_ak = [bf(jnp.where(strict, _dot_nt(a, k_t[q]), 0.0)) for a, (q, _) in zip(a_h, head_ids)]
    m_rb = [bf(jnp.where(lower, _dot_nt(x, b_t[q]), 0.0)) for x, (q, _) in zip(r_h, head_ids)]
    m_rk = [bf(jnp.where(lower, _dot_nt(x, k_t[q]), 0.0)) for x, (q, _) in zip(r_h, head_ids)]
    x = [jnp.where(m, a_s[q], 0.0) + _dot(l, vh) for (q, _), m, l, vh in zip(head_ids, mine, l_ak, v_h)]
    for lv in range(levels):
        x = [xi + _dot(n, bf(xi)) for xi, n in zip(x, n_pow)]
        if lv + 1 < levels:
            n_pow = [bf(_dot(n, n)) for n in n_pow]
    y_h = [jnp.where(m, r_s[q], 0.0) + _dot(mb, bf(xi)) + _dot(mk, vh)
           for (q, _), m, mb, mk, xi, vh in zip(head_ids, mine, m_rb, m_rk, x, v_h)]
    for q, (b, p) in enumerate(pair_ids):
        y_ref[b, :, p * LANES:(p + 1) * LANES] = y_h[2 * q] + y_h[2 * q + 1]
        uv_t = bf(jnp.concatenate([x[2 * q] + x[2 * q + 1], v[q]], axis=0).T)
        add = _dot(uv_t, jnp.concatenate([b_t[q], k_t[q]], axis=0))
        st_ref[b, p] = (s2[q] + jnp.where(same_head, add, 0.0)) * p_inc[q][c - 1:c, :]

    @pl.when(pl.program_id(1) == pl.num_programs(1) - 1)
    def _():
        sout_ref[...] = st_ref[...]


def _rwkv_scan(seqs, s0):
    b, s, _ = seqs[0].shape
    nb = 2 if b % 2 == 0 else 1
    c = _tile(s, RWKV_CHUNK)
    pairs = B_HEADS // 2
    seq = pl.BlockSpec((nb, c, B_WIDTH), lambda i, j: (i, j, 0))
    sts = pl.BlockSpec((nb, pairs, LANES, LANES), lambda i, j: (i, 0, 0, 0))
    return pl.pallas_call(
        _rwkv_scan_kernel,
        grid=(b // nb, s // c),
        in_specs=[seq] * 6 + [sts],
        out_specs=[seq, sts],
        out_shape=[jax.ShapeDtypeStruct((b, s, B_WIDTH), F32), jax.ShapeDtypeStruct((b, pairs, LANES, LANES), F32)],
        scratch_shapes=[pltpu.VMEM((nb, pairs, LANES, LANES), F32)],
        compiler_params=_cparams("parallel", "arbitrary"),
        name="rwkv_scan",
    )(*seqs, s0)


def _rwkv_post_kernel(y_ref, bonus_ref, g_ref, lg_ref, lb_ref, bd_ref, o_ref):
    bd = bd_ref[...]
    y = y_ref[...]
    d = y - _split_dot(y, bd) * (1.0 / B_HD)
    var = _split_dot(d * d, bd) * (1.0 / B_HD)
    yn = d * lax.rsqrt(var + B_LN_EPS) * lg_ref[...] + lb_ref[...]
    o_ref[...] = (yn + bonus_ref[...]) * g_ref[...]


def _rwkv_post(y2, bonus2, g2, w):
    t = y2.shape[0]
    tm = _tile(t, 512)
    row = pl.BlockSpec((tm, B_WIDTH), lambda i: (i, 0))
    vec = _full((1, B_WIDTH))
    return pl.pallas_call(
        _rwkv_post_kernel,
        grid=(t // tm,),
        in_specs=[row, row, row, vec, vec, _full((B_WIDTH, B_WIDTH))],
        out_specs=row,
        out_shape=jax.ShapeDtypeStruct((t, B_WIDTH), F32),
        compiler_params=_cparams("parallel"),
        name="rwkv_post",
    )(y2, bonus2, g2, w["b_ln_g"], w["b_ln_b"], w["bd512"])


def _out_proj_kernel(x_ref, o1_ref, o2_ref, w1_ref, w2_ref, gf_ref, wq_ref, xo_ref, xn_ref, q_ref):
    x = x_ref[...] + _dot(o1_ref[...].astype(BF16), w1_ref[...]) + _dot(o2_ref[...].astype(BF16), w2_ref[...])
    xo_ref[...] = x
    xn = (_rms(x) * gf_ref[...]).astype(BF16)
    xn_ref[...] = xn
    q_ref[...] = _dot(xn, wq_ref[...])


def _out_proj(x2, o1, o2, w1, w2, gf, wq):
    t = x2.shape[0]
    tm = _tile(t, 512)
    row = lambda n: pl.BlockSpec((tm, n), lambda i: (i, 0))
    half = o1.shape[1]
    return pl.pallas_call(
        _out_proj_kernel,
        grid=(t // tm,),
        in_specs=[row(D_MODEL), row(half), row(half), _full((half, D_MODEL)), _full((half, D_MODEL)),
                  _full((1, D_MODEL)), _full((D_MODEL, PEER_HEADS * PEER_DK))],
        out_specs=[row(D_MODEL), row(D_MODEL), row(PEER_HEADS * PEER_DK)],
        out_shape=[jax.ShapeDtypeStruct((t, D_MODEL), F32), jax.ShapeDtypeStruct((t, D_MODEL), BF16),
                   jax.ShapeDtypeStruct((t, PEER_HEADS * PEER_DK), F32)],
        compiler_params=_cparams("parallel"),
        name="out_proj",
    )(x2, o1, o2, w1, w2, gf, wq)


def _top16_rows(sc_ref, val_ref, idx_ref):
    nc, n, tb = sc_ref.shape
    rowi = lax.broadcasted_iota(I32, (n, tb), 0).astype(F32)

    def body(r, carry):
        for c in range(nc):
            sc = sc_ref[c]
            m = jnp.max(sc, axis=0, keepdims=True)
            ix = jnp.min(jnp.where(sc == m, rowi, float(n)), axis=0, keepdims=True)
            val_ref[c, pl.ds(r, 1), :] = m
            idx_ref[c, pl.ds(r, 1), :] = ix
            sc_ref[c] = jnp.where(rowi == ix, -jnp.inf, sc)
        return carry

    lax.fori_loop(0, PEER_TOPK, body, 0)


def _pair_candidates(s1, s2, x1, x2):
    subi = lax.broadcasted_iota(I32, (SUBLANES, s1.shape[1]), 0)
    sub = subi.astype(F32)
    low4 = subi < 4
    b4 = (subi & 3).astype(F32)
    two = lambda z, a: jnp.where(low4, z[a:a + 1], z[a + 1:a + 2])
    dup4 = lambda z: jnp.where(low4, z[0:8], pltpu.roll(z[0:8], 4, axis=0))
    cand = [s1[0:1] + s2[0:8], s1[0:1] + s2[8:16]]
    flat = [sub, 8 + sub]
    eid = [x1[0:1] * PEER_N_KEYS + x2[0:8], x1[0:1] * PEER_N_KEYS + x2[8:16]]
    for a in (1, 2, 3):
        cand.append(s1[a:a + 1] + s2[0:8])
        flat.append(a * PEER_TOPK + sub)
        eid.append(x1[a:a + 1] * PEER_N_KEYS + x2[0:8])
    for a in (4, 6):
        cand.append(two(s1, a) + dup4(s2))
        flat.append(jnp.where(low4, float(a * PEER_TOPK), float((a + 1) * PEER_TOPK)) + b4)
        eid.append(two(x1, a) * PEER_N_KEYS + dup4(x2))
    cand.append(s1[8:16] + s2[0:1])
    flat.append((8 + sub) * PEER_TOPK)
    eid.append(x1[8:16] * PEER_N_KEYS + x2[0:1])
    return jnp.concatenate(cand, axis=0), jnp.concatenate(flat, axis=0), jnp.concatenate(eid, axis=0)


def _peer_route_kernel(q_ref, keys_ref, e1_ref, e2_ref, gate_ref, sc_ref, topv, topi, cand_ref, eid_ref, selv, sele):
    for h in range(PEER_HEADS):
        qh = q_ref[:, h * PEER_DK:(h + 1) * PEER_DK].astype(BF16)
        sc_ref[2 * h] = _dot_nt(keys_ref[h, 0], qh)
        sc_ref[2 * h + 1] = _dot_nt(keys_ref[h, 1], qh)
    _top16_rows(sc_ref, topv, topi)
    for h in range(PEER_HEADS):
        cand, flat, eid = _pair_candidates(topv[2 * h], topv[2 * h + 1], topi[2 * h], topi[2 * h + 1])
        cand_ref[h] = cand
        eid_ref[h] = eid

    def body(r, carry):
        for h in range(PEER_HEADS):
            cand = cand_ref[h]
            m = jnp.max(cand, axis=0, keepdims=True)
            f = jnp.min(jnp.where(cand == m, flat, 1e9), axis=0, keepdims=True)
            hit = flat == f
            selv[h, pl.ds(r, 1), :] = m
            sele[h, pl.ds(r, 1), :] = jnp.max(jnp.where(hit, eid_ref[h], -1.0), axis=0, keepdims=True)
            cand_ref[h] = jnp.where(hit, -jnp.inf, cand)
        return carry

    lax.fori_loop(0, PEER_TOPK, body, 0)
    for h in range(PEER_HEADS):
        rows = slice(h * PEER_TOPK, (h + 1) * PEER_TOPK)
        top = selv[h]
        e = jnp.exp(top - top[0:1])
        gate_ref[rows, :] = e * (1.0 / jnp.sum(e, axis=0, keepdims=True))
        ex = sele[h].astype(I32)
        e1_ref[rows, :] = ex >> 7
        e2_ref[rows, :] = ex & (PEER_N_KEYS - 1)


def _peer_route(q2, keys_pad):
    t = q2.shape[0]
    tb = _tile(t, 256)
    nk = PEER_HEADS * PEER_TOPK
    out = pl.BlockSpec((nk, tb), lambda i: (0, i))
    return pl.pallas_call(
        _peer_route_kernel,
        grid=(t // tb,),
        in_specs=[pl.BlockSpec((tb, PEER_HEADS * PEER_DK), lambda i: (i, 0)),
                  _full((PEER_HEADS, 2, PEER_N_KEYS, PEER_DK))],
        out_specs=[out, out, out],
        out_shape=[jax.ShapeDtypeStruct((nk, t), I32)] * 2 + [jax.ShapeDtypeStruct((nk, t), F32)],
        scratch_shapes=[pltpu.VMEM((2 * PEER_HEADS, PEER_N_KEYS, tb), F32),
                        pltpu.VMEM((2 * PEER_HEADS, PEER_TOPK, tb), F32), pltpu.VMEM((2 * PEER_HEADS, PEER_TOPK, tb), F32),
                        pltpu.VMEM((PEER_HEADS, N_PAIR_CAND, tb), F32), pltpu.VMEM((PEER_HEADS, N_PAIR_CAND, tb), F32),
                        pltpu.VMEM((PEER_HEADS, PEER_TOPK, tb), F32), pltpu.VMEM((PEER_HEADS, PEER_TOPK, tb), F32)],
        compiler_params=_cparams("parallel"),
        name="peer_route",
    )(q2, keys_pad)


def _peer_gate_kernel(e1_ref, e2_ref, gate_ref, o_ref, scr):
    tg = e1_ref.shape[0]
    nk = PEER_HEADS * PEER_TOPK
    sub = lax.broadcasted_iota(I32, (PEER_N_KEYS, nk), 0)

    def body(c, carry):
        t0 = pl.multiple_of(c * SUBLANES, SUBLANES)
        e1 = e1_ref[pl.ds(t0, SUBLANES), :]
        e2 = e2_ref[pl.ds(t0, SUBLANES), :]
        g = gate_ref[pl.ds(t0, SUBLANES), :]
        mats = []
        for j in range(SUBLANES):
            a_t = jnp.where(sub == e1[j:j + 1], g[j:j + 1], 0.0).astype(BF16)
            b_t = jnp.where(sub == e2[j:j + 1], 1.0, 0.0).astype(BF16)
            mats.append(_dot_nt(a_t, b_t))
        scr[:, pl.ds(t0, SUBLANES), :] = jnp.swapaxes(jnp.stack(mats, axis=0), 0, 1)
        return carry

    lax.fori_loop(0, tg // SUBLANES, body, 0, unroll=4)
    o_ref[...] = scr[...].astype(BF16)


def _peer_gates(e1, e2, gate):
    t = e1.shape[0]
    tg = _tile(t, 128)
    row = pl.BlockSpec((tg, PEER_HEADS * PEER_TOPK), lambda i: (i, 0))
    return pl.pallas_call(
        _peer_gate_kernel,
        grid=(t // tg,),
        in_specs=[row, row, row],
        out_specs=pl.BlockSpec((PEER_N_KEYS, tg, PEER_N_KEYS), lambda i: (0, i, 0)),
        out_shape=jax.ShapeDtypeStruct((PEER_N_KEYS, t, PEER_N_KEYS), BF16),
        scratch_shapes=[pltpu.VMEM((PEER_N_KEYS, tg, PEER_N_KEYS), F32)],
        compiler_params=_cparams("parallel"),
        name="peer_gates",
    )(e1, e2, gate)


def _gelu(x):
    return 0.5 * x * (1.0 + lax.erf(x * (2.0 ** -0.5)))


def _peer_dense_kernel(xn_ref, ut_ref, v_ref, g_ref, x_ref, o_ref):
    @pl.when(pl.program_id(1) == 0)
    def _():
        o_ref[...] = x_ref[...]

    h = _gelu(_dot(xn_ref[...], ut_ref[...]))
    nblk = g_ref.shape[0]
    gh = [(h[:, c * LANES:(c + 1) * LANES] * g_ref[c].astype(F32)).astype(BF16) for c in range(nblk)]
    o_ref[...] += _dot(jnp.concatenate(gh, axis=-1), v_ref[...])


def _peer_dense(xn, ut, vtab, gates, x2):
    t = xn.shape[0]
    tb = _tile(t, 1024)
    eb = 1024
    return pl.pallas_call(
        _peer_dense_kernel,
        grid=(t // tb, PEER_N_EXPERTS // eb),
        in_specs=[pl.BlockSpec((tb, D_MODEL), lambda i, j: (i, 0)),
                  pl.BlockSpec((D_MODEL, eb), lambda i, j: (0, j)),
                  pl.BlockSpec((eb, D_MODEL), lambda i, j: (j, 0)),
                  pl.BlockSpec((eb // PEER_N_KEYS, tb, PEER_N_KEYS), lambda i, j: (j, i, 0)),
                  pl.BlockSpec((tb, D_MODEL), lambda i, j: (i, 0))],
        out_specs=pl.BlockSpec((tb, D_MODEL), lambda i, j: (i, 0)),
        out_shape=jax.ShapeDtypeStruct((t, D_MODEL), F32),
        compiler_params=_cparams("parallel", "arbitrary"),
        name="peer_dense",
    )(xn, ut, vtab, gates, x2)


def _peer(x2, xn, q2, pw):
    e1, e2, gate = _peer_route(q2, pw["keys"])
    gates = _peer_gates(e1.T, e2.T, gate.T)
    return _peer_dense(xn, pw["ut"], pw["v"], gates, x2)


def _in1_kernel(x_ref, g_ref, wq_ref, wk_ref, wv_ref, wiq_ref, wik_ref, wiw_ref, wd_ref,
                q_ref, k_ref, v_ref, iq_ref, ik_ref, iw_ref, d_ref):
    xn = (_rms(x_ref[...]) * g_ref[...]).astype(BF16)
    for w_ref, o_ref in ((wq_ref, q_ref), (wk_ref, k_ref), (wv_ref, v_ref), (wiq_ref, iq_ref),
                         (wik_ref, ik_ref), (wiw_ref, iw_ref), (wd_ref, d_ref)):
        o_ref[...] = _dot(xn, w_ref[...])


def _in_proj1(x2, w):
    t = x2.shape[0]
    tm = _tile(t, 512)
    widths = [C_HEADS * C_HD, LANES, LANES, C_IDX_HEADS * C_IDX_D, LANES, LANES, 2 * D_WIDTH]
    row = lambda n: pl.BlockSpec((tm, n), lambda i: (i, 0))
    return pl.pallas_call(
        _in1_kernel,
        grid=(t // tm,),
        in_specs=[row(D_MODEL), _full((1, D_MODEL))] + [_full((D_MODEL, n)) for n in widths],
        out_specs=[row(n) for n in widths],
        out_shape=[jax.ShapeDtypeStruct((t, n), F32) for n in widths],
        compiler_params=_cparams("parallel"),
        name="in_proj1",
    )(x2, w["l1_norm"], w["w_cq1"], w["w_ck1"], w["w_cv1"], w["w_ciq"], w["w_cik"], w["w_ciw"], w["w_d"])


def _dsa_prep_kernel(q_ref, k_ref, iq_ref, ik_ref, qg_ref, kg_ref, ig_ref, cos_ref, sin_ref, bd_ref, perm_ref,
                     qo_ref, ko_ref, iqo_ref, iko_ref):
    cos, sin, bd, perm = cos_ref[...], sin_ref[...], bd_ref[...], perm_ref[...]

    def norm(x, gain, n):
        ss = _split_dot(x * x, bd[:n, :n]) * (1.0 / C_HD)
        return x * lax.rsqrt(ss + NORM_EPS) * gain

    def rope(x, n):
        return x * cos[:, :n] + _split_dot(x, perm[:n, :n]) * sin[:, :n]

    nq = C_HEADS * C_HD
    qo_ref[0] = (rope(norm(q_ref[0], qg_ref[...], nq), nq) * (C_HD ** -0.5)).astype(BF16)
    ko_ref[0] = rope(norm(k_ref[0], kg_ref[...], LANES), LANES)
    iqo_ref[0] = (rope(iq_ref[0], nq) * (C_IDX_D ** -0.5)).astype(BF16)
    iko_ref[0] = rope(norm(ik_ref[0], ig_ref[...], LANES), LANES)


def _dsa_prep(q, k, iq, ik, w, cos, sin):
    b, s, _ = q.shape
    ts = _tile(s, 256)
    nq = C_HEADS * C_HD
    blk = lambda n: pl.BlockSpec((1, ts, n), lambda i, j: (i, j, 0))
    tab = pl.BlockSpec((ts, nq), lambda i, j: (j, 0))
    return pl.pallas_call(
        _dsa_prep_kernel,
        grid=(b, s // ts),
        in_specs=[blk(nq), blk(LANES), blk(nq), blk(LANES), _full((1, nq)), _full((1, LANES)), _full((1, LANES)),
                  tab, tab, _full((nq, nq)), _full((nq, nq))],
        out_specs=[blk(nq), blk(LANES), blk(nq), blk(LANES)],
        out_shape=[jax.ShapeDtypeStruct((b, s, nq), BF16), jax.ShapeDtypeStruct((b, s, LANES), F32),
                   jax.ShapeDtypeStruct((b, s, nq), BF16), jax.ShapeDtypeStruct((b, s, LANES), F32)],
        compiler_params=_cparams("parallel", "parallel"),
        name="dsa_prep",
    )(q, k, iq, ik, w["c_q_gain"], w["c_k_gain"], w["c_kidx_gain"], cos, sin, w["bd512"], w["perm_c"])


KEY_OF_NEG_INF = -(2 ** 31) + 0x7FFFFF


def _key_to_float(key):
    return pltpu.bitcast(jnp.where(key < 0, key ^ jnp.int32(0x7FFFFFFF), key), F32)


def _topk_mask(score, n_sel, su):
    tq, lp = score.shape

    def body(i, tau):
        trial = tau + lax.shift_left(jnp.int32(1), 31 - i)
        cnt = jnp.sum(jnp.where(score >= _key_to_float(trial), 1.0, 0.0), axis=-1, keepdims=True)
        return jnp.where(trial <= KEY_OF_NEG_INF, trial, jnp.where(cnt >= n_sel, trial, tau))

    kth = _key_to_float(lax.fori_loop(0, 32, body, jnp.full((tq, 1), -(2 ** 31), I32)))
    gt = jnp.where(score > kth, 1.0, 0.0)
    eq = jnp.where(score == kth, 1.0, 0.0)
    need = n_sel - jnp.sum(gt, axis=-1, keepdims=True)
    parts = []
    before = jnp.zeros((tq, 1), F32)
    for c in range(lp // LANES):
        eqc = eq[:, c * LANES:(c + 1) * LANES]
        rank = before + _dot(eqc.astype(BF16), su)
        parts.append(gt[:, c * LANES:(c + 1) * LANES] + jnp.where(rank < need, eqc, 0.0))
        before = before + jnp.sum(eqc, axis=-1, keepdims=True)
    return jnp.concatenate(parts, axis=-1)


def _dsa_attn_kernel(q_ref, iq_ref, iw_ref, ik_ref, k_ref, v_ref, su_ref, o_ref, *, past, true_len, n_sel):
    tq = q_ref.shape[1]
    low = lax.broadcasted_iota(I32, (1, LANES), 1) < C_HD
    halves = lambda x: (jnp.where(low, x, jnp.zeros((), x.dtype)), jnp.where(low, jnp.zeros((), x.dtype), x))

    def run(n, q0):
        vis = _visible(tq, n, q0, true_len)
        iw = iw_ref[0] * (C_IDX_HEADS ** -0.5)
        ik2 = ik_ref[0, :n]
        score = jnp.zeros((tq, n), F32)
        for m in range(C_IDX_HEADS // 2):
            iqa, iqb = halves(iq_ref[0, :, m * LANES:(m + 1) * LANES])
            score = score + jnp.maximum(_dot_nt(iqa, ik2), 0.0) * iw[:, 2 * m:2 * m + 1]
            score = score + jnp.maximum(_dot_nt(iqb, ik2), 0.0) * iw[:, 2 * m + 1:2 * m + 2]
        score = jnp.where(vis, score, NEG_INF)
        keep = jnp.where(vis, _topk_mask(score, n_sel, su_ref[...]), 0.0) > 0.5
        k = k_ref[0, :n]
        va, vb = halves(v_ref[0, :n])
        for m in range(C_HEADS // 2):
            qa, qb = halves(q_ref[0, :, m * LANES:(m + 1) * LANES])
            pa, inva = _masked_softmax(_dot_nt(qa, k), keep)
            pb, invb = _masked_softmax(_dot_nt(qb, k), keep)
            o_ref[0, :, m * LANES:(m + 1) * LANES] = (_dot(pa.astype(BF16), va) * inva
                                                      + _dot(pb.astype(BF16), vb) * invb)

    _for_visible_prefix(run, tq, ik_ref.shape[1], past)


def _dsa_attn(q, iq, iw, ik2, k, v, su, past, true_len):
    b, s, nq = q.shape
    lp = ik2.shape[1]
    tq = _tile(s, 256)
    n_sel = min(C_TOPK, true_len // 4)
    qblk = lambda n: pl.BlockSpec((1, tq, n), lambda i, j: (i, j, 0))
    keys = pl.BlockSpec((1, lp, LANES), lambda i, j: (i, 0, 0))
    return pl.pallas_call(
        functools.partial(_dsa_attn_kernel, past=past, true_len=true_len, n_sel=n_sel),
        grid=(b, s // tq),
        in_specs=[qblk(nq), qblk(nq), qblk(LANES), keys, keys, keys, _full((LANES, LANES))],
        out_specs=qblk(nq),
        out_shape=jax.ShapeDtypeStruct((b, s, nq), F32),
        compiler_params=_cparams("parallel", "arbitrary"),
        name="dsa_attn",
    )(q, iq, iw, ik2, k, v, su)


def _gmlp_kernel(pd_ref, lg_ref, lb_ref, ws_ref, bs_ref, o_ref, dv_ref):
    span = pd_ref.shape[1]
    h = _gelu(pd_ref[0])
    u = h[:, :D_WIDTH]
    v = h[:, D_WIDTH:]
    mu = jnp.mean(v, axis=-1, keepdims=True)
    d = v - mu
    v = d * lax.rsqrt(jnp.mean(d * d, axis=-1, keepdims=True) + NORM_EPS) * lg_ref[...] + lb_ref[...]
    dv_ref[0] = v
    causal = lax.broadcasted_iota(I32, (span, span), 1) <= lax.broadcasted_iota(I32, (span, span), 0)
    gw = D_WIDTH // D_GROUPS
    mixed = [_dot(jnp.where(causal, ws_ref[g], 0.0).astype(BF16), v[:, g * gw:(g + 1) * gw].astype(BF16))
             for g in range(D_GROUPS)]
    o_ref[0] = u * (jnp.concatenate(mixed, axis=-1) + bs_ref[...])


def _gmlp(pd, w):
    b, s, _ = pd.shape
    span = D_SPAN if s % D_SPAN == 0 else s
    ws = w["d_ws"][:, :span, :span]
    bs = jnp.repeat(w["d_bs"][:, :span].T, D_WIDTH // D_GROUPS, axis=1)
    blk = lambda n: pl.BlockSpec((1, span, n), lambda i, j: (i, j, 0))
    return pl.pallas_call(
        _gmlp_kernel,
        grid=(b, s // span),
        in_specs=[blk(2 * D_WIDTH), _full((1, D_WIDTH)), _full((1, D_WIDTH)), _full((D_GROUPS, span, span)),
                  _full((span, D_WIDTH))],
        out_specs=[blk(D_WIDTH), blk(D_WIDTH)],
        out_shape=[jax.ShapeDtypeStruct((b, s, D_WIDTH), F32)] * 2,
        compiler_params=_cparams("parallel", "parallel"),
        name="gmlp",
    )(pd, w["d_ln_g"], w["d_ln_b"], ws, bs)


def _pad_cols(w, groups, width, slot, off=0):
    k = w.shape[0]
    w = jnp.pad(w.reshape(k, groups, width), ((0, 0), (0, 0), (off, slot - off - width)))
    return w.reshape(k, groups * slot)


def _prep_weights(p):
    row = lambda v: v.reshape(1, -1).astype(F32)
    bf = lambda v: v.astype(BF16)
    w = {}
    w_in0 = p["l0_w_in"]
    w["l0_norm"] = row(p["l0_norm_mix"])
    w["w_cq"] = bf(w_in0[:, :A_D_CQ])
    w["w_ckv"] = bf(w_in0[:, A_D_CQ:A_D_CQ + A_D_C])
    a_cols = A_D_CQ + A_D_C + A_D_ROPE
    w["w_krp"] = bf(_pad_cols(w_in0[:, A_D_CQ + A_D_C:a_cols], 1, A_D_ROPE, LANES, A_D_NOPE))
    w["w_b"] = bf(w_in0[:, a_cols:])
    w["a_kv_norm"] = row(p["a_kv_norm"])
    w["a_q_norm"] = row(p["a_q_norm"])
    w["w_uq"] = bf(_pad_cols(p["a_w_uq"], A_HEADS, A_D_QK, LANES))
    ukv = p["a_w_ukv"].reshape(A_D_C, A_HEADS, A_D_NOPE + A_D_V)
    w["w_uk"] = bf(_pad_cols(ukv[:, :, :A_D_NOPE].reshape(A_D_C, -1), A_HEADS, A_D_NOPE, LANES))
    uv = ukv[:, :, A_D_NOPE:].reshape(A_D_C, A_HEADS // 2, 2, A_D_V)
    slot = lambda x, off: jnp.pad(x, ((0, 0), (0, 0), (off, LANES - off - A_D_V)))
    uv_pad = jnp.stack([slot(uv[:, :, 0], 0), slot(uv[:, :, 1], A_D_V)], axis=2)
    w["w_uv"] = bf(uv_pad.reshape(A_D_C, A_HEADS * LANES))
    w["a_q_gain"] = jnp.pad(row(p["a_q_gain"]), ((0, 0), (0, LANES - A_D_QK)))
    w["a_k_gain"] = jnp.pad(row(p["a_k_gain"]), ((0, 0), (0, LANES - A_D_QK)))
    w["perm_a"] = _rope_perm(LANES, LANES, A_D_NOPE, A_D_ROPE // 2)
    w["b_mu"] = row(p["b_mu"])
    w["b_w0"] = row(p["b_w0"])
    w["b_a0"] = row(p["b_a0"])
    zeros = jnp.zeros((B_W_LORA, B_WIDTH), F32)
    w["b_w2"] = bf(jnp.concatenate([p["b_w2"], zeros], axis=0))
    w["b_a2"] = bf(jnp.concatenate([zeros, p["b_a2"]], axis=0))
    w["b_g2"] = bf(p["b_g2"])
    for n in ("b_kk", "b_ka", "b_rk", "b_ln_g", "b_ln_b"):
        w[n] = row(p[n])
    w["bd512"] = _block_ones(B_WIDTH, B_HD)
    w["w_out0a"] = bf(p["l0_w_out"][:A_HEADS * A_D_V])
    w["w_out0b"] = bf(p["l0_w_out"][A_HEADS * A_D_V:])
    w_in1 = p["l1_w_in"]
    nq = C_HEADS * C_HD
    nkv = C_KV_HEADS * C_HD
    o = 0
    order = jnp.arange(C_HEADS).reshape(C_KV_HEADS, C_HEADS // C_KV_HEADS).T.reshape(-1)
    w["w_cq1"] = bf(w_in1[:, o:o + nq].reshape(D_MODEL, C_HEADS, C_HD)[:, order].reshape(D_MODEL, nq)); o += nq
    w["w_ck1"] = bf(w_in1[:, o:o + nkv]); o += nkv
    w["w_cv1"] = bf(w_in1[:, o:o + nkv]); o += nkv
    w["w_ciq"] = bf(w_in1[:, o:o + nq]); o += nq
    w["w_cik"] = bf(_pad_cols(w_in1[:, o:o + C_IDX_D], 1, C_IDX_D, LANES)); o += C_IDX_D
    w["w_ciw"] = bf(_pad_cols(w_in1[:, o:o + C_IDX_HEADS], 1, C_IDX_HEADS, LANES)); o += C_IDX_HEADS
    w["w_d"] = bf(w_in1[:, o:])
    w["l1_norm"] = row(p["l1_norm_mix"])
    w["c_q_gain"] = row(jnp.tile(p["c_q_gain"], C_HEADS))
    w["c_k_gain"] = row(jnp.tile(p["c_k_gain"], C_KV_HEADS))
    w["c_kidx_gain"] = jnp.pad(row(p["c_kidx_gain"]), ((0, 0), (0, LANES - C_IDX_D)))
    w["perm_c"] = _rope_perm(nq, C_HD, 0, C_HD // 2)
    w["su"] = (jnp.arange(LANES)[:, None] < jnp.arange(LANES)[None, :]).astype(BF16)
    w["d_ln_g"] = row(p["d_ln_g"])
    w["d_ln_b"] = row(p["d_ln_b"])
    w["d_ws"] = p["d_ws"]
    w["d_bs"] = p["d_bs"]
    w["w_out1a"] = bf(p["l1_w_out"][:nq].reshape(C_HEADS, C_HD, D_MODEL)[order].reshape(nq, D_MODEL))
    w["w_out1b"] = bf(p["l1_w_out"][nq:])
    for l in (0, 1):
        keys = p[f"l{l}_peer_keys"]
        half = PEER_DK // 2
        edge = lambda lo, hi: ((0, 0), (0, 0), (lo, hi))
        kp = jnp.stack([jnp.pad(keys[:, 0], edge(0, half)), jnp.pad(keys[:, 1], edge(half, 0))], axis=1)
        w[f"peer{l}"] = {"keys": bf(kp), "ut": bf(p[f"l{l}_peer_u"]).T, "v": bf(p[f"l{l}_peer_v"]),
                         "norm": row(p[f"l{l}_norm_ffn"]), "wq": bf(p[f"l{l}_peer_wq"])}
    return w


def _pad_keys(x, lp):
    return jnp.pad(x, ((0, 0), (0, lp - x.shape[1])) + ((0, 0),) * (x.ndim - 2))


def _trunk(x, st, w):
    b, s, _ = x.shape
    t = b * s
    past = st["a_ckv"].shape[1]
    true_len = past + s
    lp = -(-true_len // LANES) * LANES
    kpos = jnp.arange(lp, dtype=I32)

    cq, ckv, krp, pb = _in_proj0(x.reshape(t, D_MODEL), w)
    cos_a, sin_a = _rope_tables(kpos, A_D_ROPE // 2, LANES, A_D_NOPE, 1)
    q = _mla_q(cq.reshape(b, s, A_D_CQ), w, cos_a[past:true_len], sin_a[past:true_len])
    ckv3 = ckv.reshape(b, s, A_D_C)
    krp3 = krp.reshape(b, s, LANES)
    krope = krp3[:, :, A_D_NOPE:A_D_NOPE + A_D_ROPE]
    cache_krp = jnp.pad(st["a_krope"], ((0, 0), (0, 0), (A_D_NOPE, LANES - A_D_NOPE - A_D_ROPE)))
    ckv_all = _pad_keys(jnp.concatenate([st["a_ckv"], ckv3], axis=1), lp)
    krp_all = _pad_keys(jnp.concatenate([cache_krp, krp3], axis=1), lp)
    k_a, v_a = _mla_kv(ckv_all, krp_all, w, cos_a, sin_a)
    o_a = _mla_attn(q, k_a, v_a, past, true_len)

    pb3 = pb.reshape(b, s, B_PROJ)
    r, logw, k2, v, an, bv, g, bonus = _rwkv_prep(pb3, st["b_shift"], w)
    pairs = B_HEADS // 2
    sp = st["b_wkv"].reshape(b, pairs, 2, B_HD, B_HD)
    edge = lambda lo, hi: ((0, 0), (0, 0), (0, 0), (lo, hi))
    s0 = jnp.concatenate([jnp.pad(sp[:, :, 0], edge(0, B_HD)), jnp.pad(sp[:, :, 1], edge(B_HD, 0))], axis=2)
    y, s_out = _rwkv_scan([r, logw, k2, v, an, bv], s0)
    wkv = jnp.stack([s_out[:, :, :B_HD, :B_HD], s_out[:, :, B_HD:, B_HD:]], axis=2).reshape(b, B_HEADS, B_HD, B_HD)
    o_b = _rwkv_post(y.reshape(t, B_WIDTH), bonus.reshape(t, B_WIDTH), g.reshape(t, B_WIDTH), w)
    pw = w["peer0"]
    x1, xn1, pq1 = _out_proj(x.reshape(t, D_MODEL), o_a.reshape(t, -1), o_b, w["w_out0a"], w["w_out0b"],
                             pw["norm"], pw["wq"])
    x2 = _peer(x1, xn1, pq1, pw)

    nq = C_HEADS * C_HD
    cq1, ck1, cv1, ciq, cik, ciw, pd = _in_proj1(x2, w)
    pos_q = kpos[past:true_len]
    cos_c, sin_c = _rope_tables(pos_q, C_HD // 2, C_HD, 0, C_HEADS)
    three = lambda a: a.reshape(b, s, a.shape[-1])
    q_c, k_c, iq_c, ik_c = _dsa_prep(three(cq1), three(ck1), three(ciq), three(cik), w, cos_c, sin_c)
    c_k = k_c.reshape(b, s, C_KV_HEADS, C_HD)
    c_v = cv1.reshape(b, s, C_KV_HEADS, C_HD)
    c_kidx = ik_c[:, :, :C_IDX_D]
    flat_kv = lambda cache, new: _pad_keys(
        jnp.concatenate([cache.reshape(b, past, LANES), new], axis=1), lp).astype(BF16)
    ik_all = jnp.concatenate([st["c_kidx"], c_kidx], axis=1)
    ik2 = _pad_keys(jnp.concatenate([ik_all, ik_all], axis=-1), lp).astype(BF16)
    o_c = _dsa_attn(q_c, iq_c, three(ciw), ik2, flat_kv(st["c_k"], k_c), flat_kv(st["c_v"], three(cv1)),
                    w["su"], past, true_len)
    o_d, d_v = _gmlp(three(pd), w)
    pw = w["peer1"]
    x3, xn3, pq3 = _out_proj(x2, o_c.reshape(t, nq), o_d.reshape(t, D_WIDTH), w["w_out1a"], w["w_out1b"],
                             pw["norm"], pw["wq"])
    y_out = _peer(x3, xn3, pq3, pw).reshape(b, s, D_MODEL)
    return y_out, [ckv3, krope, wkv, pb3[:, -1], c_k, c_v, c_kidx, d_v]


def kernel(x_prompt, x_sample, cache_a_ckv, cache_a_krope, state_b_wkv, state_b_shift, cache_c_k, cache_c_v, cache_c_kidx, l0_norm_mix, l0_w_in, a_q_norm, a_w_uq, a_kv_norm, a_w_ukv, a_q_gain, a_k_gain, b_mu, b_w0, b_w2, b_a0, b_a2, b_g2, b_kk, b_ka, b_rk, b_ln_g, b_ln_b, l0_w_out, l0_norm_ffn, l0_peer_wq, l0_peer_keys, l0_peer_u, l0_peer_v, l1_norm_mix, l1_w_in, c_q_gain, c_k_gain, c_kidx_gain, d_ln_g, d_ln_b, d_ws, d_bs, l1_w_out, l1_norm_ffn, l1_peer_wq, l1_peer_keys, l1_peer_u, l1_peer_v):
    params = dict(
        l0_norm_mix=l0_norm_mix, l0_w_in=l0_w_in, a_q_norm=a_q_norm, a_w_uq=a_w_uq, a_kv_norm=a_kv_norm,
        a_w_ukv=a_w_ukv, a_q_gain=a_q_gain, a_k_gain=a_k_gain, b_mu=b_mu, b_w0=b_w0, b_w2=b_w2, b_a0=b_a0,
        b_a2=b_a2, b_g2=b_g2, b_kk=b_kk, b_ka=b_ka, b_rk=b_rk.reshape(-1), b_ln_g=b_ln_g, b_ln_b=b_ln_b,
        l0_w_out=l0_w_out, l0_norm_ffn=l0_norm_ffn, l0_peer_wq=l0_peer_wq, l0_peer_keys=l0_peer_keys,
        l0_peer_u=l0_peer_u, l0_peer_v=l0_peer_v, l1_norm_mix=l1_norm_mix, l1_w_in=l1_w_in, c_q_gain=c_q_gain,
        c_k_gain=c_k_gain, c_kidx_gain=c_kidx_gain, d_ln_g=d_ln_g, d_ln_b=d_ln_b, d_ws=d_ws, d_bs=d_bs,
        l1_w_out=l1_w_out, l1_norm_ffn=l1_norm_ffn, l1_peer_wq=l1_peer_wq, l1_peer_keys=l1_peer_keys,
        l1_peer_u=l1_peer_u, l1_peer_v=l1_peer_v)
    w = _prep_weights(params)
    bp = x_prompt.shape[0]
    dt = x_prompt.dtype
    prompt_state = dict(
        a_ckv=jnp.zeros((bp, 0, A_D_C), dt), a_krope=jnp.zeros((bp, 0, A_D_ROPE), dt),
        b_wkv=jnp.zeros((bp, B_HEADS, B_HD, B_HD), dt), b_shift=jnp.zeros((bp, B_PROJ), dt),
        c_k=jnp.zeros((bp, 0, C_KV_HEADS, C_HD), dt), c_v=jnp.zeros((bp, 0, C_KV_HEADS, C_HD), dt),
        c_kidx=jnp.zeros((bp, 0, C_IDX_D), dt))
    sample_state = dict(a_ckv=cache_a_ckv, a_krope=cache_a_krope, b_wkv=state_b_wkv, b_shift=state_b_shift,
                        c_k=cache_c_k, c_v=cache_c_v, c_kidx=cache_c_kidx)
    y_p, new_p = _trunk(x_prompt, prompt_state, w)
    y_s, new_s = _trunk(x_sample, sample_state, w)
    return (y_p, y_s, *new_p[:7], *new_s)
```

```python
import functools
import math

import jax
import jax.numpy as jnp
from jax import lax
from jax.experimental import pallas as pl
from jax.experimental.pallas import tpu as pltpu

F32 = jnp.float32
BF16 = jnp.bfloat16
I32 = jnp.int32

D_MODEL = 1024
CHUNK = 64
CHUNK_SHIFT = 6
KEY_BLOCK = 512
ROPE_THETA = 10000.0
NORM_EPS = 1e-6
NEG_INF = -1e30
LANES = 128
SUBLANES = 8

A_HEADS = 8
A_D_NOPE = 64
A_D_ROPE = 32
A_D_QK = A_D_NOPE + A_D_ROPE
A_D_V = 64
A_D_CQ = 256
A_D_C = 128
B_HEADS = 8
B_HD = 64
B_WIDTH = B_HEADS * B_HD
B_W_LORA = 64
B_A_LORA = 64
B_G_LORA = 128
B_PROJ = 3 * B_WIDTH + B_W_LORA + B_A_LORA + B_G_LORA
B_LN_EPS = 64e-5
RWKV_CHUNK = 128
C_HEADS = 8
C_KV_HEADS = 2
C_HD = 64
C_IDX_HEADS = 8
C_IDX_D = 64
C_TOPK = 256
D_GROUPS = 4
D_WIDTH = 512
D_SPAN = 128
PEER_HEADS = 8
PEER_N_KEYS = 128
PEER_N_EXPERTS = PEER_N_KEYS * PEER_N_KEYS
PEER_DK = 128
PEER_TOPK = 16
N_PAIR_CAND = 64

VMEM_LIMIT = 48 * 1024 * 1024
NT = (((1,), (1,)), ((), ()))


def _tile(n, target):
    t = min(n, target)
    while n % t:
        t -= 8
    return t


def _cparams(*sem):
    return pltpu.CompilerParams(dimension_semantics=sem, vmem_limit_bytes=VMEM_LIMIT)


def _dot(a, b):
    return jnp.dot(a, b, preferred_element_type=F32)


def _dot_nt(a, b):
    return lax.dot_general(a, b, NT, preferred_element_type=F32)


def _split_dot(x, w, passes=2):
    hi = x.astype(BF16)
    acc = _dot(hi, w)
    rem = x - hi.astype(F32)
    for _ in range(passes - 1):
        part = rem.astype(BF16)
        acc = acc + _dot(part, w)
        rem = rem - part.astype(F32)
    return acc


def _rms(x, eps=NORM_EPS):
    return x * lax.rsqrt(jnp.mean(x * x, axis=-1, keepdims=True) + eps)


def _full(shape):
    nd = len(shape)
    return pl.BlockSpec(shape, lambda *_: (0,) * nd)


def _block_ones(n, seg):
    i = jnp.arange(n)
    return (i[:, None] // seg == i[None, :] // seg).astype(BF16)


def _rope_perm(n, seg, off, half):
    r = jnp.arange(n)[:, None]
    c = jnp.arange(n)[None, :]
    same = r // seg == c // seg
    rr = r % seg - off
    cc = c % seg - off
    plus = same & (rr >= 0) & (rr < half) & (cc == rr + half)
    minus = same & (cc >= 0) & (cc < half) & (rr == cc + half)
    return (plus.astype(F32) - minus.astype(F32)).astype(BF16)


def _rope_tables(pos, half, seg, off, reps):
    inv_freq = ROPE_THETA ** (-jnp.arange(half, dtype=F32) / half)
    ang = pos.astype(F32)[:, None] * inv_freq[None, :]
    n = pos.shape[0]
    edge = ((0, 0), (off, seg - off - 2 * half))
    cos = jnp.pad(jnp.tile(jnp.cos(ang), (1, 2)), edge, constant_values=1.0)
    sin = jnp.pad(jnp.tile(jnp.sin(ang), (1, 2)), edge)
    return jnp.tile(cos, (1, reps)), jnp.tile(sin, (1, reps))


def _in0_kernel(x_ref, g_ref, wcq_ref, wckv_ref, wkr_ref, wb_ref, kvn_ref,
                cq_ref, ckv_ref, krp_ref, pb_ref):
    xn = (_rms(x_ref[...]) * g_ref[...]).astype(BF16)
    cq_ref[...] = _dot(xn, wcq_ref[...])
    ckv_ref[...] = _rms(_dot(xn, wckv_ref[...])) * kvn_ref[...]
    krp_ref[...] = _dot(xn, wkr_ref[...])
    pb_ref[...] = _dot(xn, wb_ref[...])


def _in_proj0(x2, w):
    t = x2.shape[0]
    tm = _tile(t, 512)
    row = lambda n: pl.BlockSpec((tm, n), lambda i: (i, 0))
    return pl.pallas_call(
        _in0_kernel,
        grid=(t // tm,),
        in_specs=[row(D_MODEL), _full((1, D_MODEL)), _full((D_MODEL, A_D_CQ)), _full((D_MODEL, A_D_C)),
                  _full((D_MODEL, LANES)), _full((D_MODEL, B_PROJ)), _full((1, A_D_C))],
        out_specs=[row(A_D_CQ), row(A_D_C), row(LANES), row(B_PROJ)],
        out_shape=[jax.ShapeDtypeStruct((t, A_D_CQ), F32), jax.ShapeDtypeStruct((t, A_D_C), F32),
                   jax.ShapeDtypeStruct((t, LANES), F32), jax.ShapeDtypeStruct((t, B_PROJ), F32)],
        compiler_params=_cparams("parallel"),
        name="in_proj0",
    )(x2, w["l0_norm"], w["w_cq"], w["w_ckv"], w["w_krp"], w["w_b"], w["a_kv_norm"])


def _head_norm_rope(xh, gain, cos, sin, perm):
    ss = jnp.sum(xh * xh, axis=-1, keepdims=True) * (1.0 / A_D_QK)
    xh = xh * lax.rsqrt(ss + NORM_EPS) * gain
    return xh * cos + _split_dot(xh, perm) * sin


def _mla_q_kernel(cq_ref, qn_ref, wq_ref, gain_ref, cos_ref, sin_ref, perm_ref, o_ref):
    cqn = (_rms(cq_ref[0]) * qn_ref[...]).astype(BF16)
    q = _dot(cqn, wq_ref[...])
    cos, sin, perm, gain = cos_ref[...], sin_ref[...], perm_ref[...], gain_ref[...]
    for h in range(A_HEADS):
        qh = _head_norm_rope(q[:, h * LANES:(h + 1) * LANES], gain, cos, sin, perm)
        o_ref[0, :, h * LANES:(h + 1) * LANES] = (qh * (A_D_QK ** -0.5)).astype(BF16)


def _mla_q(cq, w, cos, sin):
    b, s, _ = cq.shape
    ts = _tile(s, 256)
    return pl.pallas_call(
        _mla_q_kernel,
        grid=(b, s // ts),
        in_specs=[pl.BlockSpec((1, ts, A_D_CQ), lambda i, j: (i, j, 0)), _full((1, A_D_CQ)),
                  _full((A_D_CQ, A_HEADS * LANES)), _full((1, LANES)),
                  pl.BlockSpec((ts, LANES), lambda i, j: (j, 0)), pl.BlockSpec((ts, LANES), lambda i, j: (j, 0)),
                  _full((LANES, LANES))],
        out_specs=pl.BlockSpec((1, ts, A_HEADS * LANES), lambda i, j: (i, j, 0)),
        out_shape=jax.ShapeDtypeStruct((b, s, A_HEADS * LANES), BF16),
        compiler_params=_cparams("parallel", "parallel"),
        name="mla_q",
    )(cq, w["a_q_norm"], w["w_uq"], w["a_q_gain"], cos, sin, w["perm_a"])


def _mla_kv_kernel(ckv_ref, krp_ref, wk_ref, wv_ref, gain_ref, cos_ref, sin_ref, perm_ref, k_ref, v_ref):
    c = ckv_ref[0].astype(BF16)
    k = _dot(c, wk_ref[...])
    v_ref[0] = _dot(c, wv_ref[...]).astype(BF16)
    krp = krp_ref[0]
    cos, sin, perm, gain = cos_ref[...], sin_ref[...], perm_ref[...], gain_ref[...]
    for h in range(A_HEADS):
        kh = _head_norm_rope(k[:, h * LANES:(h + 1) * LANES] + krp, gain, cos, sin, perm)
        k_ref[0, :, h * LANES:(h + 1) * LANES] = kh.astype(BF16)


def _mla_kv(ckv_all, krp_all, w, cos, sin):
    b, lp, _ = ckv_all.shape
    tl = _tile(lp, 256)
    blk = lambda n: pl.BlockSpec((1, tl, n), lambda i, j: (i, j, 0))
    tab = pl.BlockSpec((tl, LANES), lambda i, j: (j, 0))
    return pl.pallas_call(
        _mla_kv_kernel,
        grid=(b, lp // tl),
        in_specs=[blk(A_D_C), blk(LANES), _full((A_D_C, A_HEADS * LANES)), _full((A_D_C, A_HEADS * LANES)),
                  _full((1, LANES)), tab, tab, _full((LANES, LANES))],
        out_specs=[blk(A_HEADS * LANES), blk(A_HEADS * LANES)],
        out_shape=[jax.ShapeDtypeStruct((b, lp, A_HEADS * LANES), BF16)] * 2,
        compiler_params=_cparams("parallel", "parallel"),
        name="mla_kv",
    )(ckv_all, krp_all, w["w_uk"], w["w_uv"], w["a_k_gain"], cos, sin, w["perm_a"])


def _visible(tq, lp, q0, true_len):
    qpos = q0 + lax.broadcasted_iota(I32, (tq, 1), 0)
    kpos = lax.broadcasted_iota(I32, (1, lp), 1)
    kchunk = jnp.where(kpos < true_len, kpos >> CHUNK_SHIFT, jnp.int32(1 << 30))
    return kchunk <= (qpos >> CHUNK_SHIFT)


def _masked_softmax(s, keep):
    s = jnp.where(keep, s, NEG_INF)
    p = jnp.exp(s - jnp.max(s, axis=-1, keepdims=True))
    return p, 1.0 / jnp.sum(p, axis=-1, keepdims=True)


def _for_visible_prefix(run, tq, lp, past):
    q0 = past + pl.program_id(1) * tq
    need = (((q0 + tq - 1) >> CHUNK_SHIFT) + 1) << CHUNK_SHIFT
    sizes = [min(c * KEY_BLOCK, lp) for c in range(1, -(-lp // KEY_BLOCK) + 1)]
    case = (need + (KEY_BLOCK - 1)) // KEY_BLOCK
    for c, n in enumerate(sizes, start=1):
        pl.when(case == c)(functools.partial(run, n, q0))


def _mla_attn_kernel(q_ref, k_ref, v_ref, o_ref, *, past, true_len):
    tq = q_ref.shape[1]

    def run(n, q0):
        vis = _visible(tq, n, q0, true_len)
        for m in range(A_HEADS // 2):
            acc = None
            for h in (2 * m, 2 * m + 1):
                sl = slice(h * LANES, (h + 1) * LANES)
                p, inv = _masked_softmax(_dot_nt(q_ref[0, :, sl], k_ref[0, :n, sl]), vis)
                o = _dot(p.astype(BF16), v_ref[0, :n, sl]) * inv
                acc = o if acc is None else acc + o
            o_ref[0, :, m * LANES:(m + 1) * LANES] = acc

    _for_visible_prefix(run, tq, k_ref.shape[1], past)


def _mla_attn(q, k, v, past, true_len):
    b, s, _ = q.shape
    lp = k.shape[1]
    tq = _tile(s, 256)
    kv = pl.BlockSpec((1, lp, A_HEADS * LANES), lambda i, j: (i, 0, 0))
    return pl.pallas_call(
        functools.partial(_mla_attn_kernel, past=past, true_len=true_len),
        grid=(b, s // tq),
        in_specs=[pl.BlockSpec((1, tq, A_HEADS * LANES), lambda i, j: (i, j, 0)), kv, kv],
        out_specs=pl.BlockSpec((1, tq, A_HEADS * A_D_V), lambda i, j: (i, j, 0)),
        out_shape=jax.ShapeDtypeStruct((b, s, A_HEADS * A_D_V), F32),
        compiler_params=_cparams("parallel", "arbitrary"),
        name="mla_attn",
    )(q, k, v)


def _rwkv_prep_kernel(pb_ref, shift_ref, mu_ref, w0_ref, w2_ref, a0_ref, a2_ref, g2_ref, kk_ref, ka_ref, rk_ref,
                      bd_ref, r_ref, w_ref, k_ref, v_ref, an_ref, bv_ref, g_ref, bonus_ref, last_ref):
    pb = pb_ref[0]
    ts = pb.shape[0]

    @pl.when(pl.program_id(1) == 0)
    def _():
        last_ref[...] = shift_ref[0]

    first = lax.broadcasted_iota(I32, (ts, 1), 0) == 0
    prev = jnp.where(first, last_ref[...], pltpu.roll(pb, 1, axis=0))
    last_ref[...] = pb[ts - 1:ts, :]
    xs = pb + (prev - pb) * mu_ref[...]
    r = xs[:, 0:B_WIDTH]
    k = xs[:, B_WIDTH:2 * B_WIDTH]
    v = xs[:, 2 * B_WIDTH:3 * B_WIDTH]
    wa = xs[:, 3 * B_WIDTH:3 * B_WIDTH + B_W_LORA + B_A_LORA]
    gl = xs[:, 3 * B_WIDTH + B_W_LORA + B_A_LORA:]
    bd = bd_ref[...]
    z = -(w0_ref[...] + _dot(jnp.tanh(wa).astype(BF16), w2_ref[...]))
    softplus = jnp.maximum(z, 0.0) + jnp.log(1.0 + jnp.exp(-jnp.abs(z)))
    w_ref[0] = -jnp.exp(-softplus - 0.5)
    a = jax.nn.sigmoid(a0_ref[...] + _dot(wa.astype(BF16), a2_ref[...]))
    g_ref[0] = _dot(jax.nn.sigmoid(gl).astype(BF16), g2_ref[...])
    kk = k * kk_ref[...]
    kk = kk * lax.rsqrt(_split_dot(kk * kk, bd) + 1e-12)
    k2 = k * (1.0 + (a - 1.0) * ka_ref[...])
    r_ref[0] = r
    k_ref[0] = k2
    v_ref[0] = v
    an_ref[0] = -kk
    bv_ref[0] = kk * a
    bonus_ref[0] = _split_dot(r * k2 * rk_ref[...], bd) * v


def _rwkv_prep(pb3, shift, w):
    b, s, _ = pb3.shape
    ts = _tile(s, 256)
    blk = lambda n: pl.BlockSpec((1, ts, n), lambda i, j: (i, j, 0))
    vec = _full((1, B_WIDTH))
    return pl.pallas_call(
        _rwkv_prep_kernel,
        grid=(b, s // ts),
        in_specs=[blk(B_PROJ), pl.BlockSpec((1, 1, B_PROJ), lambda i, j: (i, 0, 0)), _full((1, B_PROJ)), vec,
                  _full((LANES, B_WIDTH)), vec, _full((LANES, B_WIDTH)), _full((B_G_LORA, B_WIDTH)), vec, vec, vec,
                  _full((B_WIDTH, B_WIDTH))],
        out_specs=[blk(B_WIDTH)] * 8,
        out_shape=[jax.ShapeDtypeStruct((b, s, B_WIDTH), F32)] * 8,
        scratch_shapes=[pltpu.VMEM((1, B_PROJ), F32)],
        compiler_params=_cparams("parallel", "arbitrary"),
        name="rwkv_prep",
    )(pb3, shift[:, None, :], w["b_mu"], w["b_w0"], w["b_w2"], w["b_a0"], w["b_a2"], w["b_g2"], w["b_kk"], w["b_ka"],
      w["b_rk"], w["bd512"])


def _rwkv_scan_kernel(r_ref, lw_ref, k_ref, v_ref, an_ref, bv_ref, s0_ref, y_ref, sout_ref, st_ref):
    nb, c, _ = r_ref.shape
    pairs = B_HEADS // 2

    @pl.when(pl.program_id(1) == 0)
    def _():
        st_ref[...] = s0_ref[...]

    bf = lambda x: x.astype(BF16)
    step = lax.broadcasted_iota(I32, (c, c), 0)
    prev = lax.broadcasted_iota(I32, (c, c), 1)
    lower = prev <= step
    strict = prev < step
    tril = bf(jnp.where(lower, 1.0, 0.0))
    head_of_lane = lax.broadcasted_iota(I32, (1, LANES), 1) >> 6
    same_head = (lax.broadcasted_iota(I32, (LANES, LANES), 0) >> 6) == (lax.broadcasted_iota(I32, (LANES, LANES), 1) >> 6)
    levels = max(1, (c - 1).bit_length())
    pair_ids = [(b, p) for b in range(nb) for p in range(pairs)]
    head_ids = [(q, hh) for q in range(len(pair_ids)) for hh in range(2)]
    load = lambda ref: [ref[b, :, p * LANES:(p + 1) * LANES] for b, p in pair_ids]
    r, lw, k, v, an, bv = (load(ref) for ref in (r_ref, lw_ref, k_ref, v_ref, an_ref, bv_ref))
    pieces = []
    rem = lw
    for _ in range(3):
        pieces.append([bf(x) for x in rem])
        rem = [x - pc.astype(F32) for x, pc in zip(rem, pieces[-1])]
    logp = [_dot(tril, pc) for pc in pieces[0]]
    for level in pieces[1:]:
        logp = [acc + _dot(tril, pc) for acc, pc in zip(logp, level)]
    p_inc = [jnp.exp(x) for x in logp]
    p_inv = [jnp.exp(-x) for x in logp]
    a_t = [a * jnp.exp(lp - l) for a, lp, l in zip(an, logp, lw)]
    b_t = [bf(x * pi) for x, pi in zip(bv, p_inv)]
    k_t = [bf(x * pi) for x, pi in zip(k, p_inv)]
    r_t = [x * pi for x, pi in zip(r, p_inc)]
    s2 = [st_ref[b, p] for b, p in pair_ids]
    s2b = [bf(x) for x in s2]
    a_s = [_dot_nt(bf(x), s) for x, s in zip(a_t, s2b)]
    r_s = [_dot_nt(bf(x), s) for x, s in zip(r_t, s2b)]
    mine = [head_of_lane == hh for _, hh in head_ids]
    only = lambda xs: [bf(jnp.where(m, xs[q], 0.0)) for (q, _), m in zip(head_ids, mine)]
    a_h, r_h, v_h = only(a_t), only(r_t), only(v)
    n_pow = [bf(jnp.where(strict, _dot_nt(a, b_t[q]), 0.0)) for a, (q, _) in zip(a_h, head_ids)]
    l_ak = [bf(jnp.where(strict, _dot_nt(a, k_t[q]), 0.0)) for a, (q, _) in zip(a_h, head_ids)]
    m_rb = [bf(jnp.where(lower, _dot_nt(x, b_t[q]), 0.0)) for x, (q, _) in zip(r_h, head_ids)]
    m_rk = [bf(jnp.where(lower, _dot_nt(x, k_t[q]), 0.0)) for x, (q, _) in zip(r_h, head_ids)]
    x = [jnp.where(m, a_s[q], 0.0) + _dot(l, vh) for (q, _), m, l, vh in zip(head_ids, mine, l_ak, v_h)]
    for lv in range(levels):
        x = [xi + _dot(n, bf(xi)) for xi, n in zip(x, n_pow)]
        if lv + 1 < levels:
            n_pow = [bf(_dot(n, n)) for n in n_pow]
    y_h = [jnp.where(m, r_s[q], 0.0) + _dot(mb, bf(xi)) + _dot(mk, vh)
           for (q, _), m, mb, mk, xi, vh in zip(head_ids, mine, m_rb, m_rk, x, v_h)]
    for q, (b, p) in enumerate(pair_ids):
        y_ref[b, :, p * LANES:(p + 1) * LANES] = y_h[2 * q] + y_h[2 * q + 1]
        uv_t = bf(jnp.concatenate([x[2 * q] + x[2 * q + 1], v[q]], axis=0).T)
        add = _dot(uv_t, jnp.concatenate([b_t[q], k_t[q]], axis=0))
        st_ref[b, p] = (s2[q] + jnp.where(same_head, add, 0.0)) * p_inc[q][c - 1:c, :]

    @pl.when(pl.program_id(1) == pl.num_programs(1) - 1)
    def _():
        sout_ref[...] = st_ref[...]


def _rwkv_scan(seqs, s0):
    b, s, _ = seqs[0].shape
    nb = 2 if b % 2 == 0 else 1
    c = _tile(s, RWKV_CHUNK)
    pairs = B_HEADS // 2
    seq = pl.BlockSpec((nb, c, B_WIDTH), lambda i, j: (i, j, 0))
    sts = pl.BlockSpec((nb, pairs, LANES, LANES), lambda i, j: (i, 0, 0, 0))
    return pl.pallas_call(
        _rwkv_scan_kernel,
        grid=(b // nb, s // c),
        in_specs=[seq] * 6 + [sts],
        out_specs=[seq, sts],
        out_shape=[jax.ShapeDtypeStruct((b, s, B_WIDTH), F32), jax.ShapeDtypeStruct((b, pairs, LANES, LANES), F32)],
        scratch_shapes=[pltpu.VMEM((nb, pairs, LANES, LANES), F32)],
        compiler_params=_cparams("parallel", "arbitrary"),
        name="rwkv_scan",
    )(*seqs, s0)


def _rwkv_post_kernel(y_ref, bonus_ref, g_ref, lg_ref, lb_ref, bd_ref, o_ref):
    bd = bd_ref[...]
    y = y_ref[...]
    d = y - _split_dot(y, bd) * (1.0 / B_HD)
    var = _split_dot(d * d, bd) * (1.0 / B_HD)
    yn = d * lax.rsqrt(var + B_LN_EPS) * lg_ref[...] + lb_ref[...]
    o_ref[...] = (yn + bonus_ref[...]) * g_ref[...]


def _rwkv_post(y2, bonus2, g2, w):
    t = y2.shape[0]
    tm = _tile(t, 512)
    row = pl.BlockSpec((tm, B_WIDTH), lambda i: (i, 0))
    vec = _full((1, B_WIDTH))
    return pl.pallas_call(
        _rwkv_post_kernel,
        grid=(t // tm,),
        in_specs=[row, row, row, vec, vec, _full((B_WIDTH, B_WIDTH))],
        out_specs=row,
        out_shape=jax.ShapeDtypeStruct((t, B_WIDTH), F32),
        compiler_params=_cparams("parallel"),
        name="rwkv_post",
    )(y2, bonus2, g2, w["b_ln_g"], w["b_ln_b"], w["bd512"])


def _out_proj_kernel(x_ref, o1_ref, o2_ref, w1_ref, w2_ref, gf_ref, wq_ref, xo_ref, xn_ref, q_ref):
    x = x_ref[...] + _dot(o1_ref[...].astype(BF16), w1_ref[...]) + _dot(o2_ref[...].astype(BF16), w2_ref[...])
    xo_ref[...] = x
    xn = (_rms(x) * gf_ref[...]).astype(BF16)
    xn_ref[...] = xn
    q_ref[...] = _dot(xn, wq_ref[...])


def _out_proj(x2, o1, o2, w1, w2, gf, wq):
    t = x2.shape[0]
    tm = _tile(t, 512)
    row = lambda n: pl.BlockSpec((tm, n), lambda i: (i, 0))
    half = o1.shape[1]
    return pl.pallas_call(
        _out_proj_kernel,
        grid=(t // tm,),
        in_specs=[row(D_MODEL), row(half), row(half), _full((half, D_MODEL)), _full((half, D_MODEL)),
                  _full((1, D_MODEL)), _full((D_MODEL, PEER_HEADS * PEER_DK))],
        out_specs=[row(D_MODEL), row(D_MODEL), row(PEER_HEADS * PEER_DK)],
        out_shape=[jax.ShapeDtypeStruct((t, D_MODEL), F32), jax.ShapeDtypeStruct((t, D_MODEL), BF16),
                   jax.ShapeDtypeStruct((t, PEER_HEADS * PEER_DK), F32)],
        compiler_params=_cparams("parallel"),
        name="out_proj",
    )(x2, o1, o2, w1, w2, gf, wq)


def _top16_rows(sc_ref, val_ref, idx_ref):
    nc, n, tb = sc_ref.shape
    rowi = lax.broadcasted_iota(I32, (n, tb), 0).astype(F32)

    def body(r, carry):
        for c in range(nc):
            sc = sc_ref[c]
            m = jnp.max(sc, axis=0, keepdims=True)
            ix = jnp.min(jnp.where(sc == m, rowi, float(n)), axis=0, keepdims=True)
            val_ref[c, pl.ds(r, 1), :] = m
            idx_ref[c, pl.ds(r, 1), :] = ix
            sc_ref[c] = jnp.where(rowi == ix, -jnp.inf, sc)
        return carry

    lax.fori_loop(0, PEER_TOPK, body, 0)


def _pair_candidates(s1, s2, x1, x2):
    subi = lax.broadcasted_iota(I32, (SUBLANES, s1.shape[1]), 0)
    sub = subi.astype(F32)
    low4 = subi < 4
    b4 = (subi & 3).astype(F32)
    two = lambda z, a: jnp.where(low4, z[a:a + 1], z[a + 1:a + 2])
    dup4 = lambda z: jnp.where(low4, z[0:8], pltpu.roll(z[0:8], 4, axis=0))
    cand = [s1[0:1] + s2[0:8], s1[0:1] + s2[8:16]]
    flat = [sub, 8 + sub]
    eid = [x1[0:1] * PEER_N_KEYS + x2[0:8], x1[0:1] * PEER_N_KEYS + x2[8:16]]
    for a in (1, 2, 3):
        cand.append(s1[a:a + 1] + s2[0:8])
        flat.append(a * PEER_TOPK + sub)
        eid.append(x1[a:a + 1] * PEER_N_KEYS + x2[0:8])
    for a in (4, 6):
        cand.append(two(s1, a) + dup4(s2))
        flat.append(jnp.where(low4, float(a * PEER_TOPK), float((a + 1) * PEER_TOPK)) + b4)
        eid.append(two(x1, a) * PEER_N_KEYS + dup4(x2))
    cand.append(s1[8:16] + s2[0:1])
    flat.append((8 + sub) * PEER_TOPK)
    eid.append(x1[8:16] * PEER_N_KEYS + x2[0:1])
    return jnp.concatenate(cand, axis=0), jnp.concatenate(flat, axis=0), jnp.concatenate(eid, axis=0)


def _peer_route_kernel(q_ref, keys_ref, e1_ref, e2_ref, gate_ref, sc_ref, topv, topi, cand_ref, eid_ref, selv, sele):
    for h in range(PEER_HEADS):
        qh = q_ref[:, h * PEER_DK:(h + 1) * PEER_DK].astype(BF16)
        sc_ref[2 * h] = _dot_nt(keys_ref[h, 0], qh)
        sc_ref[2 * h + 1] = _dot_nt(keys_ref[h, 1], qh)
    _top16_rows(sc_ref, topv, topi)
    for h in range(PEER_HEADS):
        cand, flat, eid = _pair_candidates(topv[2 * h], topv[2 * h + 1], topi[2 * h], topi[2 * h + 1])
        cand_ref[h] = cand
        eid_ref[h] = eid

    def body(r, carry):
        for h in range(PEER_HEADS):
            cand = cand_ref[h]
            m = jnp.max(cand, axis=0, keepdims=True)
            f = jnp.min(jnp.where(cand == m, flat, 1e9), axis=0, keepdims=True)
            hit = flat == f
            selv[h, pl.ds(r, 1), :] = m
            sele[h, pl.ds(r, 1), :] = jnp.max(jnp.where(hit, eid_ref[h], -1.0), axis=0, keepdims=True)
            cand_ref[h] = jnp.where(hit, -jnp.inf, cand)
        return carry

    lax.fori_loop(0, PEER_TOPK, body, 0)
    for h in range(PEER_HEADS):
        rows = slice(h * PEER_TOPK, (h + 1) * PEER_TOPK)
        top = selv[h]
        e = jnp.exp(top - top[0:1])
        gate_ref[rows, :] = e * (1.0 / jnp.sum(e, axis=0, keepdims=True))
        ex = sele[h].astype(I32)
        e1_ref[rows, :] = ex >> 7
        e2_ref[rows, :] = ex & (PEER_N_KEYS - 1)


def _peer_route(q2, keys_pad):
    t = q2.shape[0]
    tb = _tile(t, 256)
    nk = PEER_HEADS * PEER_TOPK
    out = pl.BlockSpec((nk, tb), lambda i: (0, i))
    return pl.pallas_call(
        _peer_route_kernel,
        grid=(t // tb,),
        in_specs=[pl.BlockSpec((tb, PEER_HEADS * PEER_DK), lambda i: (i, 0)),
                  _full((PEER_HEADS, 2, PEER_N_KEYS, PEER_DK))],
        out_specs=[out, out, out],
        out_shape=[jax.ShapeDtypeStruct((nk, t), I32)] * 2 + [jax.ShapeDtypeStruct((nk, t), F32)],
        scratch_shapes=[pltpu.VMEM((2 * PEER_HEADS, PEER_N_KEYS, tb), F32),
                        pltpu.VMEM((2 * PEER_HEADS, PEER_TOPK, tb), F32), pltpu.VMEM((2 * PEER_HEADS, PEER_TOPK, tb), F32),
                        pltpu.VMEM((PEER_HEADS, N_PAIR_CAND, tb), F32), pltpu.VMEM((PEER_HEADS, N_PAIR_CAND, tb), F32),
                        pltpu.VMEM((PEER_HEADS, PEER_TOPK, tb), F32), pltpu.VMEM((PEER_HEADS, PEER_TOPK, tb), F32)],
        compiler_params=_cparams("parallel"),
        name="peer_route",
    )(q2, keys_pad)


def _peer_gate_kernel(e1_ref, e2_ref, gate_ref, o_ref):
    tg = e1_ref.shape[0]
    nk = PEER_HEADS * PEER_TOPK
    sub = lax.broadcasted_iota(I32, (PEER_N_KEYS, nk), 0)
    grp = 2 * SUBLANES

    def body(c, carry):
        t0 = pl.multiple_of(c * grp, grp)
        e1 = e1_ref[pl.ds(t0, grp), :]
        e2 = e2_ref[pl.ds(t0, grp), :]
        g = gate_ref[pl.ds(t0, grp), :]
        halves = []
        for j0 in range(0, grp, SUBLANES):
            mats = []
            for j in range(j0, j0 + SUBLANES):
                a_t = jnp.where(sub == e1[j:j + 1], g[j:j + 1], 0.0).astype(BF16)
                b_t = jnp.where(sub == e2[j:j + 1], 1.0, 0.0).astype(BF16)
                mats.append(_dot_nt(a_t, b_t))
            halves.append(jnp.swapaxes(jnp.stack(mats, axis=0), 0, 1))
        o_ref[:, pl.ds(t0, grp), :] = jnp.concatenate(halves, axis=1).astype(BF16)
        return carry

    lax.fori_loop(0, tg // grp, body, 0, unroll=4)


def _peer_gates(e1, e2, gate):
    t = e1.shape[0]
    tg = _tile(t, 128)
    assert tg % (8 * SUBLANES) == 0, "the gate kernel writes its output in unrolled groups of 64 tokens"
    row = pl.BlockSpec((tg, PEER_HEADS * PEER_TOPK), lambda i: (i, 0))
    return pl.pallas_call(
        _peer_gate_kernel,
        grid=(t // tg,),
        in_specs=[row, row, row],
        out_specs=pl.BlockSpec((PEER_N_KEYS, tg, PEER_N_KEYS), lambda i: (0, i, 0)),
        out_shape=jax.ShapeDtypeStruct((PEER_N_KEYS, t, PEER_N_KEYS), BF16),
        compiler_params=_cparams("parallel"),
        name="peer_gates",
    )(e1, e2, gate)


def _gelu(x):
    return 0.5 * x * (1.0 + lax.erf(x * (2.0 ** -0.5)))


def _peer_dense_kernel(xn_ref, ut_ref, v_ref, g_ref, x_ref, o_ref):
    @pl.when(pl.program_id(1) == 0)
    def _():
        o_ref[...] = x_ref[...]

    h = _gelu(_dot(xn_ref[...], ut_ref[...]))
    nblk = g_ref.shape[0]
    gh = [(h[:, c * LANES:(c + 1) * LANES] * g_ref[c].astype(F32)).astype(BF16) for c in range(nblk)]
    o_ref[...] += _dot(jnp.concatenate(gh, axis=-1), v_ref[...])


def _peer_dense(xn, ut, vtab, gates, x2):
    t = xn.shape[0]
    tb = _tile(t, 1024)
    eb = 1024
    return pl.pallas_call(
        _peer_dense_kernel,
        grid=(t // tb, PEER_N_EXPERTS // eb),
        in_specs=[pl.BlockSpec((tb, D_MODEL), lambda i, j: (i, 0)),
                  pl.BlockSpec((D_MODEL, eb), lambda i, j: (0, j)),
                  pl.BlockSpec((eb, D_MODEL), lambda i, j: (j, 0)),
                  pl.BlockSpec((eb // PEER_N_KEYS, tb, PEER_N_KEYS), lambda i, j: (j, i, 0)),
                  pl.BlockSpec((tb, D_MODEL), lambda i, j: (i, 0))],
        out_specs=pl.BlockSpec((tb, D_MODEL), lambda i, j: (i, 0)),
        out_shape=jax.ShapeDtypeStruct((t, D_MODEL), F32),
        compiler_params=_cparams("parallel", "arbitrary"),
        name="peer_dense",
    )(xn, ut, vtab, gates, x2)


def _peer(x2, xn, q2, pw):
    e1, e2, gate = _peer_route(q2, pw["keys"])
    gates = _peer_gates(e1.T, e2.T, gate.T)
    return _peer_dense(xn, pw["ut"], pw["v"], gates, x2)


def _in1_kernel(x_ref, g_ref, wq_ref, wk_ref, wv_ref, wiq_ref, wik_ref, wiw_ref, wd_ref,
                q_ref, k_ref, v_ref, iq_ref, ik_ref, iw_ref, d_ref):
    xn = (_rms(x_ref[...]) * g_ref[...]).astype(BF16)
    for w_ref, o_ref in ((wq_ref, q_ref), (wk_ref, k_ref), (wv_ref, v_ref), (wiq_ref, iq_ref),
                         (wik_ref, ik_ref), (wiw_ref, iw_ref), (wd_ref, d_ref)):
        o_ref[...] = _dot(xn, w_ref[...])


def _in_proj1(x2, w):
    t = x2.shape[0]
    tm = _tile(t, 512)
    widths = [C_HEADS * C_HD, LANES, LANES, C_IDX_HEADS * C_IDX_D, LANES, LANES, 2 * D_WIDTH]
    row = lambda n: pl.BlockSpec((tm, n), lambda i: (i, 0))
    return pl.pallas_call(
        _in1_kernel,
        grid=(t // tm,),
        in_specs=[row(D_MODEL), _full((1, D_MODEL))] + [_full((D_MODEL, n)) for n in widths],
        out_specs=[row(n) for n in widths],
        out_shape=[jax.ShapeDtypeStruct((t, n), F32) for n in widths],
        compiler_params=_cparams("parallel"),
        name="in_proj1",
    )(x2, w["l1_norm"], w["w_cq1"], w["w_ck1"], w["w_cv1"], w["w_ciq"], w["w_cik"], w["w_ciw"], w["w_d"])


def _dsa_prep_kernel(q_ref, k_ref, iq_ref, ik_ref, qg_ref, kg_ref, ig_ref, cos_ref, sin_ref, bd_ref, perm_ref,
                     qo_ref, ko_ref, iqo_ref, iko_ref):
    cos, sin, bd, perm = cos_ref[...], sin_ref[...], bd_ref[...], perm_ref[...]

    def norm(x, gain, n):
        ss = _split_dot(x * x, bd[:n, :n]) * (1.0 / C_HD)
        return x * lax.rsqrt(ss + NORM_EPS) * gain

    def rope(x, n):
        return x * cos[:, :n] + _split_dot(x, perm[:n, :n]) * sin[:, :n]

    nq = C_HEADS * C_HD
    qo_ref[0] = (rope(norm(q_ref[0], qg_ref[...], nq), nq) * (C_HD ** -0.5)).astype(BF16)
    ko_ref[0] = rope(norm(k_ref[0], kg_ref[...], LANES), LANES)
    iqo_ref[0] = (rope(iq_ref[0], nq) * (C_IDX_D ** -0.5)).astype(BF16)
    iko_ref[0] = rope(norm(ik_ref[0], ig_ref[...], LANES), LANES)


def _dsa_prep(q, k, iq, ik, w, cos, sin):
    b, s, _ = q.shape
    ts = _tile(s, 256)
    nq = C_HEADS * C_HD
    blk = lambda n: pl.BlockSpec((1, ts, n), lambda i, j: (i, j, 0))
    tab = pl.BlockSpec((ts, nq), lambda i, j: (j, 0))
    return pl.pallas_call(
        _dsa_prep_kernel,
        grid=(b, s // ts),
        in_specs=[blk(nq), blk(LANES), blk(nq), blk(LANES), _full((1, nq)), _full((1, LANES)), _full((1, LANES)),
                  tab, tab, _full((nq, nq)), _full((nq, nq))],
        out_specs=[blk(nq), blk(LANES), blk(nq), blk(LANES)],
        out_shape=[jax.ShapeDtypeStruct((b, s, nq), BF16), jax.ShapeDtypeStruct((b, s, LANES), F32),
                   jax.ShapeDtypeStruct((b, s, nq), BF16), jax.ShapeDtypeStruct((b, s, LANES), F32)],
        compiler_params=_cparams("parallel", "parallel"),
        name="dsa_prep",
    )(q, k, iq, ik, w["c_q_gain"], w["c_k_gain"], w["c_kidx_gain"], cos, sin, w["bd512"], w["perm_c"])


KEY_OF_NEG_INF = -(2 ** 31) + 0x7FFFFF


def _key_to_float(key):
    return pltpu.bitcast(jnp.where(key < 0, key ^ jnp.int32(0x7FFFFFFF), key), F32)


def _topk_mask(score, n_sel, su):
    tq, lp = score.shape

    def body(i, tau):
        trial = tau + lax.shift_left(jnp.int32(1), 31 - i)
        cnt = jnp.sum(jnp.where(score >= _key_to_float(trial), 1.0, 0.0), axis=-1, keepdims=True)
        return jnp.where(trial <= KEY_OF_NEG_INF, trial, jnp.where(cnt >= n_sel, trial, tau))

    kth = _key_to_float(lax.fori_loop(0, 32, body, jnp.full((tq, 1), -(2 ** 31), I32)))
    gt = jnp.where(score > kth, 1.0, 0.0)
    eq = jnp.where(score == kth, 1.0, 0.0)
    need = n_sel - jnp.sum(gt, axis=-1, keepdims=True)
    parts = []
    before = jnp.zeros((tq, 1), F32)
    for c in range(lp // LANES):
        eqc = eq[:, c * LANES:(c + 1) * LANES]
        rank = before + _dot(eqc.astype(BF16), su)
        parts.append(gt[:, c * LANES:(c + 1) * LANES] + jnp.where(rank < need, eqc, 0.0))
        before = before + jnp.sum(eqc, axis=-1, keepdims=True)
    return jnp.concatenate(parts, axis=-1)


def _dsa_attn_kernel(q_ref, iq_ref, iw_ref, ik_ref, k_ref, v_ref, su_ref, o_ref, *, past, true_len, n_sel):
    tq = q_ref.shape[1]
    low = lax.broadcasted_iota(I32, (1, LANES), 1) < C_HD
    halves = lambda x: (jnp.where(low, x, jnp.zeros((), x.dtype)), jnp.where(low, jnp.zeros((), x.dtype), x))

    def run(n, q0):
        vis = _visible(tq, n, q0, true_len)
        iw = iw_ref[0] * (C_IDX_HEADS ** -0.5)
        ik2 = ik_ref[0, :n]
        score = jnp.zeros((tq, n), F32)
        for m in range(C_IDX_HEADS // 2):
            iqa, iqb = halves(iq_ref[0, :, m * LANES:(m + 1) * LANES])
            score = score + jnp.maximum(_dot_nt(iqa, ik2), 0.0) * iw[:, 2 * m:2 * m + 1]
            score = score + jnp.maximum(_dot_nt(iqb, ik2), 0.0) * iw[:, 2 * m + 1:2 * m + 2]
        score = jnp.where(vis, score, NEG_INF)
        keep = jnp.where(vis, _topk_mask(score, n_sel, su_ref[...]), 0.0) > 0.5
        k = k_ref[0, :n]
        va, vb = halves(v_ref[0, :n])
        for m in range(C_HEADS // 2):
            qa, qb = halves(q_ref[0, :, m * LANES:(m + 1) * LANES])
            pa, inva = _masked_softmax(_dot_nt(qa, k), keep)
            pb, invb = _masked_softmax(_dot_nt(qb, k), keep)
            o_ref[0, :, m * LANES:(m + 1) * LANES] = (_dot(pa.astype(BF16), va) * inva
                                                      + _dot(pb.astype(BF16), vb) * invb)

    _for_visible_prefix(run, tq, ik_ref.shape[1], past)


def _dsa_attn(q, iq, iw, ik2, k, v, su, past, true_len):
    b, s, nq = q.shape
    lp = ik2.shape[1]
    tq = _tile(s, 256)
    n_sel = min(C_TOPK, true_len // 4)
    qblk = lambda n: pl.BlockSpec((1, tq, n), lambda i, j: (i, j, 0))
    keys = pl.BlockSpec((1, lp, LANES), lambda i, j: (i, 0, 0))
    return pl.pallas_call(
        functools.partial(_dsa_attn_kernel, past=past, true_len=true_len, n_sel=n_sel),
        grid=(b, s // tq),
        in_specs=[qblk(nq), qblk(nq), qblk(LANES), keys, keys, keys, _full((LANES, LANES))],
        out_specs=qblk(nq),
        out_shape=jax.ShapeDtypeStruct((b, s, nq), F32),
        compiler_params=_cparams("parallel", "arbitrary"),
        name="dsa_attn",
    )(q, iq, iw, ik2, k, v, su)


def _gmlp_kernel(pd_ref, lg_ref, lb_ref, ws_ref, bs_ref, o_ref, dv_ref):
    span = pd_ref.shape[1]
    h = _gelu(pd_ref[0])
    u = h[:, :D_WIDTH]
    v = h[:, D_WIDTH:]
    mu = jnp.mean(v, axis=-1, keepdims=True)
    d = v - mu
    v = d * lax.rsqrt(jnp.mean(d * d, axis=-1, keepdims=True) + NORM_EPS) * lg_ref[...] + lb_ref[...]
    dv_ref[0] = v
    causal = lax.broadcasted_iota(I32, (span, span), 1) <= lax.broadcasted_iota(I32, (span, span), 0)
    gw = D_WIDTH // D_GROUPS
    mixed = [_dot(jnp.where(causal, ws_ref[g], 0.0).astype(BF16), v[:, g * gw:(g + 1) * gw].astype(BF16))
             for g in range(D_GROUPS)]
    o_ref[0] = u * (jnp.concatenate(mixed, axis=-1) + bs_ref[...])


def _gmlp(pd, w):
    b, s, _ = pd.shape
    span = D_SPAN if s % D_SPAN == 0 else s
    ws = w["d_ws"][:, :span, :span]
    bs = jnp.repeat(w["d_bs"][:, :span].T, D_WIDTH // D_GROUPS, axis=1)
    blk = lambda n: pl.BlockSpec((1, span, n), lambda i, j: (i, j, 0))
    return pl.pallas_call(
        _gmlp_kernel,
        grid=(b, s // span),
        in_specs=[blk(2 * D_WIDTH), _full((1, D_WIDTH)), _full((1, D_WIDTH)), _full((D_GROUPS, span, span)),
                  _full((span, D_WIDTH))],
        out_specs=[blk(D_WIDTH), blk(D_WIDTH)],
        out_shape=[jax.ShapeDtypeStruct((b, s, D_WIDTH), F32)] * 2,
        compiler_params=_cparams("parallel", "parallel"),
        name="gmlp",
    )(pd, w["d_ln_g"], w["d_ln_b"], ws, bs)


def _pad_cols(w, groups, width, slot, off=0):
    k = w.shape[0]
    w = jnp.pad(w.reshape(k, groups, width), ((0, 0), (0, 0), (off, slot - off - width)))
    return w.reshape(k, groups * slot)


def _prep_weights(p):
    row = lambda v: v.reshape(1, -1).astype(F32)
    bf = lambda v: v.astype(BF16)
    w = {}
    w_in0 = p["l0_w_in"]
    w["l0_norm"] = row(p["l0_norm_mix"])
    w["w_cq"] = bf(w_in0[:, :A_D_CQ])
    w["w_ckv"] = bf(w_in0[:, A_D_CQ:A_D_CQ + A_D_C])
    a_cols = A_D_CQ + A_D_C + A_D_ROPE
    w["w_krp"] = bf(_pad_cols(w_in0[:, A_D_CQ + A_D_C:a_cols], 1, A_D_ROPE, LANES, A_D_NOPE))
    w["w_b"] = bf(w_in0[:, a_cols:])
    w["a_kv_norm"] = row(p["a_kv_norm"])
    w["a_q_norm"] = row(p["a_q_norm"])
    w["w_uq"] = bf(_pad_cols(p["a_w_uq"], A_HEADS, A_D_QK, LANES))
    ukv = p["a_w_ukv"].reshape(A_D_C, A_HEADS, A_D_NOPE + A_D_V)
    w["w_uk"] = bf(_pad_cols(ukv[:, :, :A_D_NOPE].reshape(A_D_C, -1), A_HEADS, A_D_NOPE, LANES))
    uv = ukv[:, :, A_D_NOPE:].reshape(A_D_C, A_HEADS // 2, 2, A_D_V)
    slot = lambda x, off: jnp.pad(x, ((0, 0), (0, 0), (off, LANES - off - A_D_V)))
    uv_pad = jnp.stack([slot(uv[:, :, 0], 0), slot(uv[:, :, 1], A_D_V)], axis=2)
    w["w_uv"] = bf(uv_pad.reshape(A_D_C, A_HEADS * LANES))
    w["a_q_gain"] = jnp.pad(row(p["a_q_gain"]), ((0, 0), (0, LANES - A_D_QK)))
    w["a_k_gain"] = jnp.pad(row(p["a_k_gain"]), ((0, 0), (0, LANES - A_D_QK)))
    w["perm_a"] = _rope_perm(LANES, LANES, A_D_NOPE, A_D_ROPE // 2)
    w["b_mu"] = row(p["b_mu"])
    w["b_w0"] = row(p["b_w0"])
    w["b_a0"] = row(p["b_a0"])
    zeros = jnp.zeros((B_W_LORA, B_WIDTH), F32)
    w["b_w2"] = bf(jnp.concatenate([p["b_w2"], zeros], axis=0))
    w["b_a2"] = bf(jnp.concatenate([zeros, p["b_a2"]], axis=0))
    w["b_g2"] = bf(p["b_g2"])
    for n in ("b_kk", "b_ka", "b_rk", "b_ln_g", "b_ln_b"):
        w[n] = row(p[n])
    w["bd512"] = _block_ones(B_WIDTH, B_HD)
    w["w_out0a"] = bf(p["l0_w_out"][:A_HEADS * A_D_V])
    w["w_out0b"] = bf(p["l0_w_out"][A_HEADS * A_D_V:])
    w_in1 = p["l1_w_in"]
    nq = C_HEADS * C_HD
    nkv = C_KV_HEADS * C_HD
    o = 0
    order = jnp.arange(C_HEADS).reshape(C_KV_HEADS, C_HEADS // C_KV_HEADS).T.reshape(-1)
    w["w_cq1"] = bf(w_in1[:, o:o + nq].reshape(D_MODEL, C_HEADS, C_HD)[:, order].reshape(D_MODEL, nq)); o += nq
    w["w_ck1"] = bf(w_in1[:, o:o + nkv]); o += nkv
    w["w_cv1"] = bf(w_in1[:, o:o + nkv]); o += nkv
    w["w_ciq"] = bf(w_in1[:, o:o + nq]); o += nq
    w["w_cik"] = bf(_pad_cols(w_in1[:, o:o + C_IDX_D], 1, C_IDX_D, LANES)); o += C_IDX_D
    w["w_ciw"] = bf(_pad_cols(w_in1[:, o:o + C_IDX_HEADS], 1, C_IDX_HEADS, LANES)); o += C_IDX_HEADS
    w["w_d"] = bf(w_in1[:, o:])
    w["l1_norm"] = row(p["l1_norm_mix"])
    w["c_q_gain"] = row(jnp.tile(p["c_q_gain"], C_HEADS))
    w["c_k_gain"] = row(jnp.tile(p["c_k_gain"], C_KV_HEADS))
    w["c_kidx_gain"] = jnp.pad(row(p["c_kidx_gain"]), ((0, 0), (0, LANES - C_IDX_D)))
    w["perm_c"] = _rope_perm(nq, C_HD, 0, C_HD // 2)
    w["su"] = (jnp.arange(LANES)[:, None] < jnp.arange(LANES)[None, :]).astype(BF16)
    w["d_ln_g"] = row(p["d_ln_g"])
    w["d_ln_b"] = row(p["d_ln_b"])
    w["d_ws"] = p["d_ws"]
    w["d_bs"] = p["d_bs"]
    w["w_out1a"] = bf(p["l1_w_out"][:nq].reshape(C_HEADS, C_HD, D_MODEL)[order].reshape(nq, D_MODEL))
    w["w_out1b"] = bf(p["l1_w_out"][nq:])
    for l in (0, 1):
        keys = p[f"l{l}_peer_keys"]
        half = PEER_DK // 2
        edge = lambda lo, hi: ((0, 0), (0, 0), (lo, hi))
        kp = jnp.stack([jnp.pad(keys[:, 0], edge(0, half)), jnp.pad(keys[:, 1], edge(half, 0))], axis=1)
        w[f"peer{l}"] = {"keys": bf(kp), "ut": bf(p[f"l{l}_peer_u"]).T, "v": bf(p[f"l{l}_peer_v"]),
                         "norm": row(p[f"l{l}_norm_ffn"]), "wq": bf(p[f"l{l}_peer_wq"])}
    return w


def _pad_keys(x, lp):
    return jnp.pad(x, ((0, 0), (0, lp - x.shape[1])) + ((0, 0),) * (x.ndim - 2))


def _trunk(x, st, w):
    b, s, _ = x.shape
    t = b * s
    past = st["a_ckv"].shape[1]
    true_len = past + s
    lp = -(-true_len // LANES) * LANES
    kpos = jnp.arange(lp, dtype=I32)

    cq, ckv, krp, pb = _in_proj0(x.reshape(t, D_MODEL), w)
    cos_a, sin_a = _rope_tables(kpos, A_D_ROPE // 2, LANES, A_D_NOPE, 1)
    q = _mla_q(cq.reshape(b, s, A_D_CQ), w, cos_a[past:true_len], sin_a[past:true_len])
    ckv3 = ckv.reshape(b, s, A_D_C)
    krp3 = krp.reshape(b, s, LANES)
    krope = krp3[:, :, A_D_NOPE:A_D_NOPE + A_D_ROPE]
    cache_krp = jnp.pad(st["a_krope"], ((0, 0), (0, 0), (A_D_NOPE, LANES - A_D_NOPE - A_D_ROPE)))
    ckv_all = _pad_keys(jnp.concatenate([st["a_ckv"], ckv3], axis=1), lp)
    krp_all = _pad_keys(jnp.concatenate([cache_krp, krp3], axis=1), lp)
    k_a, v_a = _mla_kv(ckv_all, krp_all, w, cos_a, sin_a)
    o_a = _mla_attn(q, k_a, v_a, past, true_len)

    pb3 = pb.reshape(b, s, B_PROJ)
    r, logw, k2, v, an, bv, g, bonus = _rwkv_prep(pb3, st["b_shift"], w)
    pairs = B_HEADS // 2
    sp = st["b_wkv"].reshape(b, pairs, 2, B_HD, B_HD)
    edge = lambda lo, hi: ((0, 0), (0, 0), (0, 0), (lo, hi))
    s0 = jnp.concatenate([jnp.pad(sp[:, :, 0], edge(0, B_HD)), jnp.pad(sp[:, :, 1], edge(B_HD, 0))], axis=2)
    y, s_out = _rwkv_scan([r, logw, k2, v, an, bv], s0)
    wkv = jnp.stack([s_out[:, :, :B_HD, :B_HD], s_out[:, :, B_HD:, B_HD:]], axis=2).reshape(b, B_HEADS, B_HD, B_HD)
    o_b = _rwkv_post(y.reshape(t, B_WIDTH), bonus.reshape(t, B_WIDTH), g.reshape(t, B_WIDTH), w)
    pw = w["peer0"]
    x1, xn1, pq1 = _out_proj(x.reshape(t, D_MODEL), o_a.reshape(t, -1), o_b, w["w_out0a"], w["w_out0b"],
                             pw["norm"], pw["wq"])
    x2 = _peer(x1, xn1, pq1, pw)

    nq = C_HEADS * C_HD
    cq1, ck1, cv1, ciq, cik, ciw, pd = _in_proj1(x2, w)
    pos_q = kpos[past:true_len]
    cos_c, sin_c = _rope_tables(pos_q, C_HD // 2, C_HD, 0, C_HEADS)
    three = lambda a: a.reshape(b, s, a.shape[-1])
    q_c, k_c, iq_c, ik_c = _dsa_prep(three(cq1), three(ck1), three(ciq), three(cik), w, cos_c, sin_c)
    c_k = k_c.reshape(b, s, C_KV_HEADS, C_HD)
    c_v = cv1.reshape(b, s, C_KV_HEADS, C_HD)
    c_kidx = ik_c[:, :, :C_IDX_D]
    flat_kv = lambda cache, new: _pad_keys(
        jnp.concatenate([cache.reshape(b, past, LANES), new], axis=1), lp).astype(BF16)
    ik_all = jnp.concatenate([st["c_kidx"], c_kidx], axis=1)
    ik2 = _pad_keys(jnp.concatenate([ik_all, ik_all], axis=-1), lp).astype(BF16)
    o_c = _dsa_attn(q_c, iq_c, three(ciw), ik2, flat_kv(st["c_k"], k_c), flat_kv(st["c_v"], three(cv1)),
                    w["su"], past, true_len)
    o_d, d_v = _gmlp(three(pd), w)
    pw = w["peer1"]
    x3, xn3, pq3 = _out_proj(x2, o_c.reshape(t, nq), o_d.reshape(t, D_WIDTH), w["w_out1a"], w["w_out1b"],
                             pw["norm"], pw["wq"])
    y_out = _peer(x3, xn3, pq3, pw).reshape(b, s, D_MODEL)
    return y_out, [ckv3, krope, wkv, pb3[:, -1], c_k, c_v, c_kidx, d_v]


def kernel(x_prompt, x_sample, cache_a_ckv, cache_a_krope, state_b_wkv, state_b_shift, cache_c_k, cache_c_v, cache_c_kidx, l0_norm_mix, l0_w_in, a_q_norm, a_w_uq, a_kv_norm, a_w_ukv, a_q_gain, a_k_gain, b_mu, b_w0, b_w2, b_a0, b_a2, b_g2, b_kk, b_ka, b_rk, b_ln_g, b_ln_b, l0_w_out, l0_norm_ffn, l0_peer_wq, l0_peer_keys, l0_peer_u, l0_peer_v, l1_norm_mix, l1_w_in, c_q_gain, c_k_gain, c_kidx_gain, d_ln_g, d_ln_b, d_ws, d_bs, l1_w_out, l1_norm_ffn, l1_peer_wq, l1_peer_keys, l1_peer_u, l1_peer_v):
    params = dict(
        l0_norm_mix=l0_norm_mix, l0_w_in=l0_w_in, a_q_norm=a_q_norm, a_w_uq=a_w_uq, a_kv_norm=a_kv_norm,
        a_w_ukv=a_w_ukv, a_q_gain=a_q_gain, a_k_gain=a_k_gain, b_mu=b_mu, b_w0=b_w0, b_w2=b_w2, b_a0=b_a0,
        b_a2=b_a2, b_g2=b_g2, b_kk=b_kk, b_ka=b_ka, b_rk=b_rk.reshape(-1), b_ln_g=b_ln_g, b_ln_b=b_ln_b,
        l0_w_out=l0_w_out, l0_norm_ffn=l0_norm_ffn, l0_peer_wq=l0_peer_wq, l0_peer_keys=l0_peer_keys,
        l0_peer_u=l0_peer_u, l0_peer_v=l0_peer_v, l1_norm_mix=l1_norm_mix, l1_w_in=l1_w_in, c_q_gain=c_q_gain,
        c_k_gain=c_k_gain, c_kidx_gain=c_kidx_gain, d_ln_g=d_ln_g, d_ln_b=d_ln_b, d_ws=d_ws, d_bs=d_bs,
        l1_w_out=l1_w_out, l1_norm_ffn=l1_norm_ffn, l1_peer_wq=l1_peer_wq, l1_peer_keys=l1_peer_keys,
        l1_peer_u=l1_peer_u, l1_peer_v=l1_peer_v)
    w = _prep_weights(params)
    bp = x_prompt.shape[0]
    dt = x_prompt.dtype
    prompt_state = dict(
        a_ckv=jnp.zeros((bp, 0, A_D_C), dt), a_krope=jnp.zeros((bp, 0, A_D_ROPE), dt),
        b_wkv=jnp.zeros((bp, B_HEADS, B_HD, B_HD), dt), b_shift=jnp.zeros((bp, B_PROJ), dt),
        c_k=jnp.zeros((bp, 0, C_KV_HEADS, C_HD), dt), c_v=jnp.zeros((bp, 0, C_KV_HEADS, C_HD), dt),
        c_kidx=jnp.zeros((bp, 0, C_IDX_D), dt))
    sample_state = dict(a_ckv=cache_a_ckv, a_krope=cache_a_krope, b_wkv=state_b_wkv, b_shift=state_b_shift,
                        c_k=cache_c_k, c_v=cache_c_v, c_kidx=cache_c_kidx)
    y_p, new_p = _trunk(x_prompt, prompt_state, w)
    y_s, new_s = _trunk(x_sample, sample_state, w)
    return (y_p, y_s, *new_p[:7], *new_s)
```

```python
import functools
import math

import jax
import jax.numpy as jnp
from jax import lax
from jax.experimental import pallas as pl
from jax.experimental.pallas import tpu as pltpu

F32 = jnp.float32
BF16 = jnp.bfloat16
I32 = jnp.int32

D_MODEL = 1024
CHUNK = 64
CHUNK_SHIFT = 6
KEY_BLOCK = 512
ROPE_THETA = 10000.0
NORM_EPS = 1e-6
NEG_INF = -1e30
LANES = 128
SUBLANES = 8

A_HEADS = 8
A_D_NOPE = 64
A_D_ROPE = 32
A_D_QK = A_D_NOPE + A_D_ROPE
A_D_V = 64
A_D_CQ = 256
A_D_C = 128
B_HEADS = 8
B_HD = 64
B_WIDTH = B_HEADS * B_HD
B_W_LORA = 64
B_A_LORA = 64
B_G_LORA = 128
B_PROJ = 3 * B_WIDTH + B_W_LORA + B_A_LORA + B_G_LORA
B_LN_EPS = 64e-5
RWKV_CHUNK = 128
C_HEADS = 8
C_KV_HEADS = 2
C_HD = 64
C_IDX_HEADS = 8
C_IDX_D = 64
C_TOPK = 256
D_GROUPS = 4
D_WIDTH = 512
D_SPAN = 128
PEER_HEADS = 8
PEER_N_KEYS = 128
PEER_N_EXPERTS = PEER_N_KEYS * PEER_N_KEYS
PEER_DK = 128
PEER_TOPK = 16
N_PAIR_CAND = 64

VMEM_LIMIT = 48 * 1024 * 1024
NT = (((1,), (1,)), ((), ()))


def _tile(n, target):
    t = min(n, target)
    while n % t:
        t -= 8
    return t


def _cparams(*sem):
    return pltpu.CompilerParams(dimension_semantics=sem, vmem_limit_bytes=VMEM_LIMIT)


def _dot(a, b):
    return jnp.dot(a, b, preferred_element_type=F32)


def _dot_nt(a, b):
    return lax.dot_general(a, b, NT, preferred_element_type=F32)


def _split_dot(x, w, passes=2):
    hi = x.astype(BF16)
    acc = _dot(hi, w)
    rem = x - hi.astype(F32)
    for _ in range(passes - 1):
        part = rem.astype(BF16)
        acc = acc + _dot(part, w)
        rem = rem - part.astype(F32)
    return acc


def _rms(x, eps=NORM_EPS):
    return x * lax.rsqrt(jnp.mean(x * x, axis=-1, keepdims=True) + eps)


def _full(shape):
    nd = len(shape)
    return pl.BlockSpec(shape, lambda *_: (0,) * nd)


def _block_ones(n, seg):
    i = jnp.arange(n)
    return (i[:, None] // seg == i[None, :] // seg).astype(BF16)


def _rope_perm(n, seg, off, half):
    r = jnp.arange(n)[:, None]
    c = jnp.arange(n)[None, :]
    same = r // seg == c // seg
    rr = r % seg - off
    cc = c % seg - off
    plus = same & (rr >= 0) & (rr < half) & (cc == rr + half)
    minus = same & (cc >= 0) & (cc < half) & (rr == cc + half)
    return (plus.astype(F32) - minus.astype(F32)).astype(BF16)


def _rope_tables(pos, half, seg, off, reps):
    inv_freq = ROPE_THETA ** (-jnp.arange(half, dtype=F32) / half)
    ang = pos.astype(F32)[:, None] * inv_freq[None, :]
    n = pos.shape[0]
    edge = ((0, 0), (off, seg - off - 2 * half))
    cos = jnp.pad(jnp.tile(jnp.cos(ang), (1, 2)), edge, constant_values=1.0)
    sin = jnp.pad(jnp.tile(jnp.sin(ang), (1, 2)), edge)
    return jnp.tile(cos, (1, reps)), jnp.tile(sin, (1, reps))


def _in0_kernel(x_ref, g_ref, wcq_ref, wckv_ref, wkr_ref, wb_ref, kvn_ref,
                cq_ref, ckv_ref, krp_ref, pb_ref):
    xn = (_rms(x_ref[...]) * g_ref[...]).astype(BF16)
    cq_ref[...] = _dot(xn, wcq_ref[...])
    ckv_ref[...] = _rms(_dot(xn, wckv_ref[...])) * kvn_ref[...]
    krp_ref[...] = _dot(xn, wkr_ref[...])
    pb_ref[...] = _dot(xn, wb_ref[...])


def _in_proj0(x2, w):
    t = x2.shape[0]
    tm = _tile(t, 512)
    row = lambda n: pl.BlockSpec((tm, n), lambda i: (i, 0))
    return pl.pallas_call(
        _in0_kernel,
        grid=(t // tm,),
        in_specs=[row(D_MODEL), _full((1, D_MODEL)), _full((D_MODEL, A_D_CQ)), _full((D_MODEL, A_D_C)),
                  _full((D_MODEL, LANES)), _full((D_MODEL, B_PROJ)), _full((1, A_D_C))],
        out_specs=[row(A_D_CQ), row(A_D_C), row(LANES), row(B_PROJ)],
        out_shape=[jax.ShapeDtypeStruct((t, A_D_CQ), F32), jax.ShapeDtypeStruct((t, A_D_C), F32),
                   jax.ShapeDtypeStruct((t, LANES), F32), jax.ShapeDtypeStruct((t, B_PROJ), F32)],
        compiler_params=_cparams("parallel"),
        name="in_proj0",
    )(x2, w["l0_norm"], w["w_cq"], w["w_ckv"], w["w_krp"], w["w_b"], w["a_kv_norm"])


def _head_norm_rope(xh, gain, cos, sin, perm):
    ss = jnp.sum(xh * xh, axis=-1, keepdims=True) * (1.0 / A_D_QK)
    xh = xh * lax.rsqrt(ss + NORM_EPS) * gain
    return xh * cos + _split_dot(xh, perm) * sin


def _mla_q_kernel(cq_ref, qn_ref, wq_ref, gain_ref, cos_ref, sin_ref, perm_ref, o_ref):
    cqn = (_rms(cq_ref[0]) * qn_ref[...]).astype(BF16)
    q = _dot(cqn, wq_ref[...])
    cos, sin, perm, gain = cos_ref[...], sin_ref[...], perm_ref[...], gain_ref[...]
    for h in range(A_HEADS):
        qh = _head_norm_rope(q[:, h * LANES:(h + 1) * LANES], gain, cos, sin, perm)
        o_ref[0, :, h * LANES:(h + 1) * LANES] = (qh * (A_D_QK ** -0.5)).astype(BF16)


def _mla_q(cq, w, cos, sin):
    b, s, _ = cq.shape
    ts = _tile(s, 256)
    return pl.pallas_call(
        _mla_q_kernel,
        grid=(b, s // ts),
        in_specs=[pl.BlockSpec((1, ts, A_D_CQ), lambda i, j: (i, j, 0)), _full((1, A_D_CQ)),
                  _full((A_D_CQ, A_HEADS * LANES)), _full((1, LANES)),
                  pl.BlockSpec((ts, LANES), lambda i, j: (j, 0)), pl.BlockSpec((ts, LANES), lambda i, j: (j, 0)),
                  _full((LANES, LANES))],
        out_specs=pl.BlockSpec((1, ts, A_HEADS * LANES), lambda i, j: (i, j, 0)),
        out_shape=jax.ShapeDtypeStruct((b, s, A_HEADS * LANES), BF16),
        compiler_params=_cparams("parallel", "parallel"),
        name="mla_q",
    )(cq, w["a_q_norm"], w["w_uq"], w["a_q_gain"], cos, sin, w["perm_a"])


def _mla_kv_kernel(ckv_ref, krp_ref, wk_ref, wv_ref, gain_ref, cos_ref, sin_ref, perm_ref, k_ref, v_ref):
    c = ckv_ref[0].astype(BF16)
    k = _dot(c, wk_ref[...])
    v_ref[0] = _dot(c, wv_ref[...]).astype(BF16)
    krp = krp_ref[0]
    cos, sin, perm, gain = cos_ref[...], sin_ref[...], perm_ref[...], gain_ref[...]
    for h in range(A_HEADS):
        kh = _head_norm_rope(k[:, h * LANES:(h + 1) * LANES] + krp, gain, cos, sin, perm)
        k_ref[0, :, h * LANES:(h + 1) * LANES] = kh.astype(BF16)


def _mla_kv(ckv_all, krp_all, w, cos, sin):
    b, lp, _ = ckv_all.shape
    tl = _tile(lp, 256)
    blk = lambda n: pl.BlockSpec((1, tl, n), lambda i, j: (i, j, 0))
    tab = pl.BlockSpec((tl, LANES), lambda i, j: (j, 0))
    return pl.pallas_call(
        _mla_kv_kernel,
        grid=(b, lp // tl),
        in_specs=[blk(A_D_C), blk(LANES), _full((A_D_C, A_HEADS * LANES)), _full((A_D_C, A_HEADS * LANES)),
                  _full((1, LANES)), tab, tab, _full((LANES, LANES))],
        out_specs=[blk(A_HEADS * LANES), blk(A_HEADS * LANES)],
        out_shape=[jax.ShapeDtypeStruct((b, lp, A_HEADS * LANES), BF16)] * 2,
        compiler_params=_cparams("parallel", "parallel"),
        name="mla_kv",
    )(ckv_all, krp_all, w["w_uk"], w["w_uv"], w["a_k_gain"], cos, sin, w["perm_a"])


def _visible(tq, lp, q0, true_len):
    qpos = q0 + lax.broadcasted_iota(I32, (tq, 1), 0)
    kpos = lax.broadcasted_iota(I32, (1, lp), 1)
    kchunk = jnp.where(kpos < true_len, kpos >> CHUNK_SHIFT, jnp.int32(1 << 30))
    return kchunk <= (qpos >> CHUNK_SHIFT)


def _masked_softmax(s, keep):
    s = jnp.where(keep, s, NEG_INF)
    p = jnp.exp(s - jnp.max(s, axis=-1, keepdims=True))
    return p, 1.0 / jnp.sum(p, axis=-1, keepdims=True)


def _for_visible_prefix(run, tq, lp, past):
    q0 = past + pl.program_id(1) * tq
    need = (((q0 + tq - 1) >> CHUNK_SHIFT) + 1) << CHUNK_SHIFT
    sizes = [min(c * KEY_BLOCK, lp) for c in range(1, -(-lp // KEY_BLOCK) + 1)]
    case = (need + (KEY_BLOCK - 1)) // KEY_BLOCK
    for c, n in enumerate(sizes, start=1):
        pl.when(case == c)(functools.partial(run, n, q0))


def _mla_attn_kernel(q_ref, k_ref, v_ref, o_ref, *, past, true_len):
    tq = q_ref.shape[1]

    def run(n, q0):
        vis = _visible(tq, n, q0, true_len)
        for m in range(A_HEADS // 2):
            acc = None
            for h in (2 * m, 2 * m + 1):
                sl = slice(h * LANES, (h + 1) * LANES)
                p, inv = _masked_softmax(_dot_nt(q_ref[0, :, sl], k_ref[0, :n, sl]), vis)
                o = _dot(p.astype(BF16), v_ref[0, :n, sl]) * inv
                acc = o if acc is None else acc + o
            o_ref[0, :, m * LANES:(m + 1) * LANES] = acc

    _for_visible_prefix(run, tq, k_ref.shape[1], past)


def _mla_attn(q, k, v, past, true_len):
    b, s, _ = q.shape
    lp = k.shape[1]
    tq = _tile(s, 256)
    kv = pl.BlockSpec((1, lp, A_HEADS * LANES), lambda i, j: (i, 0, 0))
    return pl.pallas_call(
        functools.partial(_mla_attn_kernel, past=past, true_len=true_len),
        grid=(b, s // tq),
        in_specs=[pl.BlockSpec((1, tq, A_HEADS * LANES), lambda i, j: (i, j, 0)), kv, kv],
        out_specs=pl.BlockSpec((1, tq, A_HEADS * A_D_V), lambda i, j: (i, j, 0)),
        out_shape=jax.ShapeDtypeStruct((b, s, A_HEADS * A_D_V), F32),
        compiler_params=_cparams("parallel", "arbitrary"),
        name="mla_attn",
    )(q, k, v)


def _rwkv_prep_kernel(pb_ref, shift_ref, mu_ref, w0_ref, w2_ref, a0_ref, a2_ref, g2_ref, kk_ref, ka_ref, rk_ref,
                      bd_ref, r_ref, w_ref, k_ref, v_ref, an_ref, bv_ref, g_ref, bonus_ref, last_ref):
    pb = pb_ref[0]
    ts = pb.shape[0]

    @pl.when(pl.program_id(1) == 0)
    def _():
        last_ref[...] = shift_ref[0]

    first = lax.broadcasted_iota(I32, (ts, 1), 0) == 0
    prev = jnp.where(first, last_ref[...], pltpu.roll(pb, 1, axis=0))
    last_ref[...] = pb[ts - 1:ts, :]
    xs = pb + (prev - pb) * mu_ref[...]
    r = xs[:, 0:B_WIDTH]
    k = xs[:, B_WIDTH:2 * B_WIDTH]
    v = xs[:, 2 * B_WIDTH:3 * B_WIDTH]
    wa = xs[:, 3 * B_WIDTH:3 * B_WIDTH + B_W_LORA + B_A_LORA]
    gl = xs[:, 3 * B_WIDTH + B_W_LORA + B_A_LORA:]
    bd = bd_ref[...]
    z = -(w0_ref[...] + _dot(jnp.tanh(wa).astype(BF16), w2_ref[...]))
    softplus = jnp.maximum(z, 0.0) + jnp.log(1.0 + jnp.exp(-jnp.abs(z)))
    w_ref[0] = -jnp.exp(-softplus - 0.5)
    a = jax.nn.sigmoid(a0_ref[...] + _dot(wa.astype(BF16), a2_ref[...]))
    g_ref[0] = _dot(jax.nn.sigmoid(gl).astype(BF16), g2_ref[...])
    kk = k * kk_ref[...]
    kk = kk * lax.rsqrt(_split_dot(kk * kk, bd) + 1e-12)
    k2 = k * (1.0 + (a - 1.0) * ka_ref[...])
    r_ref[0] = r
    k_ref[0] = k2
    v_ref[0] = v
    an_ref[0] = -kk
    bv_ref[0] = kk * a
    bonus_ref[0] = _split_dot(r * k2 * rk_ref[...], bd) * v


def _rwkv_prep(pb3, shift, w):
    b, s, _ = pb3.shape
    ts = _tile(s, 256)
    blk = lambda n: pl.BlockSpec((1, ts, n), lambda i, j: (i, j, 0))
    vec = _full((1, B_WIDTH))
    return pl.pallas_call(
        _rwkv_prep_kernel,
        grid=(b, s // ts),
        in_specs=[blk(B_PROJ), pl.BlockSpec((1, 1, B_PROJ), lambda i, j: (i, 0, 0)), _full((1, B_PROJ)), vec,
                  _full((LANES, B_WIDTH)), vec, _full((LANES, B_WIDTH)), _full((B_G_LORA, B_WIDTH)), vec, vec, vec,
                  _full((B_WIDTH, B_WIDTH))],
        out_specs=[blk(B_WIDTH)] * 8,
        out_shape=[jax.ShapeDtypeStruct((b, s, B_WIDTH), F32)] * 8,
        scratch_shapes=[pltpu.VMEM((1, B_PROJ), F32)],
        compiler_params=_cparams("parallel", "arbitrary"),
        name="rwkv_prep",
    )(pb3, shift[:, None, :], w["b_mu"], w["b_w0"], w["b_w2"], w["b_a0"], w["b_a2"], w["b_g2"], w["b_kk"], w["b_ka"],
      w["b_rk"], w["bd512"])


def _rwkv_scan_kernel(r_ref, lw_ref, k_ref, v_ref, an_ref, bv_ref, s0_ref, y_ref, sout_ref, st_ref):
    nb, c, _ = r_ref.shape
    pairs = B_HEADS // 2

    @pl.when(pl.program_id(1) == 0)
    def _():
        st_ref[...] = s0_ref[...]

    bf = lambda x: x.astype(BF16)
    step = lax.broadcasted_iota(I32, (c, c), 0)
    prev = lax.broadcasted_iota(I32, (c, c), 1)
    lower = prev <= step
    strict = prev < step
    tril = bf(jnp.where(lower, 1.0, 0.0))
    head_of_lane = lax.broadcasted_iota(I32, (1, LANES), 1) >> 6
    same_head = (lax.broadcasted_iota(I32, (LANES, LANES), 0) >> 6) == (lax.broadcasted_iota(I32, (LANES, LANES), 1) >> 6)
    levels = max(1, (c - 1).bit_length())
    pair_ids = [(b, p) for b in range(nb) for p in range(pairs)]
    head_ids = [(q, hh) for q in range(len(pair_ids)) for hh in range(2)]
    load = lambda ref: [ref[b, :, p * LANES:(p + 1) * LANES] for b, p in pair_ids]
    r, lw, k, v, an, bv = (load(ref) for ref in (r_ref, lw_ref, k_ref, v_ref, an_ref, bv_ref))
    pieces = []
    rem = lw
    for _ in range(3):
        pieces.append([bf(x) for x in rem])
        rem = [x - pc.astype(F32) for x, pc in zip(rem, pieces[-1])]
    logp = [_dot(tril, pc) for pc in pieces[0]]
    for level in pieces[1:]:
        logp = [acc + _dot(tril, pc) for acc, pc in zip(logp, level)]
    p_inc = [jnp.exp(x) for x in logp]
    p_inv = [jnp.exp(-x) for x in logp]
    a_t = [a * jnp.exp(lp - l) for a, lp, l in zip(an, logp, lw)]
    b_t = [bf(x * pi) for x, pi in zip(bv, p_inv)]
    k_t = [bf(x * pi) for x, pi in zip(k, p_inv)]
    r_t = [x * pi for x, pi in zip(r, p_inc)]
    s2 = [st_ref[b, p] for b, p in pair_ids]
    s2b = [bf(x) for x in s2]
    a_s = [_dot_nt(bf(x), s) for x, s in zip(a_t, s2b)]
    r_s = [_dot_nt(bf(x), s) for x, s in zip(r_t, s2b)]
    mine = [head_of_lane == hh for _, hh in head_ids]
    only = lambda xs: [bf(jnp.where(m, xs[q], 0.0)) for (q, _), m in zip(head_ids, mine)]
    a_h, r_h, v_h = only(a_t), only(r_t), only(v)
    n_pow = [bf(jnp.where(strict, _dot_nt(a, b_t[q]), 0.0)) for a, (q, _) in zip(a_h, head_ids)]
    l_ak = [bf(jnp.where(strict, _dot_nt(a, k_t[q]), 0.0)) for a, (q, _) in zip(a_h, head_ids)]
    m_rb = [bf(jnp.where(lower, _dot_nt(x, b_t[q]), 0.0)) for x, (q, _) in zip(r_h, head_ids)]
    m_rk = [bf(jnp.where(lower, _dot_nt(x, k_t[q]), 0.0)) for x, (q, _) in zip(r_h, head_ids)]
    x = [jnp.where(m, a_s[q], 0.0) + _dot(l, vh) for (q, _), m, l, vh in zip(head_ids, mine, l_ak, v_h)]
    for lv in range(levels):
        x = [xi + _dot(n, bf(xi)) for xi, n in zip(x, n_pow)]
        if lv + 1 < levels:
            n_pow = [bf(_dot(n, n)) for n in n_pow]
    y_h = [jnp.where(m, r_s[q], 0.0) + _dot(mb, bf(xi)) + _dot(mk, vh)
           for (q, _), m, mb, mk, xi, vh in zip(head_ids, mine, m_rb, m_rk, x, v_h)]
    for q, (b, p) in enumerate(pair_ids):
        y_ref[b, :, p * LANES:(p + 1) * LANES] = y_h[2 * q] + y_h[2 * q + 1]
        uv_t = bf(jnp.concatenate([x[2 * q] + x[2 * q + 1], v[q]], axis=0).T)
        add = _dot(uv_t, jnp.concatenate([b_t[q], k_t[q]], axis=0))
        st_ref[b, p] = (s2[q] + jnp.where(same_head, add, 0.0)) * p_inc[q][c - 1:c, :]

    @pl.when(pl.program_id(1) == pl.num_programs(1) - 1)
    def _():
        sout_ref[...] = st_ref[...]


def _rwkv_scan(seqs, s0):
    b, s, _ = seqs[0].shape
    nb = 2 if b % 2 == 0 else 1
    c = _tile(s, RWKV_CHUNK)
    pairs = B_HEADS // 2
    seq = pl.BlockSpec((nb, c, B_WIDTH), lambda i, j: (i, j, 0))
    sts = pl.BlockSpec((nb, pairs, LANES, LANES), lambda i, j: (i, 0, 0, 0))
    return pl.pallas_call(
        _rwkv_scan_kernel,
        grid=(b // nb, s // c),
        in_specs=[seq] * 6 + [sts],
        out_specs=[seq, sts],
        out_shape=[jax.ShapeDtypeStruct((b, s, B_WIDTH), F32), jax.ShapeDtypeStruct((b, pairs, LANES, LANES), F32)],
        scratch_shapes=[pltpu.VMEM((nb, pairs, LANES, LANES), F32)],
        compiler_params=_cparams("parallel", "arbitrary"),
        name="rwkv_scan",
    )(*seqs, s0)


def _rwkv_post_kernel(y_ref, bonus_ref, g_ref, lg_ref, lb_ref, bd_ref, o_ref):
    bd = bd_ref[...]
    y = y_ref[...]
    d = y - _split_dot(y, bd) * (1.0 / B_HD)
    var = _split_dot(d * d, bd) * (1.0 / B_HD)
    yn = d * lax.rsqrt(var + B_LN_EPS) * lg_ref[...] + lb_ref[...]
    o_ref[...] = (yn + bonus_ref[...]) * g_ref[...]


def _rwkv_post(y2, bonus2, g2, w):
    t = y2.shape[0]
    tm = _tile(t, 512)
    row = pl.BlockSpec((tm, B_WIDTH), lambda i: (i, 0))
    vec = _full((1, B_WIDTH))
    return pl.pallas_call(
        _rwkv_post_kernel,
        grid=(t // tm,),
        in_specs=[row, row, row, vec, vec, _full((B_WIDTH, B_WIDTH))],
        out_specs=row,
        out_shape=jax.ShapeDtypeStruct((t, B_WIDTH), F32),
        compiler_params=_cparams("parallel"),
        name="rwkv_post",
    )(y2, bonus2, g2, w["b_ln_g"], w["b_ln_b"], w["bd512"])


def _out_proj_kernel(x_ref, o1_ref, o2_ref, w1_ref, w2_ref, gf_ref, wq_ref, xo_ref, xn_ref, q_ref):
    x = x_ref[...] + _dot(o1_ref[...].astype(BF16), w1_ref[...]) + _dot(o2_ref[...].astype(BF16), w2_ref[...])
    xo_ref[...] = x
    xn = (_rms(x) * gf_ref[...]).astype(BF16)
    xn_ref[...] = xn
    q_ref[...] = _dot(xn, wq_ref[...])


def _out_proj(x2, o1, o2, w1, w2, gf, wq):
    t = x2.shape[0]
    tm = _tile(t, 512)
    row = lambda n: pl.BlockSpec((tm, n), lambda i: (i, 0))
    half = o1.shape[1]
    return pl.pallas_call(
        _out_proj_kernel,
        grid=(t // tm,),
        in_specs=[row(D_MODEL), row(half), row(half), _full((half, D_MODEL)), _full((half, D_MODEL)),
                  _full((1, D_MODEL)), _full((D_MODEL, PEER_HEADS * PEER_DK))],
        out_specs=[row(D_MODEL), row(D_MODEL), row(PEER_HEADS * PEER_DK)],
        out_shape=[jax.ShapeDtypeStruct((t, D_MODEL), F32), jax.ShapeDtypeStruct((t, D_MODEL), BF16),
                   jax.ShapeDtypeStruct((t, PEER_HEADS * PEER_DK), F32)],
        compiler_params=_cparams("parallel"),
        name="out_proj",
    )(x2, o1, o2, w1, w2, gf, wq)


def _top16_rows(sc_ref, val_ref, idx_ref):
    nc, n, tb = sc_ref.shape
    rowi = lax.broadcasted_iota(I32, (n, tb), 0).astype(F32)

    def body(r, carry):
        for c in range(nc):
            sc = sc_ref[c]
            m = jnp.max(sc, axis=0, keepdims=True)
            ix = jnp.min(jnp.where(sc == m, rowi, float(n)), axis=0, keepdims=True)
            val_ref[c, pl.ds(r, 1), :] = m
            idx_ref[c, pl.ds(r, 1), :] = ix
            sc_ref[c] = jnp.where(rowi == ix, -jnp.inf, sc)
        return carry

    lax.fori_loop(0, PEER_TOPK, body, 0)


def _pair_candidates(s1, s2, x1, x2):
    subi = lax.broadcasted_iota(I32, (SUBLANES, s1.shape[1]), 0)
    sub = subi.astype(F32)
    low4 = subi < 4
    b4 = (subi & 3).astype(F32)
    two = lambda z, a: jnp.where(low4, z[a:a + 1], z[a + 1:a + 2])
    dup4 = lambda z: jnp.where(low4, z[0:8], pltpu.roll(z[0:8], 4, axis=0))
    cand = [s1[0:1] + s2[0:8], s1[0:1] + s2[8:16]]
    flat = [sub, 8 + sub]
    eid = [x1[0:1] * PEER_N_KEYS + x2[0:8], x1[0:1] * PEER_N_KEYS + x2[8:16]]
    for a in (1, 2, 3):
        cand.append(s1[a:a + 1] + s2[0:8])
        flat.append(a * PEER_TOPK + sub)
        eid.append(x1[a:a + 1] * PEER_N_KEYS + x2[0:8])
    for a in (4, 6):
        cand.append(two(s1, a) + dup4(s2))
        flat.append(jnp.where(low4, float(a * PEER_TOPK), float((a + 1) * PEER_TOPK)) + b4)
        eid.append(two(x1, a) * PEER_N_KEYS + dup4(x2))
    cand.append(s1[8:16] + s2[0:1])
    flat.append((8 + sub) * PEER_TOPK)
    eid.append(x1[8:16] * PEER_N_KEYS + x2[0:1])
    return jnp.concatenate(cand, axis=0), jnp.concatenate(flat, axis=0), jnp.concatenate(eid, axis=0)


def _peer_route_kernel(q_ref, keys_ref, e1_ref, e2_ref, gate_ref, sc_ref, topv, topi, cand_ref, eid_ref, selv, sele):
    for h in range(PEER_HEADS):
        qh = q_ref[:, h * PEER_DK:(h + 1) * PEER_DK].astype(BF16)
        sc_ref[2 * h] = _dot_nt(keys_ref[h, 0], qh)
        sc_ref[2 * h + 1] = _dot_nt(keys_ref[h, 1], qh)
    _top16_rows(sc_ref, topv, topi)
    for h in range(PEER_HEADS):
        cand, flat, eid = _pair_candidates(topv[2 * h], topv[2 * h + 1], topi[2 * h], topi[2 * h + 1])
        cand_ref[h] = cand
        eid_ref[h] = eid

    def body(r, carry):
        for h in range(PEER_HEADS):
            cand = cand_ref[h]
            m = jnp.max(cand, axis=0, keepdims=True)
            f = jnp.min(jnp.where(cand == m, flat, 1e9), axis=0, keepdims=True)
            hit = flat == f
            selv[h, pl.ds(r, 1), :] = m
            sele[h, pl.ds(r, 1), :] = jnp.max(jnp.where(hit, eid_ref[h], -1.0), axis=0, keepdims=True)
            cand_ref[h] = jnp.where(hit, -jnp.inf, cand)
        return carry

    lax.fori_loop(0, PEER_TOPK, body, 0)
    for h in range(PEER_HEADS):
        rows = slice(h * PEER_TOPK, (h + 1) * PEER_TOPK)
        top = selv[h]
        e = jnp.exp(top - top[0:1])
        gate_ref[rows, :] = e * (1.0 / jnp.sum(e, axis=0, keepdims=True))
        ex = sele[h].astype(I32)
        e1_ref[rows, :] = ex >> 7
        e2_ref[rows, :] = ex & (PEER_N_KEYS - 1)


def _peer_route(q2, keys_pad):
    t = q2.shape[0]
    tb = _tile(t, 256)
    nk = PEER_HEADS * PEER_TOPK
    out = pl.BlockSpec((nk, tb), lambda i: (0, i))
    return pl.pallas_call(
        _peer_route_kernel,
        grid=(t // tb,),
        in_specs=[pl.BlockSpec((tb, PEER_HEADS * PEER_DK), lambda i: (i, 0)),
                  _full((PEER_HEADS, 2, PEER_N_KEYS, PEER_DK))],
        out_specs=[out, out, out],
        out_shape=[jax.ShapeDtypeStruct((nk, t), I32)] * 2 + [jax.ShapeDtypeStruct((nk, t), F32)],
        scratch_shapes=[pltpu.VMEM((2 * PEER_HEADS, PEER_N_KEYS, tb), F32),
                        pltpu.VMEM((2 * PEER_HEADS, PEER_TOPK, tb), F32), pltpu.VMEM((2 * PEER_HEADS, PEER_TOPK, tb), F32),
                        pltpu.VMEM((PEER_HEADS, N_PAIR_CAND, tb), F32), pltpu.VMEM((PEER_HEADS, N_PAIR_CAND, tb), F32),
                        pltpu.VMEM((PEER_HEADS, PEER_TOPK, tb), F32), pltpu.VMEM((PEER_HEADS, PEER_TOPK, tb), F32)],
        compiler_params=_cparams("parallel"),
        name="peer_route",
    )(q2, keys_pad)


def _peer_gate_kernel(e1_ref, e2_ref, gate_ref, o_ref):
    tg = e1_ref.shape[0]
    nk = PEER_HEADS * PEER_TOPK
    sub = lax.broadcasted_iota(I32, (PEER_N_KEYS, nk), 0)
    grp = 2 * SUBLANES

    def body(c, carry):
        t0 = pl.multiple_of(c * grp, grp)
        e1 = e1_ref[pl.ds(t0, grp), :]
        e2 = e2_ref[pl.ds(t0, grp), :]
        g = gate_ref[pl.ds(t0, grp), :]
        halves = []
        for j0 in range(0, grp, SUBLANES):
            mats = []
            for j in range(j0, j0 + SUBLANES):
                a_t = jnp.where(sub == e1[j:j + 1], g[j:j + 1], 0.0).astype(BF16)
                b_t = jnp.where(sub == e2[j:j + 1], 1.0, 0.0).astype(BF16)
                mats.append(_dot_nt(a_t, b_t))
            halves.append(jnp.swapaxes(jnp.stack(mats, axis=0), 0, 1))
        o_ref[:, pl.ds(t0, grp), :] = jnp.concatenate(halves, axis=1).astype(BF16)
        return carry

    lax.fori_loop(0, tg // grp, body, 0, unroll=4)


def _peer_gates(e1, e2, gate):
    t = e1.shape[0]
    tg = _tile(t, 128)
    assert tg % (8 * SUBLANES) == 0, "the gate kernel writes its output in unrolled groups of 64 tokens"
    row = pl.BlockSpec((tg, PEER_HEADS * PEER_TOPK), lambda i: (i, 0))
    return pl.pallas_call(
        _peer_gate_kernel,
        grid=(t // tg,),
        in_specs=[row, row, row],
        out_specs=pl.BlockSpec((PEER_N_KEYS, tg, PEER_N_KEYS), lambda i: (0, i, 0)),
        out_shape=jax.ShapeDtypeStruct((PEER_N_KEYS, t, PEER_N_KEYS), BF16),
        compiler_params=_cparams("parallel"),
        name="peer_gates",
    )(e1, e2, gate)


def _gelu(x):
    return 0.5 * x * (1.0 + lax.erf(x * (2.0 ** -0.5)))


def _peer_dense_kernel(xn_ref, u_ref, v_ref, g_ref, x_ref, o_ref):
    @pl.when(pl.program_id(1) == 0)
    def _():
        o_ref[...] = x_ref[...]

    h = _gelu(_dot_nt(xn_ref[...], u_ref[...]))
    nblk = g_ref.shape[0]
    gh = [(h[:, c * LANES:(c + 1) * LANES] * g_ref[c].astype(F32)).astype(BF16) for c in range(nblk)]
    o_ref[...] += _dot(jnp.concatenate(gh, axis=-1), v_ref[...])


def _peer_dense(xn, ut, vtab, gates, x2):
    t = xn.shape[0]
    tb = _tile(t, 1024)
    eb = 1024
    return pl.pallas_call(
        _peer_dense_kernel,
        grid=(t // tb, PEER_N_EXPERTS // eb),
        in_specs=[pl.BlockSpec((tb, D_MODEL), lambda i, j: (i, 0)),
                  pl.BlockSpec((eb, D_MODEL), lambda i, j: (j, 0)),
                  pl.BlockSpec((eb, D_MODEL), lambda i, j: (j, 0)),
                  pl.BlockSpec((eb // PEER_N_KEYS, tb, PEER_N_KEYS), lambda i, j: (j, i, 0)),
                  pl.BlockSpec((tb, D_MODEL), lambda i, j: (i, 0))],
        out_specs=pl.BlockSpec((tb, D_MODEL), lambda i, j: (i, 0)),
        out_shape=jax.ShapeDtypeStruct((t, D_MODEL), F32),
        compiler_params=_cparams("parallel", "arbitrary"),
        name="peer_dense",
    )(xn, ut, vtab, gates, x2)


def _peer(x2, xn, q2, pw):
    e1, e2, gate = _peer_route(q2, pw["keys"])
    gates = _peer_gates(e1.T, e2.T, gate.T)
    return _peer_dense(xn, pw["u"], pw["v"], gates, x2)


def _in1_kernel(x_ref, g_ref, wq_ref, wk_ref, wv_ref, wiq_ref, wik_ref, wiw_ref, wd_ref,
                q_ref, k_ref, v_ref, iq_ref, ik_ref, iw_ref, d_ref):
    xn = (_rms(x_ref[...]) * g_ref[...]).astype(BF16)
    for w_ref, o_ref in ((wq_ref, q_ref), (wk_ref, k_ref), (wv_ref, v_ref), (wiq_ref, iq_ref),
                         (wik_ref, ik_ref), (wiw_ref, iw_ref), (wd_ref, d_ref)):
        o_ref[...] = _dot(xn, w_ref[...])


def _in_proj1(x2, w):
    t = x2.shape[0]
    tm = _tile(t, 512)
    widths = [C_HEADS * C_HD, LANES, LANES, C_IDX_HEADS * C_IDX_D, LANES, LANES, 2 * D_WIDTH]
    row = lambda n: pl.BlockSpec((tm, n), lambda i: (i, 0))
    return pl.pallas_call(
        _in1_kernel,
        grid=(t // tm,),
        in_specs=[row(D_MODEL), _full((1, D_MODEL))] + [_full((D_MODEL, n)) for n in widths],
        out_specs=[row(n) for n in widths],
        out_shape=[jax.ShapeDtypeStruct((t, n), F32) for n in widths],
        compiler_params=_cparams("parallel"),
        name="in_proj1",
    )(x2, w["l1_norm"], w["w_cq1"], w["w_ck1"], w["w_cv1"], w["w_ciq"], w["w_cik"], w["w_ciw"], w["w_d"])


def _dsa_prep_kernel(q_ref, k_ref, iq_ref, ik_ref, qg_ref, kg_ref, ig_ref, cos_ref, sin_ref, bd_ref, perm_ref,
                     qo_ref, ko_ref, iqo_ref, iko_ref):
    cos, sin, bd, perm = cos_ref[...], sin_ref[...], bd_ref[...], perm_ref[...]

    def norm(x, gain, n):
        ss = _split_dot(x * x, bd[:n, :n]) * (1.0 / C_HD)
        return x * lax.rsqrt(ss + NORM_EPS) * gain

    def rope(x, n):
        return x * cos[:, :n] + _split_dot(x, perm[:n, :n]) * sin[:, :n]

    nq = C_HEADS * C_HD
    qo_ref[0] = (rope(norm(q_ref[0], qg_ref[...], nq), nq) * (C_HD ** -0.5)).astype(BF16)
    ko_ref[0] = rope(norm(k_ref[0], kg_ref[...], LANES), LANES)
    iqo_ref[0] = (rope(iq_ref[0], nq) * (C_IDX_D ** -0.5)).astype(BF16)
    iko_ref[0] = rope(norm(ik_ref[0], ig_ref[...], LANES), LANES)


def _dsa_prep(q, k, iq, ik, w, cos, sin):
    b, s, _ = q.shape
    ts = _tile(s, 256)
    nq = C_HEADS * C_HD
    blk = lambda n: pl.BlockSpec((1, ts, n), lambda i, j: (i, j, 0))
    tab = pl.BlockSpec((ts, nq), lambda i, j: (j, 0))
    return pl.pallas_call(
        _dsa_prep_kernel,
        grid=(b, s // ts),
        in_specs=[blk(nq), blk(LANES), blk(nq), blk(LANES), _full((1, nq)), _full((1, LANES)), _full((1, LANES)),
                  tab, tab, _full((nq, nq)), _full((nq, nq))],
        out_specs=[blk(nq), blk(LANES), blk(nq), blk(LANES)],
        out_shape=[jax.ShapeDtypeStruct((b, s, nq), BF16), jax.ShapeDtypeStruct((b, s, LANES), F32),
                   jax.ShapeDtypeStruct((b, s, nq), BF16), jax.ShapeDtypeStruct((b, s, LANES), F32)],
        compiler_params=_cparams("parallel", "parallel"),
        name="dsa_prep",
    )(q, k, iq, ik, w["c_q_gain"], w["c_k_gain"], w["c_kidx_gain"], cos, sin, w["bd512"], w["perm_c"])


KEY_OF_NEG_INF = -(2 ** 31) + 0x7FFFFF


def _key_to_float(key):
    return pltpu.bitcast(jnp.where(key < 0, key ^ jnp.int32(0x7FFFFFFF), key), F32)


def _topk_mask(score, n_sel, su):
    tq, lp = score.shape

    def body(i, tau):
        trial = tau + lax.shift_left(jnp.int32(1), 31 - i)
        cnt = jnp.sum(jnp.where(score >= _key_to_float(trial), 1.0, 0.0), axis=-1, keepdims=True)
        return jnp.where(trial <= KEY_OF_NEG_INF, trial, jnp.where(cnt >= n_sel, trial, tau))

    kth = _key_to_float(lax.fori_loop(0, 32, body, jnp.full((tq, 1), -(2 ** 31), I32)))
    gt = jnp.where(score > kth, 1.0, 0.0)
    eq = jnp.where(score == kth, 1.0, 0.0)
    need = n_sel - jnp.sum(gt, axis=-1, keepdims=True)
    parts = []
    before = jnp.zeros((tq, 1), F32)
    for c in range(lp // LANES):
        eqc = eq[:, c * LANES:(c + 1) * LANES]
        rank = before + _dot(eqc.astype(BF16), su)
        parts.append(gt[:, c * LANES:(c + 1) * LANES] + jnp.where(rank < need, eqc, 0.0))
        before = before + jnp.sum(eqc, axis=-1, keepdims=True)
    return jnp.concatenate(parts, axis=-1)


def _dsa_attn_kernel(q_ref, iq_ref, iw_ref, ik_ref, k_ref, v_ref, su_ref, o_ref, *, past, true_len, n_sel):
    tq = q_ref.shape[1]
    low = lax.broadcasted_iota(I32, (1, LANES), 1) < C_HD
    halves = lambda x: (jnp.where(low, x, jnp.zeros((), x.dtype)), jnp.where(low, jnp.zeros((), x.dtype), x))

    def run(n, q0):
        vis = _visible(tq, n, q0, true_len)
        iw = iw_ref[0] * (C_IDX_HEADS ** -0.5)
        ik2 = ik_ref[0, :n]
        score = jnp.zeros((tq, n), F32)
        for m in range(C_IDX_HEADS // 2):
            iqa, iqb = halves(iq_ref[0, :, m * LANES:(m + 1) * LANES])
            score = score + jnp.maximum(_dot_nt(iqa, ik2), 0.0) * iw[:, 2 * m:2 * m + 1]
            score = score + jnp.maximum(_dot_nt(iqb, ik2), 0.0) * iw[:, 2 * m + 1:2 * m + 2]
        score = jnp.where(vis, score, NEG_INF)
        keep = jnp.where(vis, _topk_mask(score, n_sel, su_ref[...]), 0.0) > 0.5
        k = k_ref[0, :n]
        va, vb = halves(v_ref[0, :n])
        for m in range(C_HEADS // 2):
            qa, qb = halves(q_ref[0, :, m * LANES:(m + 1) * LANES])
            pa, inva = _masked_softmax(_dot_nt(qa, k), keep)
            pb, invb = _masked_softmax(_dot_nt(qb, k), keep)
            o_ref[0, :, m * LANES:(m + 1) * LANES] = (_dot(pa.astype(BF16), va) * inva
                                                      + _dot(pb.astype(BF16), vb) * invb)

    _for_visible_prefix(run, tq, ik_ref.shape[1], past)


def _dsa_attn(q, iq, iw, ik2, k, v, su, past, true_len):
    b, s, nq = q.shape
    lp = ik2.shape[1]
    tq = _tile(s, 256)
    n_sel = min(C_TOPK, true_len // 4)
    qblk = lambda n: pl.BlockSpec((1, tq, n), lambda i, j: (i, j, 0))
    keys = pl.BlockSpec((1, lp, LANES), lambda i, j: (i, 0, 0))
    return pl.pallas_call(
        functools.partial(_dsa_attn_kernel, past=past, true_len=true_len, n_sel=n_sel),
        grid=(b, s // tq),
        in_specs=[qblk(nq), qblk(nq), qblk(LANES), keys, keys, keys, _full((LANES, LANES))],
        out_specs=qblk(nq),
        out_shape=jax.ShapeDtypeStruct((b, s, nq), F32),
        compiler_params=_cparams("parallel", "arbitrary"),
        name="dsa_attn",
    )(q, iq, iw, ik2, k, v, su)


def _gmlp_kernel(pd_ref, lg_ref, lb_ref, ws_ref, bs_ref, o_ref, dv_ref):
    span = pd_ref.shape[1]
    h = _gelu(pd_ref[0])
    u = h[:, :D_WIDTH]
    v = h[:, D_WIDTH:]
    mu = jnp.mean(v, axis=-1, keepdims=True)
    d = v - mu
    v = d * lax.rsqrt(jnp.mean(d * d, axis=-1, keepdims=True) + NORM_EPS) * lg_ref[...] + lb_ref[...]
    dv_ref[0] = v
    causal = lax.broadcasted_iota(I32, (span, span), 1) <= lax.broadcasted_iota(I32, (span, span), 0)
    gw = D_WIDTH // D_GROUPS
    mixed = [_dot(jnp.where(causal, ws_ref[g], 0.0).astype(BF16), v[:, g * gw:(g + 1) * gw].astype(BF16))
             for g in range(D_GROUPS)]
    o_ref[0] = u * (jnp.concatenate(mixed, axis=-1) + bs_ref[...])


def _gmlp(pd, w):
    b, s, _ = pd.shape
    span = D_SPAN if s % D_SPAN == 0 else s
    ws = w["d_ws"][:, :span, :span]
    bs = jnp.repeat(w["d_bs"][:, :span].T, D_WIDTH // D_GROUPS, axis=1)
    blk = lambda n: pl.BlockSpec((1, span, n), lambda i, j: (i, j, 0))
    return pl.pallas_call(
        _gmlp_kernel,
        grid=(b, s // span),
        in_specs=[blk(2 * D_WIDTH), _full((1, D_WIDTH)), _full((1, D_WIDTH)), _full((D_GROUPS, span, span)),
                  _full((span, D_WIDTH))],
        out_specs=[blk(D_WIDTH), blk(D_WIDTH)],
        out_shape=[jax.ShapeDtypeStruct((b, s, D_WIDTH), F32)] * 2,
        compiler_params=_cparams("parallel", "parallel"),
        name="gmlp",
    )(pd, w["d_ln_g"], w["d_ln_b"], ws, bs)


def _pad_cols(w, groups, width, slot, off=0):
    k = w.shape[0]
    w = jnp.pad(w.reshape(k, groups, width), ((0, 0), (0, 0), (off, slot - off - width)))
    return w.reshape(k, groups * slot)


def _prep_weights(p):
    row = lambda v: v.reshape(1, -1).astype(F32)
    bf = lambda v: v.astype(BF16)
    w = {}
    w_in0 = p["l0_w_in"]
    w["l0_norm"] = row(p["l0_norm_mix"])
    w["w_cq"] = bf(w_in0[:, :A_D_CQ])
    w["w_ckv"] = bf(w_in0[:, A_D_CQ:A_D_CQ + A_D_C])
    a_cols = A_D_CQ + A_D_C + A_D_ROPE
    w["w_krp"] = bf(_pad_cols(w_in0[:, A_D_CQ + A_D_C:a_cols], 1, A_D_ROPE, LANES, A_D_NOPE))
    w["w_b"] = bf(w_in0[:, a_cols:])
    w["a_kv_norm"] = row(p["a_kv_norm"])
    w["a_q_norm"] = row(p["a_q_norm"])
    w["w_uq"] = bf(_pad_cols(p["a_w_uq"], A_HEADS, A_D_QK, LANES))
    ukv = p["a_w_ukv"].reshape(A_D_C, A_HEADS, A_D_NOPE + A_D_V)
    w["w_uk"] = bf(_pad_cols(ukv[:, :, :A_D_NOPE].reshape(A_D_C, -1), A_HEADS, A_D_NOPE, LANES))
    uv = ukv[:, :, A_D_NOPE:].reshape(A_D_C, A_HEADS // 2, 2, A_D_V)
    slot = lambda x, off: jnp.pad(x, ((0, 0), (0, 0), (off, LANES - off - A_D_V)))
    uv_pad = jnp.stack([slot(uv[:, :, 0], 0), slot(uv[:, :, 1], A_D_V)], axis=2)
    w["w_uv"] = bf(uv_pad.reshape(A_D_C, A_HEADS * LANES))
    w["a_q_gain"] = jnp.pad(row(p["a_q_gain"]), ((0, 0), (0, LANES - A_D_QK)))
    w["a_k_gain"] = jnp.pad(row(p["a_k_gain"]), ((0, 0), (0, LANES - A_D_QK)))
    w["perm_a"] = _rope_perm(LANES, LANES, A_D_NOPE, A_D_ROPE // 2)
    w["b_mu"] = row(p["b_mu"])
    w["b_w0"] = row(p["b_w0"])
    w["b_a0"] = row(p["b_a0"])
    zeros = jnp.zeros((B_W_LORA, B_WIDTH), F32)
    w["b_w2"] = bf(jnp.concatenate([p["b_w2"], zeros], axis=0))
    w["b_a2"] = bf(jnp.concatenate([zeros, p["b_a2"]], axis=0))
    w["b_g2"] = bf(p["b_g2"])
    for n in ("b_kk", "b_ka", "b_rk", "b_ln_g", "b_ln_b"):
        w[n] = row(p[n])
    w["bd512"] = _block_ones(B_WIDTH, B_HD)
    w["w_out0a"] = bf(p["l0_w_out"][:A_HEADS * A_D_V])
    w["w_out0b"] = bf(p["l0_w_out"][A_HEADS * A_D_V:])
    w_in1 = p["l1_w_in"]
    nq = C_HEADS * C_HD
    nkv = C_KV_HEADS * C_HD
    o = 0
    order = jnp.arange(C_HEADS).reshape(C_KV_HEADS, C_HEADS // C_KV_HEADS).T.reshape(-1)
    w["w_cq1"] = bf(w_in1[:, o:o + nq].reshape(D_MODEL, C_HEADS, C_HD)[:, order].reshape(D_MODEL, nq)); o += nq
    w["w_ck1"] = bf(w_in1[:, o:o + nkv]); o += nkv
    w["w_cv1"] = bf(w_in1[:, o:o + nkv]); o += nkv
    w["w_ciq"] = bf(w_in1[:, o:o + nq]); o += nq
    w["w_cik"] = bf(_pad_cols(w_in1[:, o:o + C_IDX_D], 1, C_IDX_D, LANES)); o += C_IDX_D
    w["w_ciw"] = bf(_pad_cols(w_in1[:, o:o + C_IDX_HEADS], 1, C_IDX_HEADS, LANES)); o += C_IDX_HEADS
    w["w_d"] = bf(w_in1[:, o:])
    w["l1_norm"] = row(p["l1_norm_mix"])
    w["c_q_gain"] = row(jnp.tile(p["c_q_gain"], C_HEADS))
    w["c_k_gain"] = row(jnp.tile(p["c_k_gain"], C_KV_HEADS))
    w["c_kidx_gain"] = jnp.pad(row(p["c_kidx_gain"]), ((0, 0), (0, LANES - C_IDX_D)))
    w["perm_c"] = _rope_perm(nq, C_HD, 0, C_HD // 2)
    w["su"] = (jnp.arange(LANES)[:, None] < jnp.arange(LANES)[None, :]).astype(BF16)
    w["d_ln_g"] = row(p["d_ln_g"])
    w["d_ln_b"] = row(p["d_ln_b"])
    w["d_ws"] = p["d_ws"]
    w["d_bs"] = p["d_bs"]
    w["w_out1a"] = bf(p["l1_w_out"][:nq].reshape(C_HEADS, C_HD, D_MODEL)[order].reshape(nq, D_MODEL))
    w["w_out1b"] = bf(p["l1_w_out"][nq:])
    for l in (0, 1):
        keys = p[f"l{l}_peer_keys"]
        half = PEER_DK // 2
        edge = lambda lo, hi: ((0, 0), (0, 0), (lo, hi))
        kp = jnp.stack([jnp.pad(keys[:, 0], edge(0, half)), jnp.pad(keys[:, 1], edge(half, 0))], axis=1)
        w[f"peer{l}"] = {"keys": bf(kp), "u": bf(p[f"l{l}_peer_u"]), "v": bf(p[f"l{l}_peer_v"]),
                         "norm": row(p[f"l{l}_norm_ffn"]), "wq": bf(p[f"l{l}_peer_wq"])}
    return w


def _pad_keys(x, lp):
    return jnp.pad(x, ((0, 0), (0, lp - x.shape[1])) + ((0, 0),) * (x.ndim - 2))


def _trunk(x, st, w):
    b, s, _ = x.shape
    t = b * s
    past = st["a_ckv"].shape[1]
    true_len = past + s
    lp = -(-true_len // LANES) * LANES
    kpos = jnp.arange(lp, dtype=I32)

    cq, ckv, krp, pb = _in_proj0(x.reshape(t, D_MODEL), w)
    cos_a, sin_a = _rope_tables(kpos, A_D_ROPE // 2, LANES, A_D_NOPE, 1)
    q = _mla_q(cq.reshape(b, s, A_D_CQ), w, cos_a[past:true_len], sin_a[past:true_len])
    ckv3 = ckv.reshape(b, s, A_D_C)
    krp3 = krp.reshape(b, s, LANES)
    krope = krp3[:, :, A_D_NOPE:A_D_NOPE + A_D_ROPE]
    cache_krp = jnp.pad(st["a_krope"], ((0, 0), (0, 0), (A_D_NOPE, LANES - A_D_NOPE - A_D_ROPE)))
    ckv_all = _pad_keys(jnp.concatenate([st["a_ckv"], ckv3], axis=1), lp)
    krp_all = _pad_keys(jnp.concatenate([cache_krp, krp3], axis=1), lp)
    k_a, v_a = _mla_kv(ckv_all, krp_all, w, cos_a, sin_a)
    o_a = _mla_attn(q, k_a, v_a, past, true_len)

    pb3 = pb.reshape(b, s, B_PROJ)
    r, logw, k2, v, an, bv, g, bonus = _rwkv_prep(pb3, st["b_shift"], w)
    pairs = B_HEADS // 2
    sp = st["b_wkv"].reshape(b, pairs, 2, B_HD, B_HD)
    edge = lambda lo, hi: ((0, 0), (0, 0), (0, 0), (lo, hi))
    s0 = jnp.concatenate([jnp.pad(sp[:, :, 0], edge(0, B_HD)), jnp.pad(sp[:, :, 1], edge(B_HD, 0))], axis=2)
    y, s_out = _rwkv_scan([r, logw, k2, v, an, bv], s0)
    wkv = jnp.stack([s_out[:, :, :B_HD, :B_HD], s_out[:, :, B_HD:, B_HD:]], axis=2).reshape(b, B_HEADS, B_HD, B_HD)
    o_b = _rwkv_post(y.reshape(t, B_WIDTH), bonus.reshape(t, B_WIDTH), g.reshape(t, B_WIDTH), w)
    pw = w["peer0"]
    x1, xn1, pq1 = _out_proj(x.reshape(t, D_MODEL), o_a.reshape(t, -1), o_b, w["w_out0a"], w["w_out0b"],
                             pw["norm"], pw["wq"])
    x2 = _peer(x1, xn1, pq1, pw)

    nq = C_HEADS * C_HD
    cq1, ck1, cv1, ciq, cik, ciw, pd = _in_proj1(x2, w)
    pos_q = kpos[past:true_len]
    cos_c, sin_c = _rope_tables(pos_q, C_HD // 2, C_HD, 0, C_HEADS)
    three = lambda a: a.reshape(b, s, a.shape[-1])
    q_c, k_c, iq_c, ik_c = _dsa_prep(three(cq1), three(ck1), three(ciq), three(cik), w, cos_c, sin_c)
    c_k = k_c.reshape(b, s, C_KV_HEADS, C_HD)
    c_v = cv1.reshape(b, s, C_KV_HEADS, C_HD)
    c_kidx = ik_c[:, :, :C_IDX_D]
    flat_kv = lambda cache, new: _pad_keys(
        jnp.concatenate([cache.reshape(b, past, LANES), new], axis=1), lp).astype(BF16)
    ik_all = jnp.concatenate([st["c_kidx"], c_kidx], axis=1)
    ik2 = _pad_keys(jnp.concatenate([ik_all, ik_all], axis=-1), lp).astype(BF16)
    o_c = _dsa_attn(q_c, iq_c, three(ciw), ik2, flat_kv(st["c_k"], k_c), flat_kv(st["c_v"], three(cv1)),
                    w["su"], past, true_len)
    o_d, d_v = _gmlp(three(pd), w)
    pw = w["peer1"]
    x3, xn3, pq3 = _out_proj(x2, o_c.reshape(t, nq), o_d.reshape(t, D_WIDTH), w["w_out1a"], w["w_out1b"],
                             pw["norm"], pw["wq"])
    y_out = _peer(x3, xn3, pq3, pw).reshape(b, s, D_MODEL)
    return y_out, [ckv3, krope, wkv, pb3[:, -1], c_k, c_v, c_kidx, d_v]


def kernel(x_prompt, x_sample, cache_a_ckv, cache_a_krope, state_b_wkv, state_b_shift, cache_c_k, cache_c_v, cache_c_kidx, l0_norm_mix, l0_w_in, a_q_norm, a_w_uq, a_kv_norm, a_w_ukv, a_q_gain, a_k_gain, b_mu, b_w0, b_w2, b_a0, b_a2, b_g2, b_kk, b_ka, b_rk, b_ln_g, b_ln_b, l0_w_out, l0_norm_ffn, l0_peer_wq, l0_peer_keys, l0_peer_u, l0_peer_v, l1_norm_mix, l1_w_in, c_q_gain, c_k_gain, c_kidx_gain, d_ln_g, d_ln_b, d_ws, d_bs, l1_w_out, l1_norm_ffn, l1_peer_wq, l1_peer_keys, l1_peer_u, l1_peer_v):
    params = dict(
        l0_norm_mix=l0_norm_mix, l0_w_in=l0_w_in, a_q_norm=a_q_norm, a_w_uq=a_w_uq, a_kv_norm=a_kv_norm,
        a_w_ukv=a_w_ukv, a_q_gain=a_q_gain, a_k_gain=a_k_gain, b_mu=b_mu, b_w0=b_w0, b_w2=b_w2, b_a0=b_a0,
        b_a2=b_a2, b_g2=b_g2, b_kk=b_kk, b_ka=b_ka, b_rk=b_rk.reshape(-1), b_ln_g=b_ln_g, b_ln_b=b_ln_b,
        l0_w_out=l0_w_out, l0_norm_ffn=l0_norm_ffn, l0_peer_wq=l0_peer_wq, l0_peer_keys=l0_peer_keys,
        l0_peer_u=l0_peer_u, l0_peer_v=l0_peer_v, l1_norm_mix=l1_norm_mix, l1_w_in=l1_w_in, c_q_gain=c_q_gain,
        c_k_gain=c_k_gain, c_kidx_gain=c_kidx_gain, d_ln_g=d_ln_g, d_ln_b=d_ln_b, d_ws=d_ws, d_bs=d_bs,
        l1_w_out=l1_w_out, l1_norm_ffn=l1_norm_ffn, l1_peer_wq=l1_peer_wq, l1_peer_keys=l1_peer_keys,
        l1_peer_u=l1_peer_u, l1_peer_v=l1_peer_v)
    w = _prep_weights(params)
    bp = x_prompt.shape[0]
    dt = x_prompt.dtype
    prompt_state = dict(
        a_ckv=jnp.zeros((bp, 0, A_D_C), dt), a_krope=jnp.zeros((bp, 0, A_D_ROPE), dt),
        b_wkv=jnp.zeros((bp, B_HEADS, B_HD, B_HD), dt), b_shift=jnp.zeros((bp, B_PROJ), dt),
        c_k=jnp.zeros((bp, 0, C_KV_HEADS, C_HD), dt), c_v=jnp.zeros((bp, 0, C_KV_HEADS, C_HD), dt),
        c_kidx=jnp.zeros((bp, 0, C_IDX_D), dt))
    sample_state = dict(a_ckv=cache_a_ckv, a_krope=cache_a_krope, b_wkv=state_b_wkv, b_shift=state_b_shift,
                        c_k=cache_c_k, c_v=cache_c_v, c_kidx=cache_c_kidx)
    y_p, new_p = _trunk(x_prompt, prompt_state, w)
    y_s, new_s = _trunk(x_sample, sample_state, w)
    return (y_p, y_s, *new_p[:7], *new_s)
```
